```python
import jax, jax.numpy as jnp
from jax import lax
import numpy as np

D_MODEL = 2048
BATCH = 4
SEQ = 2048
DEPTH = 2

GRID_W = 64
CTX_LEN = 256
N_BRANCH = 4
BRANCH_W = D_MODEL // N_BRANCH
HEAD_DIM = 64
N_HEADS = BRANCH_W // HEAD_DIM
N_KV_HEADS = 2
WINDOW = 128
BLOCK = 128
ROPE_THETA = 10000.0
SGU_CHUNK = 128
SGU_GROUPS = 4
SGU_GROUP_W = BRANCH_W // SGU_GROUPS
SCONV_K = 3
CONF_K = 31
N_EXPERTS = 16
N_GROUPS = 4
EXPERTS_PER_GROUP = N_EXPERTS // N_GROUPS
TOP_K = 2
D_FF_EXPERT = D_MODEL // 2
Q_W = N_HEADS * HEAD_DIM
KV_W = N_KV_HEADS * HEAD_DIM
SECTION_WIDTHS = (Q_W, KV_W, KV_W, BRANCH_W, BRANCH_W, BRANCH_W, BRANCH_W, BRANCH_W, BRANCH_W, BRANCH_W, N_BRANCH * D_MODEL)
N_IN = Q_W + 2 * KV_W + 7 * BRANCH_W + N_BRANCH * D_MODEL

kernel_name = "hybrid_parallel_gated_mixers_grouped_moe_dit"

F32 = jnp.float32


def rmsnorm(x, g, eps=1e-6):
    xf = x.astype(F32)
    y = xf * lax.rsqrt(jnp.mean(xf * xf, axis=-1, keepdims=True) + eps)
    return (y * g.astype(F32)).astype(x.dtype)


def layernorm(x, g, b, eps=1e-5):
    xf = x.astype(F32)
    mu = jnp.mean(xf, axis=-1, keepdims=True)
    var = jnp.mean(jnp.square(xf - mu), axis=-1, keepdims=True)
    y = (xf - mu) * lax.rsqrt(var + eps) * g.astype(F32) + b.astype(F32)
    return y.astype(x.dtype)


def split_sections(p):
    pts, acc = [], 0
    for w in SECTION_WIDTHS[:-1]:
        acc += w
        pts.append(acc)
    return jnp.split(p, pts, axis=-1)


def axial_rope_tables(n):
    rows = n // GRID_W
    row = jnp.broadcast_to(jnp.arange(rows)[:, None], (rows, GRID_W)).reshape(-1)
    col = jnp.broadcast_to(jnp.arange(GRID_W)[None, :], (rows, GRID_W)).reshape(-1)
    half = HEAD_DIM // 2
    inv = 1.0 / (ROPE_THETA ** (jnp.arange(0, half, 2, dtype=F32) / half))
    ang_r = row.astype(F32)[:, None, None] * inv
    ang_c = col.astype(F32)[:, None, None] * inv
    return jnp.cos(ang_r), jnp.sin(ang_r), jnp.cos(ang_c), jnp.sin(ang_c)


def rotate(x, cos, sin):
    xf = x.astype(F32)
    x1, x2 = jnp.split(xf, 2, axis=-1)
    return jnp.concatenate([x1 * cos - x2 * sin, x2 * cos + x1 * sin], axis=-1).astype(x.dtype)


def axial_rope(x, tabs):
    cr, sr, cc, sc = tabs
    half = HEAD_DIM // 2
    return jnp.concatenate([rotate(x[..., :half], cr, sr), rotate(x[..., half:], cc, sc)], axis=-1)


def window_attention(q, k, v, kc, vc, sink):
    b, s, h, dh = q.shape
    g = k.shape[2]
    r = h // g
    nb = s // BLOCK
    scale = dh ** -0.5
    qb = q.reshape(b, nb, BLOCK, g, r, dh)

    def band(t):
        tp = jnp.pad(t, ((0, 0), (BLOCK, BLOCK), (0, 0), (0, 0))).reshape(b, nb + 2, BLOCK, g, dh)
        return jnp.concatenate([tp[:, :-2], tp[:, 1:-1], tp[:, 2:]], axis=2)

    kb, vb = band(k), band(v)
    s_loc = jnp.einsum('bnqgrd,bnkgd->bngrqk', qb, kb).astype(F32) * scale
    qpos = jnp.arange(nb)[:, None] * BLOCK + jnp.arange(BLOCK)[None, :]
    kpos = jnp.arange(nb)[:, None] * BLOCK - BLOCK + jnp.arange(3 * BLOCK)[None, :]
    valid = (jnp.abs(qpos[:, :, None] - kpos[:, None, :]) <= WINDOW) & (kpos[:, None, :] >= 0) & (kpos[:, None, :] < s)
    s_loc = jnp.where(valid[None, :, None, None], s_loc, -jnp.inf)
    s_ctx = jnp.einsum('bnqgrd,bmgd->bngrqm', qb, kc).astype(F32) * scale
    sink_b = jnp.broadcast_to(sink.astype(F32).reshape(g, r)[None, None, :, :, None, None], s_loc.shape[:-1] + (1,))
    p = jax.nn.softmax(jnp.concatenate([s_loc, s_ctx, sink_b], axis=-1), axis=-1)
    nk = 3 * BLOCK
    lc = kc.shape[1]
    p_loc = p[..., :nk].astype(v.dtype)
    p_ctx = p[..., nk:nk + lc].astype(v.dtype)
    o = jnp.einsum('bngrqk,bnkgd->bnqgrd', p_loc, vb) + jnp.einsum('bngrqm,bmgd->bnqgrd', p_ctx, vc)
    return o.reshape(b, s, h * dh)


def context_attention(qc, kc, vc, sink):
    b, l, h, dh = qc.shape
    g = kc.shape[2]
    r = h // g
    qg = qc.reshape(b, l, g, r, dh)
    s = jnp.einsum('blgrd,bmgd->bgrlm', qg, kc).astype(F32) * (dh ** -0.5)
    sink_b = jnp.broadcast_to(sink.astype(F32).reshape(g, r)[None, :, :, None, None], s.shape[:-1] + (1,))
    p = jax.nn.softmax(jnp.concatenate([s, sink_b], axis=-1), axis=-1)
    o = jnp.einsum('bgrlm,bmgd->blgrd', p[..., :l].astype(vc.dtype), vc)
    return o.reshape(b, l, h * dh)


def chunk_sgu(u, v, ln_g, ln_b, w_s, b_s):
    b, t, w = u.shape
    u = jax.nn.gelu(u)
    v = layernorm(jax.nn.gelu(v), ln_g, ln_b)
    vr = v.reshape(b, t // SGU_CHUNK, SGU_CHUNK, SGU_GROUPS, SGU_GROUP_W)
    sp = jnp.einsum('gpq,bnqgc->bnpgc', w_s, vr) + b_s.T[:, :, None]
    return u * sp.reshape(b, t, w)


def dwconv(x, w):
    kw = w.shape[0]
    return lax.conv_general_dilated(x, w[:, None, :].astype(x.dtype), window_strides=(1,),
                                    padding=((kw // 2, kw // 2),),
                                    dimension_numbers=('NWC', 'WIO', 'NWC'),
                                    feature_group_count=x.shape[-1])


def conformer_conv(a, bgate, dw_w, dw_b, ln_g, ln_b):
    z = a * jax.nn.sigmoid(bgate)
    z = dwconv(z, dw_w) + dw_b
    z = layernorm(z, ln_g, ln_b)
    return jax.nn.silu(z)


def branch_mix(attn_y, su, sv, cb, cc, cx, ga, gb, gates,
               sgu_ln_g, sgu_ln_b, sgu_w, sgu_b, sconv_w,
               conf_dw_w, conf_dw_b, conf_ln_g, conf_ln_b, w_branch, w_out):
    y_b = chunk_sgu(su, sv, sgu_ln_g, sgu_ln_b, sgu_w, sgu_b)
    y_c = cb * dwconv(cc * cx, sconv_w)
    y_d = conformer_conv(ga, gb, conf_dw_w, conf_dw_b, conf_ln_g, conf_ln_b)
    gates = jax.nn.sigmoid(gates.astype(F32)).astype(attn_y.dtype)
    merged = 0.0
    for i, y in enumerate((attn_y, y_b, y_c, y_d)):
        merged = merged + gates[..., i * D_MODEL:(i + 1) * D_MODEL] * (y @ w_branch[i])
    return merged @ w_out


def moe(h, router_w, router_b, w_up, w_down):
    t = h.shape[0]
    scores = jax.nn.softmax((h @ router_w).astype(F32), axis=-1)
    biased = scores + router_b.astype(F32)
    grp_score = lax.top_k(biased.reshape(t, N_GROUPS, EXPERTS_PER_GROUP), TOP_K)[0].sum(-1)
    gsel = jnp.argmax(grp_score, axis=-1)
    in_group = (jnp.arange(N_EXPERTS) // EXPERTS_PER_GROUP)[None, :] == gsel[:, None]
    _, idx = lax.top_k(jnp.where(in_group, biased, -jnp.inf), TOP_K)
    wsel = jnp.take_along_axis(scores, idx, axis=-1)
    wsel = wsel / jnp.sum(wsel, axis=-1, keepdims=True)
    comb = jnp.sum(jax.nn.one_hot(idx, N_EXPERTS, dtype=F32) * wsel[..., None], axis=1).astype(h.dtype)
    y = jnp.zeros_like(h)
    for e in range(N_EXPERTS):
        a, b = jnp.split(h @ w_up[e], 2, axis=-1)
        y = y + comb[:, e:e + 1] * ((jax.nn.silu(a) * b) @ w_down[e])
    return y


def setup_inputs(seed: int = 0) -> dict:
    key = jax.random.key(seed)
    ks = jax.random.split(key, 32)

    def nrm(k, shape, scale):
        return jax.random.normal(k, shape, jnp.float32) * scale

    D = D_MODEL
    return {
        "x": nrm(ks[0], (BATCH, SEQ, D), 1.0),
        "c": nrm(ks[1], (BATCH, D), 1.0),
        "ctx": nrm(ks[2], (BATCH, CTX_LEN, D), 1.0),
        "c_ctx": nrm(ks[3], (D,), 1.0),
        "ada_w": nrm(ks[4], (DEPTH, D, 6 * D), 0.3 * D ** -0.5),
        "ada_b": nrm(ks[5], (DEPTH, 6 * D), 0.01),
        "norm1_g": 1.0 + nrm(ks[6], (DEPTH, D), 0.02),
        "norm2_g": 1.0 + nrm(ks[7], (DEPTH, D), 0.02),
        "w_in": nrm(ks[8], (DEPTH, D, N_IN), D ** -0.5),
        "attn_sink": nrm(ks[9], (DEPTH, N_HEADS), 0.5),
        "sgu_ln_g": 1.0 + nrm(ks[10], (DEPTH, BRANCH_W), 0.02),
        "sgu_ln_b": nrm(ks[11], (DEPTH, BRANCH_W), 0.02),
        "sgu_w": nrm(ks[12], (DEPTH, SGU_GROUPS, SGU_CHUNK, SGU_CHUNK), SGU_CHUNK ** -0.5),
        "sgu_b": 1.0 + nrm(ks[13], (DEPTH, SGU_GROUPS, SGU_CHUNK), 0.02),
        "sconv_w": nrm(ks[14], (DEPTH, SCONV_K, BRANCH_W), SCONV_K ** -0.5),
        "conf_dw_w": nrm(ks[15], (DEPTH, CONF_K, BRANCH_W), CONF_K ** -0.5),
        "conf_dw_b": nrm(ks[16], (DEPTH, BRANCH_W), 0.02),
        "conf_ln_g": 1.0 + nrm(ks[17], (DEPTH, BRANCH_W), 0.02),
        "conf_ln_b": nrm(ks[18], (DEPTH, BRANCH_W), 0.02),
        "w_branch": nrm(ks[19], (DEPTH, N_BRANCH, BRANCH_W, D), BRANCH_W ** -0.5),
        "w_out": nrm(ks[20], (DEPTH, D, D), D ** -0.5),
        "router_w": nrm(ks[21], (D, N_EXPERTS), D ** -0.5),
        "router_b": nrm(ks[22], (N_EXPERTS,), 0.01),
        "exp_w_up": nrm(ks[23], (DEPTH, N_EXPERTS, D, 2 * D_FF_EXPERT), D ** -0.5),
        "exp_w_down": nrm(ks[24], (DEPTH, N_EXPERTS, D_FF_EXPERT, D), D_FF_EXPERT ** -0.5),
        "final_g": 1.0 + nrm(ks[25], (D,), 0.02),
    }


def reference(x, c, ctx, c_ctx, ada_w, ada_b, norm1_g, norm2_g, w_in, attn_sink,
              sgu_ln_g, sgu_ln_b, sgu_w, sgu_b, sconv_w, conf_dw_w, conf_dw_b,
              conf_ln_g, conf_ln_b, w_branch, w_out, router_w, router_b,
              exp_w_up, exp_w_down, final_g):
    b, s, d = x.shape
    l_ctx = ctx.shape[1]
    tabs = axial_rope_tables(s)
    cx = ctx
    silu_c = jax.nn.silu(c)
    silu_cc = jax.nn.silu(c_ctx)[None, :]
    for l in range(DEPTH):
        last = l == DEPTH - 1
        sh1, sc1, g1, sh2, sc2, g2 = jnp.split((silu_c @ ada_w[l] + ada_b[l])[:, None, :], 6, axis=-1)
        sh1c, sc1c, g1c, sh2c, sc2c, g2c = jnp.split((silu_cc @ ada_w[l] + ada_b[l])[:, None, :], 6, axis=-1)
        lp = (sgu_ln_g[l], sgu_ln_b[l], sgu_w[l], sgu_b[l], sconv_w[l],
              conf_dw_w[l], conf_dw_b[l], conf_ln_g[l], conf_ln_b[l], w_branch[l], w_out[l])

        hx = rmsnorm(x, norm1_g[l]) * (1 + sc1) + sh1
        hc = rmsnorm(cx, norm1_g[l]) * (1 + sc1c) + sh1c
        px = split_sections(hx @ w_in[l])
        pc = split_sections(hc @ w_in[l])
        kc = pc[1].reshape(b, l_ctx, N_KV_HEADS, HEAD_DIM)
        vc = pc[2].reshape(b, l_ctx, N_KV_HEADS, HEAD_DIM)
        q = axial_rope(px[0].reshape(b, s, N_HEADS, HEAD_DIM), tabs)
        k = axial_rope(px[1].reshape(b, s, N_KV_HEADS, HEAD_DIM), tabs)
        v = px[2].reshape(b, s, N_KV_HEADS, HEAD_DIM)
        ya = window_attention(q, k, v, kc, vc, attn_sink[l])
        x_new = x + g1 * branch_mix(ya, *px[3:], *lp)
        if not last:
            yac = context_attention(pc[0].reshape(b, l_ctx, N_HEADS, HEAD_DIM), kc, vc, attn_sink[l])
            cx = cx + g1c * branch_mix(yac, *pc[3:], *lp)
        x = x_new

        hx2 = (rmsnorm(x, norm2_g[l]) * (1 + sc2) + sh2).reshape(b * s, d)
        if not last:
            hc2 = (rmsnorm(cx, norm2_g[l]) * (1 + sc2c) + sh2c).reshape(b * l_ctx, d)
            tokens = jnp.concatenate([hx2, hc2], axis=0)
        else:
            tokens = hx2
        yt = moe(tokens, router_w, router_b, exp_w_up[l], exp_w_down[l])
        x = x + g2 * yt[:b * s].reshape(b, s, d)
        if not last:
            cx = cx + g2c * yt[b * s:].reshape(b, l_ctx, d)
    return rmsnorm(x, final_g)
```

```python
import functools

import numpy as np
import jax
import jax.numpy as jnp
from jax import lax
from jax.experimental import pallas as pl
from jax.experimental.pallas import tpu as pltpu

F32 = jnp.float32
BF16 = jnp.bfloat16
I32 = jnp.int32

D = 2048
NB = 4
S = 2048
LC = 256
DEPTH = 2
GRID_W = 64
BW = 512
HD = 64
NH = 8
NKV = 2
REP = NH // NKV
WINDOW = 128
QB = 128
ROPE_THETA = 10000.0
SGU_CHUNK = 128
SGU_GROUPS = 4
SCONV_K = 3
CONF_K = 31
NE = 16
NG = 4
EPG = NE // NG
DFF = D // 2
N_IN = BW + 2 * NKV * HD + 7 * BW + 4 * D
TX = NB * S
TC = NB * LC
T = TX + TC

OFF_Q, OFF_KV = 0, 512
OFF_SU, OFF_SV, OFF_CB, OFF_CC, OFF_CX, OFF_GA, OFF_GB, OFF_GATES = 768, 1280, 1792, 2304, 2816, 3328, 3840, 4352

LANES = 128
V7X_VMEM_BYTES = 64 * 1024 * 1024
MIB = 1024 * 1024

TM_IN = 512
TN_IN = 1792
TM_TOK = 256
TM_E = 256
HALO = 16
NT_E = (2 * T) // TM_E + NE


def _cparams(sem, vmem_mib):
    return pltpu.CompilerParams(dimension_semantics=sem, vmem_limit_bytes=vmem_mib * MIB)


def _sigmoid(x):
    return 1.0 / (1.0 + jnp.exp(-x))


def _gelu_tanh(x):
    c = np.float32(np.sqrt(2.0 / np.pi))
    return 0.5 * x * (1.0 + jnp.tanh(c * (x + np.float32(0.044715) * (x * x * x))))


def _layernorm(x, g, b, eps=1e-5):
    mu = jnp.mean(x, axis=-1, keepdims=True)
    xc = x - mu
    var = jnp.mean(xc * xc, axis=-1, keepdims=True)
    return xc * lax.rsqrt(var + eps) * g + b


def _mod_row(i, tm):
    return jnp.where(i < TX // tm, (i * tm) // S, NB)


def _ada_body(c_ref, w_ref, b_ref, o_ref):
    c = c_ref[...]
    s = (c * _sigmoid(c)).astype(BF16)
    o_ref[...] = jnp.dot(s, w_ref[...].astype(BF16), preferred_element_type=F32) + b_ref[...]


def _ada(cvec, ada_w, ada_b):
    tn = 1024
    return pl.pallas_call(
        _ada_body,
        out_shape=jax.ShapeDtypeStruct((DEPTH, 8, 6 * D), F32),
        grid=(DEPTH, 6 * D // tn),
        in_specs=[
            pl.BlockSpec((8, D), lambda l, j: (0, 0)),
            pl.BlockSpec((None, D, tn), lambda l, j: (l, 0, j)),
            pl.BlockSpec((None, 1, tn), lambda l, j: (l, 0, j)),
        ],
        out_specs=pl.BlockSpec((None, 8, tn), lambda l, j: (l, 0, j)),
        compiler_params=_cparams(("arbitrary", "arbitrary"), 40),
        name="ada",
    )(cvec, ada_w, ada_b.reshape(DEPTH, 1, 6 * D))


def _norm_mod_body(x_ref, g_ref, mod_ref, o_ref, *, sh_idx, sc_idx):
    x = x_ref[...]
    y = x * lax.rsqrt(jnp.mean(x * x, axis=-1, keepdims=True) + 1e-6) * g_ref[...]
    o_ref[...] = (y * (1.0 + mod_ref[sc_idx:sc_idx + 1, :]) + mod_ref[sh_idx:sh_idx + 1, :]).astype(o_ref.dtype)


def _norm_mod(x, g, mods, layer, n_rows, sh_idx, sc_idx):
    tm = TM_IN
    return pl.pallas_call(
        functools.partial(_norm_mod_body, sh_idx=sh_idx, sc_idx=sc_idx),
        out_shape=jax.ShapeDtypeStruct((n_rows, D), BF16),
        grid=(n_rows // tm,),
        in_specs=[
            pl.BlockSpec((tm, D), lambda i: (i, 0)),
            pl.BlockSpec((1, D), lambda i: (0, 0)),
            pl.BlockSpec((None, None, 6, D), lambda i: (layer, _mod_row(i, tm), 0, 0)),
        ],
        out_specs=pl.BlockSpec((tm, D), lambda i: (i, 0)),
        compiler_params=_cparams(("arbitrary",), 32),
        name="norm_mod",
    )(x, g.reshape(1, D), mods)


def _final_norm_body(x_ref, g_ref, o_ref):
    x = x_ref[...]
    o_ref[...] = x * lax.rsqrt(jnp.mean(x * x, axis=-1, keepdims=True) + 1e-6) * g_ref[...]


def _final_norm(x, g):
    tm = TM_IN
    return pl.pallas_call(
        _final_norm_body,
        out_shape=jax.ShapeDtypeStruct((TX, D), F32),
        grid=(TX // tm,),
        in_specs=[pl.BlockSpec((tm, D), lambda i: (i, 0)), pl.BlockSpec((1, D), lambda i: (0, 0))],
        out_specs=pl.BlockSpec((tm, D), lambda i: (i, 0)),
        compiler_params=_cparams(("arbitrary",), 32),
        name="final_norm",
    )(x, g.reshape(1, D))


def _inproj_body(h_ref, w_ref, o_ref, wbf_ref):
    j = pl.program_id(0)
    i = pl.program_id(1)

    @pl.when(i == 0)
    def _():
        wbf_ref[...] = w_ref[...].astype(BF16)

    acc = jnp.dot(h_ref[...], wbf_ref[...], preferred_element_type=F32)
    col0 = j * TN_IN

    @pl.when(col0 + TN_IN <= OFF_GB)
    def _():
        o_ref[...] = acc.astype(BF16)

    @pl.when(col0 >= OFF_GB)
    def _():
        o_ref[...] = _sigmoid(acc).astype(BF16)

    @pl.when(jnp.logical_and(col0 < OFF_GB, col0 + TN_IN > OFF_GB))
    def _():
        col = col0 + lax.broadcasted_iota(I32, acc.shape, 1)
        o_ref[...] = jnp.where(col >= OFF_GB, _sigmoid(acc), acc).astype(BF16)


def _inproj(h, w_in, layer, n_rows, n_col_tiles):
    tm = TM_IN
    return pl.pallas_call(
        _inproj_body,
        out_shape=jax.ShapeDtypeStruct((n_rows, n_col_tiles * TN_IN), BF16),
        grid=(n_col_tiles, n_rows // tm),
        in_specs=[
            pl.BlockSpec((tm, D), lambda j, i: (i, 0)),
            pl.BlockSpec((None, D, TN_IN), lambda j, i: (layer, 0, j)),
        ],
        out_specs=pl.BlockSpec((tm, TN_IN), lambda j, i: (i, j)),
        scratch_shapes=[pltpu.VMEM((D, TN_IN), BF16)],
        compiler_params=_cparams(("arbitrary", "arbitrary"), 56),
        name="inproj",
    )(h, w_in)


def _rope_constants():
    rh = np.zeros((HD, HD), np.float32)
    for base in (0, 32):
        for d in range(16):
            rh[base + 16 + d, base + d] = -1.0
            rh[base + d, base + 16 + d] = 1.0
    rq = np.kron(np.eye(NH, dtype=np.float32), rh)
    rk = np.kron(np.eye(NKV, dtype=np.float32), rh)
    rep = np.zeros((NKV * HD, NH * HD), np.float32)
    for g in range(NKV):
        for r in range(REP):
            for d in range(HD):
                rep[g * HD + d, g * REP * HD + r * HD + d] = 1.0
    return jnp.asarray(rq, BF16), jnp.asarray(rep, BF16), jnp.asarray(rk @ rep, BF16)


def _rope_tables():
    half = HD // 2
    inv = 1.0 / (ROPE_THETA ** (jnp.arange(0, half, 2, dtype=F32) / half))
    pos = jnp.arange(S)
    ar = (pos // GRID_W).astype(F32)[:, None] * inv
    ac = (pos % GRID_W).astype(F32)[:, None] * inv
    cos = jnp.tile(jnp.concatenate([jnp.cos(ar), jnp.cos(ar), jnp.cos(ac), jnp.cos(ac)], axis=1), (1, NH))
    sin = jnp.tile(jnp.concatenate([jnp.sin(ar), jnp.sin(ar), jnp.sin(ac), jnp.sin(ac)], axis=1), (1, NH))
    cos = jnp.concatenate([cos, jnp.ones((TM_TOK, NH * HD), F32)], axis=0)
    sin = jnp.concatenate([sin, jnp.zeros((TM_TOK, NH * HD), F32)], axis=0)
    return cos, sin


def _prep_body(q_ref, kv_ref, cos_ref, sin_ref, rq_ref, rep_ref, rrep_ref, qo_ref, ko_ref, vo_ref):
    cos = cos_ref[...]
    sin = sin_ref[...]
    q = q_ref[...]
    qs = jnp.dot(q, rq_ref[...], preferred_element_type=F32)
    qo_ref[...] = ((q.astype(F32) * cos + qs * sin) * (HD ** -0.5)).astype(BF16)
    k = kv_ref[:, 0:NKV * HD]
    v = kv_ref[:, NKV * HD:2 * NKV * HD]
    kr = jnp.dot(k, rep_ref[...], preferred_element_type=F32)
    ks = jnp.dot(k, rrep_ref[...], preferred_element_type=F32)
    ko_ref[...] = (kr * cos + ks * sin).astype(BF16)
    vo_ref[...] = jnp.dot(v, rep_ref[...], preferred_element_type=F32).astype(BF16)


def _prep(p, n_rows, n_x_tiles, cos, sin, consts):
    tm = TM_TOK
    rq, rep, rrep = consts
    tps = S // tm

    def tab(i):
        return (jnp.where(i < n_x_tiles, i % tps, tps), 0)

    w = NH * HD
    return pl.pallas_call(
        _prep_body,
        out_shape=[jax.ShapeDtypeStruct((n_rows, w), BF16)] * 3,
        grid=(n_rows // tm,),
        in_specs=[
            pl.BlockSpec((tm, w), lambda i: (i, 0)),
            pl.BlockSpec((tm, 2 * NKV * HD), lambda i: (i, OFF_KV // (2 * NKV * HD))),
            pl.BlockSpec((tm, w), tab),
            pl.BlockSpec((tm, w), tab),
            pl.BlockSpec((w, w), lambda i: (0, 0)),
            pl.BlockSpec((NKV * HD, w), lambda i: (0, 0)),
            pl.BlockSpec((NKV * HD, w), lambda i: (0, 0)),
        ],
        out_specs=[pl.BlockSpec((tm, w), lambda i: (i, 0))] * 3,
        compiler_params=_cparams(("arbitrary",), 32),
        name="prep",
    )(p, p, cos, sin, rq, rep, rrep)


def _attn_group(q, kb, vb, sink_ref, g, valid, nq):
    gw = REP * HD
    lane_head = lax.broadcasted_iota(I32, (nq, gw), 1) // HD
    qg = q[:, g * gw:(g + 1) * gw]
    qs = jnp.concatenate([jnp.where(lane_head == r, qg, jnp.zeros_like(qg)) for r in range(REP)], axis=0)
    s = lax.dot_general(qs, kb, (((1,), (1,)), ((), ())), preferred_element_type=F32)
    if valid is not None:
        s = jnp.where(jnp.concatenate([valid] * REP, axis=0), s, -jnp.inf)
    row_head = lax.broadcasted_iota(I32, (REP * nq, 1), 0) // nq
    sink = jnp.zeros((REP * nq, 1), F32)
    for r in range(REP):
        sink = jnp.where(row_head == r, sink_ref[g * REP + r], sink)
    m = jnp.maximum(jnp.max(s, axis=-1, keepdims=True), sink)
    e = jnp.exp(s - m)
    den = jnp.sum(e, axis=-1, keepdims=True) + jnp.exp(sink - m)
    p = (e / den).astype(BF16)
    o = jnp.dot(p, vb, preferred_element_type=F32)
    og = jnp.zeros((nq, gw), F32)
    for r in range(REP):
        og = og + jnp.where(lane_head == r, o[r * nq:(r + 1) * nq, :], 0.0)
    return og


def _attn_body(sink_ref, q_ref, kp_ref, kc_ref, kn_ref, vp_ref, vc_ref, vn_ref, kx_ref, vx_ref, o_ref):
    n = pl.program_id(1)
    nblk = S // QB
    nk = 3 * QB + LC
    row = lax.broadcasted_iota(I32, (QB, nk), 0)
    col = lax.broadcasted_iota(I32, (QB, nk), 1)
    lo = jnp.where(n == 0, QB, 0)
    hi = jnp.where(n == nblk - 1, 2 * QB, 3 * QB)
    band = (col >= row) & (col <= row + 2 * WINDOW) & (col >= lo) & (col < hi)
    valid = band | (col >= 3 * QB)
    q = q_ref[...]
    gw = REP * HD
    outs = []
    for g in range(NKV):
        sl = slice(g * gw, (g + 1) * gw)
        kb = jnp.concatenate([kp_ref[:, sl], kc_ref[:, sl], kn_ref[:, sl], kx_ref[:, sl]], axis=0)
        vb = jnp.concatenate([vp_ref[:, sl], vc_ref[:, sl], vn_ref[:, sl], vx_ref[:, sl]], axis=0)
        outs.append(_attn_group(q, kb, vb, sink_ref, g, valid, QB))
    o_ref[...] = jnp.concatenate(outs, axis=1).astype(BF16)


def _attn(qr, kr, vr, kc_arr, vc_arr, ctx_blk0, sink):
    nblk = S // QB
    w = NH * HD

    def cur(b, n):
        return (b * nblk + n, 0)

    def prev(b, n):
        return (b * nblk + jnp.maximum(n - 1, 0), 0)

    def nxt(b, n):
        return (b * nblk + jnp.minimum(n + 1, nblk - 1), 0)

    def cx(b, n):
        return (ctx_blk0 + b, 0)

    blk = lambda f: pl.BlockSpec((QB, w), f)
    return pl.pallas_call(
        _attn_body,
        out_shape=jax.ShapeDtypeStruct((TX, w), BF16),
        grid=(NB, nblk),
        in_specs=[
            pl.BlockSpec(memory_space=pltpu.SMEM),
            blk(cur), blk(prev), blk(cur), blk(nxt), blk(prev), blk(cur), blk(nxt),
            pl.BlockSpec((LC, w), cx), pl.BlockSpec((LC, w), cx),
        ],
        out_specs=blk(cur),
        compiler_params=_cparams(("arbitrary", "arbitrary"), 32),
        name="window_attn",
    )(sink, qr, kr, kr, kr, vr, vr, vr, kc_arr, vc_arr)


def _ctx_attn_body(sink_ref, q_ref, kx_ref, vx_ref, o_ref):
    gw = REP * HD
    q = q_ref[...]
    outs = []
    for g in range(NKV):
        sl = slice(g * gw, (g + 1) * gw)
        outs.append(_attn_group(q, kx_ref[:, sl], vx_ref[:, sl], sink_ref, g, None, LC))
    o_ref[...] = jnp.concatenate(outs, axis=1).astype(BF16)


def _ctx_attn(qr, kr, vr, sink):
    w = NH * HD
    blk0 = TX // LC
    spec = pl.BlockSpec((LC, w), lambda b: (blk0 + b, 0))
    return pl.pallas_call(
        _ctx_attn_body,
        out_shape=jax.ShapeDtypeStruct((TC, w), BF16),
        grid=(NB,),
        in_specs=[pl.BlockSpec(memory_space=pltpu.SMEM), spec, spec, spec],
        out_specs=pl.BlockSpec((LC, w), lambda b: (b, 0)),
        compiler_params=_cparams(("arbitrary",), 32),
        name="ctx_attn",
    )(sink, qr, kr, vr)


def _mix_body(su_ref, sv_ref, cb_ref, cc_ref, cx_ref, ga_ref, gb_ref,
              ccp_ref, cxp_ref, gap_ref, gbp_ref, ccn_ref, cxn_ref, gan_ref, gbn_ref,
              lng_ref, lnb_ref, sw_ref, sbias_ref, scw_ref, dww_ref, dwb_ref, clg_ref, clb_ref,
              o_ref, m_scr, z_scr, *, n_x_tiles):
    tm = TM_TOK
    i = pl.program_id(0)
    tps = S // tm
    is_x = i < n_x_tiles
    first = jnp.logical_or(jnp.logical_not(is_x), (i % tps) == 0)
    last = jnp.logical_or(jnp.logical_not(is_x), (i % tps) == tps - 1)
    keep_p = jnp.where(first, 0.0, 1.0)
    keep_n = jnp.where(last, 0.0, 1.0)

    u = _gelu_tanh(su_ref[...].astype(F32))
    v = _layernorm(_gelu_tanh(sv_ref[...].astype(F32)), lng_ref[...], lnb_ref[...]).astype(BF16)
    gwid = BW // SGU_GROUPS
    chunks = []
    for c in range(tm // SGU_CHUNK):
        parts = []
        for g in range(SGU_GROUPS):
            vc = v[c * SGU_CHUNK:(c + 1) * SGU_CHUNK, g * gwid:(g + 1) * gwid]
            parts.append(jnp.dot(sw_ref[g], vc, preferred_element_type=F32))
        chunks.append(jnp.concatenate(parts, axis=1) + sbias_ref[...])
    yb = u * jnp.concatenate(chunks, axis=0)

    m_scr[HALO:HALO + tm, :] = cc_ref[...].astype(F32) * cx_ref[...].astype(F32)
    m_scr[0:HALO, :] = ccp_ref[...].astype(F32) * cxp_ref[...].astype(F32) * keep_p
    m_scr[HALO + tm:, :] = ccn_ref[...].astype(F32) * cxn_ref[...].astype(F32) * keep_n
    z = jnp.zeros((tm, BW), F32)
    for k in range(SCONV_K):
        z = z + scw_ref[k:k + 1, :] * m_scr[pl.ds(HALO - SCONV_K // 2 + k, tm), :]
    yc = cb_ref[...].astype(F32) * z

    z_scr[HALO:HALO + tm, :] = ga_ref[...].astype(F32) * gb_ref[...].astype(F32)
    z_scr[0:HALO, :] = gap_ref[...].astype(F32) * gbp_ref[...].astype(F32) * keep_p
    z_scr[HALO + tm:, :] = gan_ref[...].astype(F32) * gbn_ref[...].astype(F32) * keep_n
    acc = jnp.zeros((tm, BW), F32) + dwb_ref[...]
    for k in range(CONF_K):
        acc = acc + dww_ref[k:k + 1, :] * z_scr[pl.ds(HALO - CONF_K // 2 + k, tm), :]
    zn = _layernorm(acc, clg_ref[...], clb_ref[...])
    yd = zn * _sigmoid(zn)

    o_ref[...] = jnp.concatenate([yb, yc, yd], axis=1).astype(BF16)


def _mixers(p, n_rows, n_x_tiles, lp):
    tm = TM_TOK
    (sgu_ln_g, sgu_ln_b, sgu_w, sgu_b, sconv_w, conf_dw_w, conf_dw_b, conf_ln_g, conf_ln_b) = lp

    def sec(off):
        return pl.BlockSpec((pl.Element(tm), pl.Element(BW)), lambda i: (i * tm, off))

    hpt = tm // HALO

    def halo_prev(off):
        return pl.BlockSpec((pl.Element(HALO), pl.Element(BW)),
                            lambda i: (jnp.maximum(i * hpt - 1, 0) * HALO, off))

    def halo_next(off):
        return pl.BlockSpec((pl.Element(HALO), pl.Element(BW)),
                            lambda i: (jnp.minimum(i * hpt + hpt, n_rows // HALO - 1) * HALO, off))

    def const(shape):
        return pl.BlockSpec(shape, lambda i: (0,) * len(shape))

    sbias = jnp.repeat(sgu_b.T, BW // SGU_GROUPS, axis=1)
    row = lambda a: a.reshape(1, BW)
    conv_offs = (OFF_CC, OFF_CX, OFF_GA, OFF_GB)
    return pl.pallas_call(
        functools.partial(_mix_body, n_x_tiles=n_x_tiles),
        out_shape=jax.ShapeDtypeStruct((n_rows, 3 * BW), BF16),
        grid=(n_rows // tm,),
        in_specs=[sec(o) for o in (OFF_SU, OFF_SV, OFF_CB, OFF_CC, OFF_CX, OFF_GA, OFF_GB)]
        + [halo_prev(o) for o in conv_offs] + [halo_next(o) for o in conv_offs]
        + [const((1, BW)), const((1, BW)), const((SGU_GROUPS, SGU_CHUNK, SGU_CHUNK)), const((SGU_CHUNK, BW)),
           const((SCONV_K, BW)), const((CONF_K, BW)), const((1, BW)), const((1, BW)), const((1, BW))],
        out_specs=pl.BlockSpec((tm, 3 * BW), lambda i: (i, 0)),
        scratch_shapes=[pltpu.VMEM((tm + 2 * HALO, BW), F32), pltpu.VMEM((tm + 2 * HALO, BW), F32)],
        compiler_params=_cparams(("arbitrary",), 32),
        name="mixers",
    )(*([p] * 15), row(sgu_ln_g), row(sgu_ln_b), sgu_w.astype(BF16), sbias, sconv_w, conf_dw_w,
      row(conf_dw_b), row(conf_ln_g), row(conf_ln_b))


def _post_body(g_ref, ya_ref, yr_ref, wb_ref, wo_ref, x_ref, mod_ref, n2g_ref, rwh_ref, rwl_ref,
               xo_ref, h2_ref, lg_ref):
    merged = None
    for br in range(4):
        y = ya_ref[...] if br == 0 else yr_ref[:, (br - 1) * BW:br * BW]
        pr = jnp.dot(y, wb_ref[br], preferred_element_type=F32)
        term = g_ref[:, br * D:(br + 1) * D].astype(F32) * pr
        merged = term if merged is None else merged + term
    out = jnp.dot(merged.astype(BF16), wo_ref[...], preferred_element_type=F32)
    xn = x_ref[...] + mod_ref[2:3, :] * out
    xo_ref[...] = xn
    y2 = xn * lax.rsqrt(jnp.mean(xn * xn, axis=-1, keepdims=True) + 1e-6) * n2g_ref[...]
    h2 = y2 * (1.0 + mod_ref[4:5, :]) + mod_ref[3:4, :]
    h2_ref[...] = h2
    hi = h2.astype(BF16)
    lo = (h2 - hi.astype(F32)).astype(BF16)
    nt = (((1,), (1,)), ((), ()))
    lg = lax.dot_general(rwh_ref[...], hi, nt, preferred_element_type=F32)
    lg = lg + lax.dot_general(rwh_ref[...], lo, nt, preferred_element_type=F32)
    lg = lg + lax.dot_general(rwl_ref[...], hi, nt, preferred_element_type=F32)
    lg_ref[...] = lg


def _post(p, ya, yr, wb, wo, x, mods, layer, n2g, rwh, rwl, n_rows):
    tm = TM_TOK
    one = pl.Buffered(1)
    return pl.pallas_call(
        _post_body,
        out_shape=[jax.ShapeDtypeStruct((n_rows, D), F32), jax.ShapeDtypeStruct((n_rows, D), F32),
                   jax.ShapeDtypeStruct((NE, n_rows), F32)],
        grid=(n_rows // tm,),
        in_specs=[
            pl.BlockSpec((pl.Element(tm), pl.Element(4 * D)), lambda i: (i * tm, OFF_GATES)),
            pl.BlockSpec((tm, BW), lambda i: (i, 0)),
            pl.BlockSpec((tm, 3 * BW), lambda i: (i, 0)),
            pl.BlockSpec((4, BW, D), lambda i: (0, 0, 0), pipeline_mode=one),
            pl.BlockSpec((D, D), lambda i: (0, 0), pipeline_mode=one),
            pl.BlockSpec((tm, D), lambda i: (i, 0)),
            pl.BlockSpec((None, None, 6, D), lambda i: (layer, _mod_row(i, tm), 0, 0)),
            pl.BlockSpec((1, D), lambda i: (0, 0)),
            pl.BlockSpec((NE, D), lambda i: (0, 0)),
            pl.BlockSpec((NE, D), lambda i: (0, 0)),
        ],
        out_specs=[pl.BlockSpec((tm, D), lambda i: (i, 0)), pl.BlockSpec((tm, D), lambda i: (i, 0)),
                   pl.BlockSpec((NE, tm), lambda i: (0, i))],
        compiler_params=_cparams(("arbitrary",), 52),
        name="post",
    )(p, ya, yr, wb, wo, x, mods, n2g.reshape(1, D), rwh, rwl)


def _route_body(lg_ref, rb_ref, tri_ref, o_ref, cnt_ref, carry_ref):
    tm = TM_TOK
    i = pl.program_id(0)

    @pl.when(i == 0)
    def _():
        carry_ref[...] = jnp.zeros_like(carry_ref)

    lg = lg_ref[...]
    e = jnp.exp(lg - jnp.max(lg, axis=0, keepdims=True))
    sc = e / jnp.sum(e, axis=0, keepdims=True)
    bi = sc + rb_ref[...]
    b = [bi[k:k + 1, :] for k in range(NE)]
    s = [sc[k:k + 1, :] for k in range(NE)]

    gs = []
    for g in range(NG):
        v = b[g * EPG:(g + 1) * EPG]
        best = None
        for a in range(EPG):
            for c in range(a + 1, EPG):
                ps = v[a] + v[c]
                best = ps if best is None else jnp.maximum(best, ps)
        gs.append(best)
    gsel = jnp.zeros((1, tm), I32)
    gbest = gs[0]
    for g in range(1, NG):
        take = gs[g] > gbest
        gsel = jnp.where(take, g, gsel)
        gbest = jnp.where(take, gs[g], gbest)

    vb, vs = [], []
    for j in range(EPG):
        xb, xs = b[j], s[j]
        for g in range(1, NG):
            xb = jnp.where(gsel == g, b[g * EPG + j], xb)
            xs = jnp.where(gsel == g, s[g * EPG + j], xs)
        vb.append(xb)
        vs.append(xs)
    order = []
    for j in range(EPG):
        c = jnp.zeros((1, tm), I32)
        for m in range(EPG):
            if m == j:
                continue
            ahead = (vb[m] >= vb[j]) if m < j else (vb[m] > vb[j])
            c = c + jnp.where(ahead, 1, 0)
        order.append(c)
    zero = jnp.zeros((1, tm), F32)
    w0 = zero
    w1 = zero
    j0 = jnp.zeros((1, tm), I32)
    j1 = jnp.zeros((1, tm), I32)
    for j in range(EPG):
        w0 = jnp.where(order[j] == 0, vs[j], w0)
        w1 = jnp.where(order[j] == 1, vs[j], w1)
        j0 = jnp.where(order[j] == 0, j, j0)
        j1 = jnp.where(order[j] == 1, j, j1)
    tot = w0 + w1
    e0 = gsel * EPG + j0
    e1 = gsel * EPG + j1

    eid = lax.broadcasted_iota(I32, (NE, tm), 0)
    oh0 = eid == e0
    oh1 = eid == e1
    oh = jnp.where(oh0 | oh1, 1.0, 0.0)
    rank = jnp.dot(oh.astype(BF16), tri_ref[...], preferred_element_type=F32) + carry_ref[:, 0:1]
    r0 = jnp.sum(jnp.where(oh0, rank, 0.0), axis=0, keepdims=True)
    r1 = jnp.sum(jnp.where(oh1, rank, 0.0), axis=0, keepdims=True)
    new_carry = carry_ref[...] + jnp.sum(oh, axis=1, keepdims=True)
    carry_ref[...] = new_carry
    cnt_ref[...] = new_carry

    o_ref[0:1, :] = e0.astype(F32)
    o_ref[1:2, :] = e1.astype(F32)
    o_ref[2:3, :] = r0
    o_ref[3:4, :] = r1
    o_ref[4:5, :] = w0 / tot
    o_ref[5:6, :] = w1 / tot
    o_ref[6:8, :] = jnp.zeros((2, tm), F32)


def _route(lg, router_b, n_rows):
    tm = TM_TOK
    tri = jnp.asarray(np.triu(np.ones((tm, tm), np.float32), 1), BF16)
    return pl.pallas_call(
        _route_body,
        out_shape=[jax.ShapeDtypeStruct((8, n_rows), F32), jax.ShapeDtypeStruct((NE, LANES), F32)],
        grid=(n_rows // tm,),
        in_specs=[
            pl.BlockSpec((NE, tm), lambda i: (0, i)),
            pl.BlockSpec((NE, 1), lambda i: (0, 0)),
            pl.BlockSpec((tm, tm), lambda i: (0, 0)),
        ],
        out_specs=[pl.BlockSpec((8, tm), lambda i: (0, i)), pl.BlockSpec((NE, LANES), lambda i: (0, 0))],
        scratch_shapes=[pltpu.VMEM((NE, LANES), F32)],
        compiler_params=_cparams(("arbitrary",), 32),
        name="route",
    )(lg, router_b.reshape(NE, 1), tri)


def _row_copy(src_ref, src_row, dst_ref, dst_row, sem):
    return pltpu.make_async_copy(src_ref.at[pl.ds(src_row, 1)], dst_ref.at[pl.ds(dst_row, 1)], sem)


def _dispatch_body(pos_ref, h_ref, xs_in_ref, xs_ref, sem):
    del xs_in_ref
    tm = TM_TOK

    def issue(r, c):
        _row_copy(h_ref, r, xs_ref, pos_ref[0, r], sem).start()
        _row_copy(h_ref, r, xs_ref, pos_ref[1, r], sem).start()
        return c

    lax.fori_loop(0, tm, issue, 0)

    def drain(r, c):
        _row_copy(h_ref, 0, xs_ref, 0, sem).wait()
        _row_copy(h_ref, 0, xs_ref, 0, sem).wait()
        return c

    lax.fori_loop(0, tm, drain, 0)


def _dispatch(pos, h2, xs0, n_rows):
    tm = TM_TOK
    return pl.pallas_call(
        _dispatch_body,
        out_shape=jax.ShapeDtypeStruct(xs0.shape, F32),
        grid=(n_rows // tm,),
        in_specs=[
            pl.BlockSpec((None, 2, tm), lambda i: (i, 0, 0), memory_space=pltpu.SMEM),
            pl.BlockSpec((tm, D), lambda i: (i, 0)),
            pl.BlockSpec(memory_space=pl.ANY),
        ],
        out_specs=pl.BlockSpec(memory_space=pl.ANY),
        scratch_shapes=[pltpu.SemaphoreType.DMA(())],
        input_output_aliases={2: 0},
        compiler_params=_cparams(("arbitrary",), 32),
        name="dispatch",
    )(pos, h2, xs0)


def _moe_body(te_ref, nu_ref, xs_ref, wu_ref, wd_ref, ys_ref):
    del te_ref
    r = pl.program_id(0)

    @pl.when(r < nu_ref[0])
    def _():
        x = xs_ref[...].astype(BF16)
        hc = jnp.dot(x, wu_ref[...], preferred_element_type=F32)
        a = hc[:, :DFF]
        b = hc[:, DFF:]
        act = (a * _sigmoid(a) * b).astype(BF16)
        ys_ref[...] = jnp.dot(act, wd_ref[...], preferred_element_type=F32)

    @pl.when(r >= nu_ref[0])
    def _():
        ys_ref[...] = jnp.zeros_like(ys_ref)


def _moe(tile_expert, n_used, xs, wu, wd, layer):
    def row(r, te, nu):
        return (jnp.minimum(r, nu[0] - 1), 0)

    def wsel(r, te, nu):
        return (layer, te[jnp.minimum(r, nu[0] - 1)], 0, 0)

    return pl.pallas_call(
        _moe_body,
        out_shape=jax.ShapeDtypeStruct(xs.shape, F32),
        grid_spec=pltpu.PrefetchScalarGridSpec(
            num_scalar_prefetch=2,
            grid=(NT_E,),
            in_specs=[
                pl.BlockSpec((TM_E, D), row),
                pl.BlockSpec((None, None, D, 2 * DFF), wsel),
                pl.BlockSpec((None, None, DFF, D), wsel),
            ],
            out_specs=pl.BlockSpec((TM_E, D), lambda r, te, nu: (r, 0)),
        ),
        compiler_params=_cparams(("arbitrary",), 48),
        name="moe",
    )(tile_expert, n_used, xs, wu, wd)


def _combine_body(pos_ref, ys_ref, x_ref, w_ref, mod_ref, o_ref, ybuf, sem):
    tm = TM_TOK

    def issue(r, c):
        _row_copy(ys_ref, pos_ref[0, r], ybuf.at[0], r, sem).start()
        _row_copy(ys_ref, pos_ref[1, r], ybuf.at[1], r, sem).start()
        return c

    lax.fori_loop(0, tm, issue, 0)

    def drain(r, c):
        _row_copy(ys_ref, 0, ybuf.at[0], 0, sem).wait()
        _row_copy(ys_ref, 0, ybuf.at[1], 0, sem).wait()
        return c

    lax.fori_loop(0, tm, drain, 0)

    reps = D // LANES
    w0 = jnp.tile(w_ref[:, 0:LANES], (1, reps))
    w1 = jnp.tile(w_ref[:, LANES:2 * LANES], (1, reps))
    y = w0 * ybuf[0] + w1 * ybuf[1]
    o_ref[...] = x_ref[...] + mod_ref[5:6, :] * y


def _combine(pos, ys, x, wlanes, mods, layer, n_rows):
    tm = TM_TOK
    return pl.pallas_call(
        _combine_body,
        out_shape=jax.ShapeDtypeStruct((n_rows, D), F32),
        grid=(n_rows // tm,),
        in_specs=[
            pl.BlockSpec((None, 2, tm), lambda i: (i, 0, 0), memory_space=pltpu.SMEM),
            pl.BlockSpec(memory_space=pl.ANY),
            pl.BlockSpec((tm, D), lambda i: (i, 0)),
            pl.BlockSpec((tm, 2 * LANES), lambda i: (i, 0)),
            pl.BlockSpec((None, None, 6, D), lambda i: (layer, _mod_row(i, tm), 0, 0)),
        ],
        out_specs=pl.BlockSpec((tm, D), lambda i: (i, 0)),
        scratch_shapes=[pltpu.VMEM((2, tm, D), F32), pltpu.SemaphoreType.DMA(())],
        compiler_params=_cparams(("arbitrary",), 32),
        name="combine",
    )(pos, ys, x, wlanes, mods)


def _route_plan(route, cnt, n_rows):
    counts = cnt[:, 0].astype(I32)
    padded = ((counts + TM_E - 1) // TM_E) * TM_E
    ends = jnp.cumsum(padded)
    offs = ends - padded
    e01 = route[0:2].astype(I32)
    pos = offs[e01] + route[2:4].astype(I32)
    pos = pos.reshape(2, n_rows // TM_TOK, TM_TOK).transpose(1, 0, 2)
    tile_start = jnp.arange(NT_E, dtype=I32) * TM_E
    tile_expert = jnp.minimum(jnp.sum((tile_start[:, None] >= ends[None, :]).astype(I32), axis=1), NE - 1)
    n_used = (ends[-1] // TM_E).reshape(1).astype(I32)
    wl = jnp.concatenate([jnp.broadcast_to(route[4][:, None], (n_rows, LANES)),
                          jnp.broadcast_to(route[5][:, None], (n_rows, LANES))], axis=1)
    return pos, tile_expert.astype(I32), n_used, wl


def kernel(x, c, ctx, c_ctx, ada_w, ada_b, norm1_g, norm2_g, w_in, attn_sink, sgu_ln_g, sgu_ln_b, sgu_w, sgu_b,
           sconv_w, conf_dw_w, conf_dw_b, conf_ln_g, conf_ln_b, w_branch, w_out, router_w, router_b,
           exp_w_up, exp_w_down, final_g):
    cvec = jnp.concatenate([c, c_ctx[None, :], jnp.zeros((8 - NB - 1, D), F32)], axis=0)
    mods = _ada(cvec, ada_w, ada_b).reshape(DEPTH, 8, 6, D)
    xa = jnp.concatenate([x.reshape(TX, D), ctx.reshape(TC, D)], axis=0)
    cos, sin = _rope_tables()
    consts = _rope_constants()
    rwt = router_w.T
    rwh = rwt.astype(BF16)
    rwl = (rwt - rwh.astype(F32)).astype(BF16)
    wb_all = w_branch.astype(BF16)
    wo_all = w_out.astype(BF16)
    wu_all = exp_w_up.astype(BF16)
    wd_all = exp_w_down.astype(BF16)

    for l in range(DEPTH):
        last = l == DEPTH - 1
        n_rows = TX if last else T
        n_x_tiles = TX // TM_TOK
        lp = (sgu_ln_g[l], sgu_ln_b[l], sgu_w[l], sgu_b[l], sconv_w[l], conf_dw_w[l], conf_dw_b[l],
              conf_ln_g[l], conf_ln_b[l])

        h = _norm_mod(xa, norm1_g[l], mods, l, T, 0, 1)
        if not last:
            p = _inproj(h, w_in, l, T, N_IN // TN_IN)
            qr, kr, vr = _prep(p, T, n_x_tiles, cos, sin, consts)
            ya = _attn(qr, kr, vr, kr, vr, TX // LC, attn_sink[l])
            yac = _ctx_attn(qr, kr, vr, attn_sink[l])
            ya = jnp.concatenate([ya, yac], axis=0)
        else:
            p = _inproj(h, w_in, l, TX, N_IN // TN_IN)
            pc = _inproj(h[TX:], w_in, l, TC, 1)
            qr, kr, vr = _prep(p, TX, n_x_tiles, cos, sin, consts)
            _, kc, vc = _prep(pc, TC, 0, cos, sin, consts)
            ya = _attn(qr, kr, vr, kc, vc, 0, attn_sink[l])
        yr = _mixers(p, n_rows, n_x_tiles, lp)
        xn, h2, lg = _post(p, ya, yr, wb_all[l], wo_all[l], xa, mods, l, norm2_g[l], rwh, rwl, n_rows)
        route, cnt = _route(lg, router_b, n_rows)
        pos, tile_expert, n_used, wl = _route_plan(route, cnt, n_rows)
        xs = _dispatch(pos, h2, jnp.zeros((NT_E * TM_E, D), F32), n_rows)
        ys = _moe(tile_expert, n_used, xs, wu_all, wd_all, l)
        xa = _combine(pos, ys, xn, wl, mods, l, n_rows)

    return _final_norm(xa, final_g).reshape(NB, S, D)
```

```python
import functools

import numpy as np
import jax
import jax.numpy as jnp
from jax import lax
from jax.experimental import pallas as pl
from jax.experimental.pallas import tpu as pltpu

F32 = jnp.float32
BF16 = jnp.bfloat16
I32 = jnp.int32

D = 2048
NB = 4
S = 2048
LC = 256
DEPTH = 2
GRID_W = 64
BW = 512
HD = 64
NH = 8
NKV = 2
REP = NH // NKV
WINDOW = 128
QB = 128
ROPE_THETA = 10000.0
SGU_CHUNK = 128
SGU_GROUPS = 4
SCONV_K = 3
CONF_K = 31
NE = 16
NG = 4
EPG = NE // NG
DFF = D // 2
N_IN = BW + 2 * NKV * HD + 7 * BW + 4 * D
TX = NB * S
TC = NB * LC
T = TX + TC

OFF_Q, OFF_KV = 0, 512
OFF_SU, OFF_SV, OFF_CB, OFF_CC, OFF_CX, OFF_GA, OFF_GB, OFF_GATES = 768, 1280, 1792, 2304, 2816, 3328, 3840, 4352

LANES = 128
V7X_VMEM_BYTES = 64 * 1024 * 1024
MIB = 1024 * 1024

TM_IN = 512
INPROJ_ROW_CHUNK = 256
TN_IN = 1792
TM_TOK = 256
TM_E = 256
HALO = 16
ROW_UNROLL = 8


def _n_expert_tiles(n_rows):
    return (2 * n_rows) // TM_E + NE


def _cparams(sem, vmem_mib):
    return pltpu.CompilerParams(dimension_semantics=sem, vmem_limit_bytes=vmem_mib * MIB)


def _sigmoid(x):
    return 0.5 * jnp.tanh(0.5 * x) + 0.5


def _gelu_tanh(x):
    c = np.float32(np.sqrt(2.0 / np.pi))
    return 0.5 * x * (1.0 + jnp.tanh(c * (x + np.float32(0.044715) * (x * x * x))))


def _layernorm(x, g, b, eps=1e-5):
    mu = jnp.mean(x, axis=-1, keepdims=True)
    xc = x - mu
    var = jnp.mean(xc * xc, axis=-1, keepdims=True)
    return xc * lax.rsqrt(var + eps) * g + b


def _mod_row(i, tm):
    return jnp.where(i < TX // tm, (i * tm) // S, NB)


def _ada_body(c_ref, w_ref, b_ref, o_ref):
    c = c_ref[...]
    s = (c * _sigmoid(c)).astype(BF16)
    o_ref[...] = jnp.dot(s, w_ref[...].astype(BF16), preferred_element_type=F32) + b_ref[...]


def _ada(cvec, ada_w, ada_b):
    tn = 1024
    return pl.pallas_call(
        _ada_body,
        out_shape=jax.ShapeDtypeStruct((DEPTH, 8, 6 * D), F32),
        grid=(DEPTH, 6 * D // tn),
        in_specs=[
            pl.BlockSpec((8, D), lambda l, j: (0, 0)),
            pl.BlockSpec((None, D, tn), lambda l, j: (l, 0, j)),
            pl.BlockSpec((None, 1, tn), lambda l, j: (l, 0, j)),
        ],
        out_specs=pl.BlockSpec((None, 8, tn), lambda l, j: (l, 0, j)),
        compiler_params=_cparams(("arbitrary", "arbitrary"), 40),
        name="ada",
    )(cvec, ada_w, ada_b.reshape(DEPTH, 1, 6 * D))


def _norm_mod_body(x_ref, g_ref, mod_ref, o_ref, *, sh_idx, sc_idx):
    x = x_ref[...]
    y = x * lax.rsqrt(jnp.mean(x * x, axis=-1, keepdims=True) + 1e-6) * g_ref[...]
    o_ref[...] = (y * (1.0 + mod_ref[sc_idx:sc_idx + 1, :]) + mod_ref[sh_idx:sh_idx + 1, :]).astype(o_ref.dtype)


def _norm_mod(x, g, mods, layer, n_rows, sh_idx, sc_idx):
    tm = TM_IN
    return pl.pallas_call(
        functools.partial(_norm_mod_body, sh_idx=sh_idx, sc_idx=sc_idx),
        out_shape=jax.ShapeDtypeStruct((n_rows, D), BF16),
        grid=(n_rows // tm,),
        in_specs=[
            pl.BlockSpec((tm, D), lambda i: (i, 0)),
            pl.BlockSpec((1, D), lambda i: (0, 0)),
            pl.BlockSpec((None, None, 6, D), lambda i: (layer, _mod_row(i, tm), 0, 0)),
        ],
        out_specs=pl.BlockSpec((tm, D), lambda i: (i, 0)),
        compiler_params=_cparams(("arbitrary",), 32),
        name="norm_mod",
    )(x, g.reshape(1, D), mods)


def _final_norm_body(x_ref, g_ref, o_ref):
    x = x_ref[...]
    o_ref[...] = x * lax.rsqrt(jnp.mean(x * x, axis=-1, keepdims=True) + 1e-6) * g_ref[...]


def _final_norm(x, g):
    tm = TM_IN
    return pl.pallas_call(
        _final_norm_body,
        out_shape=jax.ShapeDtypeStruct((TX, D), F32),
        grid=(TX // tm,),
        in_specs=[pl.BlockSpec((tm, D), lambda i: (i, 0)), pl.BlockSpec((1, D), lambda i: (0, 0))],
        out_specs=pl.BlockSpec((tm, D), lambda i: (i, 0)),
        compiler_params=_cparams(("arbitrary",), 32),
        name="final_norm",
    )(x, g.reshape(1, D))


def _inproj_body(h_ref, w_ref, o_ref, wbf_ref):
    j = pl.program_id(0)
    i = pl.program_id(1)

    @pl.when(i == 0)
    def _():
        wbf_ref[...] = w_ref[...].astype(BF16)

    col0 = j * TN_IN
    rows = INPROJ_ROW_CHUNK

    def run(epilogue):
        for c in range(h_ref.shape[0] // rows):
            sl = slice(c * rows, (c + 1) * rows)
            acc = jnp.dot(h_ref[sl, :], wbf_ref[...], preferred_element_type=F32)
            o_ref[sl, :] = epilogue(acc).astype(BF16)

    @pl.when(col0 + TN_IN <= OFF_GB)
    def _():
        run(lambda acc: acc)

    @pl.when(col0 >= OFF_GB)
    def _():
        run(_sigmoid)

    @pl.when(jnp.logical_and(col0 < OFF_GB, col0 + TN_IN > OFF_GB))
    def _():
        col = col0 + lax.broadcasted_iota(I32, (rows, TN_IN), 1)
        run(lambda acc: jnp.where(col >= OFF_GB, _sigmoid(acc), acc))


def _inproj(h, w_in, layer, n_rows, n_col_tiles):
    tm = TM_IN
    return pl.pallas_call(
        _inproj_body,
        out_shape=jax.ShapeDtypeStruct((n_rows, n_col_tiles * TN_IN), BF16),
        grid=(n_col_tiles, n_rows // tm),
        in_specs=[
            pl.BlockSpec((tm, D), lambda j, i: (i, 0)),
            pl.BlockSpec((None, D, TN_IN), lambda j, i: (layer, 0, j)),
        ],
        out_specs=pl.BlockSpec((tm, TN_IN), lambda j, i: (i, j)),
        scratch_shapes=[pltpu.VMEM((D, TN_IN), BF16)],
        compiler_params=_cparams(("arbitrary", "arbitrary"), 56),
        name="inproj",
    )(h, w_in)


def _rope_constants():
    rh = np.zeros((HD, HD), np.float32)
    for base in (0, 32):
        for d in range(16):
            rh[base + 16 + d, base + d] = -1.0
            rh[base + d, base + 16 + d] = 1.0
    rq = np.kron(np.eye(NH, dtype=np.float32), rh)
    rk = np.kron(np.eye(NKV, dtype=np.float32), rh)
    rep = np.zeros((NKV * HD, NH * HD), np.float32)
    for g in range(NKV):
        for r in range(REP):
            for d in range(HD):
                rep[g * HD + d, g * REP * HD + r * HD + d] = 1.0
    return jnp.asarray(rq, BF16), jnp.asarray(rep, BF16), jnp.asarray(rk @ rep, BF16)


def _rope_tables():
    half = HD // 2
    inv = 1.0 / (ROPE_THETA ** (jnp.arange(0, half, 2, dtype=F32) / half))
    pos = jnp.arange(S)
    ar = (pos // GRID_W).astype(F32)[:, None] * inv
    ac = (pos % GRID_W).astype(F32)[:, None] * inv
    cos = jnp.tile(jnp.concatenate([jnp.cos(ar), jnp.cos(ar), jnp.cos(ac), jnp.cos(ac)], axis=1), (1, NH))
    sin = jnp.tile(jnp.concatenate([jnp.sin(ar), jnp.sin(ar), jnp.sin(ac), jnp.sin(ac)], axis=1), (1, NH))
    cos = jnp.concatenate([cos, jnp.ones((TM_TOK, NH * HD), F32)], axis=0)
    sin = jnp.concatenate([sin, jnp.zeros((TM_TOK, NH * HD), F32)], axis=0)
    return cos, sin


def _prep_body(q_ref, kv_ref, cos_ref, sin_ref, rq_ref, rep_ref, rrep_ref, qo_ref, ko_ref, vo_ref):
    cos = cos_ref[...]
    sin = sin_ref[...]
    q = q_ref[...]
    qs = jnp.dot(q, rq_ref[...], preferred_element_type=F32)
    qo_ref[...] = ((q.astype(F32) * cos + qs * sin) * (HD ** -0.5)).astype(BF16)
    k = kv_ref[:, 0:NKV * HD]
    v = kv_ref[:, NKV * HD:2 * NKV * HD]
    kr = jnp.dot(k, rep_ref[...], preferred_element_type=F32)
    ks = jnp.dot(k, rrep_ref[...], preferred_element_type=F32)
    ko_ref[...] = (kr * cos + ks * sin).astype(BF16)
    vo_ref[...] = jnp.dot(v, rep_ref[...], preferred_element_type=F32).astype(BF16)


def _prep(p, n_rows, n_x_tiles, cos, sin, consts):
    tm = TM_TOK
    rq, rep, rrep = consts
    tps = S // tm

    def tab(i):
        return (jnp.where(i < n_x_tiles, i % tps, tps), 0)

    w = NH * HD
    return pl.pallas_call(
        _prep_body,
        out_shape=[jax.ShapeDtypeStruct((n_rows, w), BF16)] * 3,
        grid=(n_rows // tm,),
        in_specs=[
            pl.BlockSpec((tm, w), lambda i: (i, 0)),
            pl.BlockSpec((tm, 2 * NKV * HD), lambda i: (i, OFF_KV // (2 * NKV * HD))),
            pl.BlockSpec((tm, w), tab),
            pl.BlockSpec((tm, w), tab),
            pl.BlockSpec((w, w), lambda i: (0, 0)),
            pl.BlockSpec((NKV * HD, w), lambda i: (0, 0)),
            pl.BlockSpec((NKV * HD, w), lambda i: (0, 0)),
        ],
        out_specs=[pl.BlockSpec((tm, w), lambda i: (i, 0))] * 3,
        compiler_params=_cparams(("arbitrary",), 32),
        name="prep",
    )(p, p, cos, sin, rq, rep, rrep)


def _attn_group(q, kb, vb, sink_ref, g, valid, nq):
    gw = REP * HD
    lane_head = lax.broadcasted_iota(I32, (nq, gw), 1) // HD
    qg = q[:, g * gw:(g + 1) * gw]
    qs = jnp.concatenate([jnp.where(lane_head == r, qg, jnp.zeros_like(qg)) for r in range(REP)], axis=0)
    s = lax.dot_general(qs, kb, (((1,), (1,)), ((), ())), preferred_element_type=F32)
    if valid is not None:
        s = jnp.where(jnp.concatenate([valid] * REP, axis=0), s, -jnp.inf)
    row_head = lax.broadcasted_iota(I32, (REP * nq, 1), 0) // nq
    sink = jnp.zeros((REP * nq, 1), F32)
    for r in range(REP):
        sink = jnp.where(row_head == r, sink_ref[g * REP + r], sink)
    m = jnp.maximum(jnp.max(s, axis=-1, keepdims=True), sink)
    e = jnp.exp(s - m)
    den = jnp.sum(e, axis=-1, keepdims=True) + jnp.exp(sink - m)
    p = (e / den).astype(BF16)
    o = jnp.dot(p, vb, preferred_element_type=F32)
    og = jnp.zeros((nq, gw), F32)
    for r in range(REP):
        og = og + jnp.where(lane_head == r, o[r * nq:(r + 1) * nq, :], 0.0)
    return og


def _attn_body(sink_ref, q_ref, kp_ref, kc_ref, kn_ref, vp_ref, vc_ref, vn_ref, kx_ref, vx_ref, o_ref):
    n = pl.program_id(1)
    nblk = S // QB
    nk = 3 * QB + LC
    row = lax.broadcasted_iota(I32, (QB, nk), 0)
    col = lax.broadcasted_iota(I32, (QB, nk), 1)
    lo = jnp.where(n == 0, QB, 0)
    hi = jnp.where(n == nblk - 1, 2 * QB, 3 * QB)
    band = (col >= row) & (col <= row + 2 * WINDOW) & (col >= lo) & (col < hi)
    valid = band | (col >= 3 * QB)
    q = q_ref[...]
    gw = REP * HD
    outs = []
    for g in range(NKV):
        sl = slice(g * gw, (g + 1) * gw)
        kb = jnp.concatenate([kp_ref[:, sl], kc_ref[:, sl], kn_ref[:, sl], kx_ref[:, sl]], axis=0)
        vb = jnp.concatenate([vp_ref[:, sl], vc_ref[:, sl], vn_ref[:, sl], vx_ref[:, sl]], axis=0)
        outs.append(_attn_group(q, kb, vb, sink_ref, g, valid, QB))
    o_ref[...] = jnp.concatenate(outs, axis=1).astype(BF16)


def _attn(qr, kr, vr, kc_arr, vc_arr, ctx_blk0, sink):
    nblk = S // QB
    w = NH * HD

    def cur(b, n):
        return (b * nblk + n, 0)

    def prev(b, n):
        return (b * nblk + jnp.maximum(n - 1, 0), 0)

    def nxt(b, n):
        return (b * nblk + jnp.minimum(n + 1, nblk - 1), 0)

    def cx(b, n):
        return (ctx_blk0 + b, 0)

    blk = lambda f: pl.BlockSpec((QB, w), f)
    return pl.pallas_call(
        _attn_body,
        out_shape=jax.ShapeDtypeStruct((TX, w), BF16),
        grid=(NB, nblk),
        in_specs=[
            pl.BlockSpec(memory_space=pltpu.SMEM),
            blk(cur), blk(prev), blk(cur), blk(nxt), blk(prev), blk(cur), blk(nxt),
            pl.BlockSpec((LC, w), cx), pl.BlockSpec((LC, w), cx),
        ],
        out_specs=blk(cur),
        compiler_params=_cparams(("arbitrary", "arbitrary"), 32),
        name="window_attn",
    )(sink, qr, kr, kr, kr, vr, vr, vr, kc_arr, vc_arr)


def _ctx_attn_body(sink_ref, q_ref, kx_ref, vx_ref, o_ref):
    gw = REP * HD
    q = q_ref[...]
    outs = []
    for g in range(NKV):
        sl = slice(g * gw, (g + 1) * gw)
        outs.append(_attn_group(q, kx_ref[:, sl], vx_ref[:, sl], sink_ref, g, None, LC))
    o_ref[...] = jnp.concatenate(outs, axis=1).astype(BF16)


def _ctx_attn(qr, kr, vr, sink):
    w = NH * HD
    blk0 = TX // LC
    spec = pl.BlockSpec((LC, w), lambda b: (blk0 + b, 0))
    return pl.pallas_call(
        _ctx_attn_body,
        out_shape=jax.ShapeDtypeStruct((TC, w), BF16),
        grid=(NB,),
        in_specs=[pl.BlockSpec(memory_space=pltpu.SMEM), spec, spec, spec],
        out_specs=pl.BlockSpec((LC, w), lambda b: (b, 0)),
        compiler_params=_cparams(("arbitrary",), 32),
        name="ctx_attn",
    )(sink, qr, kr, vr)


def _mix_body(su_ref, sv_ref, cb_ref, cc_ref, cx_ref, ga_ref, gb_ref,
              ccp_ref, cxp_ref, gap_ref, gbp_ref, ccn_ref, cxn_ref, gan_ref, gbn_ref,
              lng_ref, lnb_ref, sw_ref, sbias_ref, scw_ref, dww_ref, dwb_ref, clg_ref, clb_ref,
              o_ref, m_scr, z_scr, *, n_x_tiles):
    tm = TM_TOK
    i = pl.program_id(0)
    tps = S // tm
    is_x = i < n_x_tiles
    first = jnp.logical_or(jnp.logical_not(is_x), (i % tps) == 0)
    last = jnp.logical_or(jnp.logical_not(is_x), (i % tps) == tps - 1)
    keep_p = jnp.where(first, 0.0, 1.0)
    keep_n = jnp.where(last, 0.0, 1.0)

    u = _gelu_tanh(su_ref[...].astype(F32))
    v = _layernorm(_gelu_tanh(sv_ref[...].astype(F32)), lng_ref[...], lnb_ref[...]).astype(BF16)
    gwid = BW // SGU_GROUPS
    chunks = []
    for c in range(tm // SGU_CHUNK):
        parts = []
        for g in range(SGU_GROUPS):
            vc = v[c * SGU_CHUNK:(c + 1) * SGU_CHUNK, g * gwid:(g + 1) * gwid]
            parts.append(jnp.dot(sw_ref[g], vc, preferred_element_type=F32))
        chunks.append(jnp.concatenate(parts, axis=1) + sbias_ref[...])
    yb = u * jnp.concatenate(chunks, axis=0)

    m_scr[HALO:HALO + tm, :] = cc_ref[...].astype(F32) * cx_ref[...].astype(F32)
    m_scr[0:HALO, :] = ccp_ref[...].astype(F32) * cxp_ref[...].astype(F32) * keep_p
    m_scr[HALO + tm:, :] = ccn_ref[...].astype(F32) * cxn_ref[...].astype(F32) * keep_n
    z = jnp.zeros((tm, BW), F32)
    for k in range(SCONV_K):
        z = z + scw_ref[k:k + 1, :] * m_scr[pl.ds(HALO - SCONV_K // 2 + k, tm), :]
    yc = cb_ref[...].astype(F32) * z

    z_scr[HALO:HALO + tm, :] = ga_ref[...].astype(F32) * gb_ref[...].astype(F32)
    z_scr[0:HALO, :] = gap_ref[...].astype(F32) * gbp_ref[...].astype(F32) * keep_p
    z_scr[HALO + tm:, :] = gan_ref[...].astype(F32) * gbn_ref[...].astype(F32) * keep_n
    acc = jnp.zeros((tm, BW), F32) + dwb_ref[...]
    for k in range(CONF_K):
        acc = acc + dww_ref[k:k + 1, :] * z_scr[pl.ds(HALO - CONF_K // 2 + k, tm), :]
    zn = _layernorm(acc, clg_ref[...], clb_ref[...])
    yd = zn * _sigmoid(zn)

    o_ref[...] = jnp.concatenate([yb, yc, yd], axis=1).astype(BF16)


def _mixers(p, n_rows, n_x_tiles, lp):
    tm = TM_TOK
    (sgu_ln_g, sgu_ln_b, sgu_w, sgu_b, sconv_w, conf_dw_w, conf_dw_b, conf_ln_g, conf_ln_b) = lp

    def sec(off):
        return pl.BlockSpec((pl.Element(tm), pl.Element(BW)), lambda i: (i * tm, off))

    hpt = tm // HALO

    def halo_prev(off):
        return pl.BlockSpec((pl.Element(HALO), pl.Element(BW)),
                            lambda i: (jnp.maximum(i * hpt - 1, 0) * HALO, off))

    def halo_next(off):
        return pl.BlockSpec((pl.Element(HALO), pl.Element(BW)),
                            lambda i: (jnp.minimum(i * hpt + hpt, n_rows // HALO - 1) * HALO, off))

    def const(shape):
        return pl.BlockSpec(shape, lambda i: (0,) * len(shape))

    sbias = jnp.repeat(sgu_b.T, BW // SGU_GROUPS, axis=1)
    row = lambda a: a.reshape(1, BW)
    conv_offs = (OFF_CC, OFF_CX, OFF_GA, OFF_GB)
    return pl.pallas_call(
        functools.partial(_mix_body, n_x_tiles=n_x_tiles),
        out_shape=jax.ShapeDtypeStruct((n_rows, 3 * BW), BF16),
        grid=(n_rows // tm,),
        in_specs=[sec(o) for o in (OFF_SU, OFF_SV, OFF_CB, OFF_CC, OFF_CX, OFF_GA, OFF_GB)]
        + [halo_prev(o) for o in conv_offs] + [halo_next(o) for o in conv_offs]
        + [const((1, BW)), const((1, BW)), const((SGU_GROUPS, SGU_CHUNK, SGU_CHUNK)), const((SGU_CHUNK, BW)),
           const((SCONV_K, BW)), const((CONF_K, BW)), const((1, BW)), const((1, BW)), const((1, BW))],
        out_specs=pl.BlockSpec((tm, 3 * BW), lambda i: (i, 0)),
        scratch_shapes=[pltpu.VMEM((tm + 2 * HALO, BW), F32), pltpu.VMEM((tm + 2 * HALO, BW), F32)],
        compiler_params=_cparams(("arbitrary",), 32),
        name="mixers",
    )(*([p] * 15), row(sgu_ln_g), row(sgu_ln_b), sgu_w.astype(BF16), sbias, sconv_w, conf_dw_w,
      row(conf_dw_b), row(conf_ln_g), row(conf_ln_b))


def _post_body(g_ref, ya_ref, yr_ref, wb_ref, wo_ref, x_ref, mod_ref, n2g_ref, rwh_ref, rwl_ref,
               xo_ref, h2_ref, lg_ref):
    merged = None
    for br in range(4):
        y = ya_ref[...] if br == 0 else yr_ref[:, (br - 1) * BW:br * BW]
        pr = jnp.dot(y, wb_ref[br], preferred_element_type=F32)
        term = g_ref[:, br * D:(br + 1) * D].astype(F32) * pr
        merged = term if merged is None else merged + term
    out = jnp.dot(merged.astype(BF16), wo_ref[...], preferred_element_type=F32)
    xn = x_ref[...] + mod_ref[2:3, :] * out
    xo_ref[...] = xn
    y2 = xn * lax.rsqrt(jnp.mean(xn * xn, axis=-1, keepdims=True) + 1e-6) * n2g_ref[...]
    h2 = y2 * (1.0 + mod_ref[4:5, :]) + mod_ref[3:4, :]
    h2_ref[...] = h2
    hi = h2.astype(BF16)
    lo = (h2 - hi.astype(F32)).astype(BF16)
    nt = (((1,), (1,)), ((), ()))
    lg = lax.dot_general(rwh_ref[...], hi, nt, preferred_element_type=F32)
    lg = lg + lax.dot_general(rwh_ref[...], lo, nt, preferred_element_type=F32)
    lg = lg + lax.dot_general(rwl_ref[...], hi, nt, preferred_element_type=F32)
    lg_ref[...] = lg


def _post(p, ya, yr, wb, wo, x, mods, layer, n2g, rwh, rwl, n_rows):
    tm = TM_TOK
    one = pl.Buffered(1)
    return pl.pallas_call(
        _post_body,
        out_shape=[jax.ShapeDtypeStruct((n_rows, D), F32), jax.ShapeDtypeStruct((n_rows, D), F32),
                   jax.ShapeDtypeStruct((NE, n_rows), F32)],
        grid=(n_rows // tm,),
        in_specs=[
            pl.BlockSpec((pl.Element(tm), pl.Element(4 * D)), lambda i: (i * tm, OFF_GATES)),
            pl.BlockSpec((tm, BW), lambda i: (i, 0)),
            pl.BlockSpec((tm, 3 * BW), lambda i: (i, 0)),
            pl.BlockSpec((4, BW, D), lambda i: (0, 0, 0), pipeline_mode=one),
            pl.BlockSpec((D, D), lambda i: (0, 0), pipeline_mode=one),
            pl.BlockSpec((tm, D), lambda i: (i, 0)),
            pl.BlockSpec((None, None, 6, D), lambda i: (layer, _mod_row(i, tm), 0, 0)),
            pl.BlockSpec((1, D), lambda i: (0, 0)),
            pl.BlockSpec((NE, D), lambda i: (0, 0)),
            pl.BlockSpec((NE, D), lambda i: (0, 0)),
        ],
        out_specs=[pl.BlockSpec((tm, D), lambda i: (i, 0)), pl.BlockSpec((tm, D), lambda i: (i, 0)),
                   pl.BlockSpec((NE, tm), lambda i: (0, i))],
        compiler_params=_cparams(("arbitrary",), 52),
        name="post",
    )(p, ya, yr, wb, wo, x, mods, n2g.reshape(1, D), rwh, rwl)


def _route_body(lg_ref, rb_ref, tri_ref, o_ref, cnt_ref, carry_ref):
    tm = TM_TOK
    i = pl.program_id(0)

    @pl.when(i == 0)
    def _():
        carry_ref[...] = jnp.zeros_like(carry_ref)

    lg = lg_ref[...]
    e = jnp.exp(lg - jnp.max(lg, axis=0, keepdims=True))
    sc = e / jnp.sum(e, axis=0, keepdims=True)
    bi = sc + rb_ref[...]
    b = [bi[k:k + 1, :] for k in range(NE)]
    s = [sc[k:k + 1, :] for k in range(NE)]

    gs = []
    for g in range(NG):
        v = b[g * EPG:(g + 1) * EPG]
        best = None
        for a in range(EPG):
            for c in range(a + 1, EPG):
                ps = v[a] + v[c]
                best = ps if best is None else jnp.maximum(best, ps)
        gs.append(best)
    gsel = jnp.zeros((1, tm), I32)
    gbest = gs[0]
    for g in range(1, NG):
        take = gs[g] > gbest
        gsel = jnp.where(take, g, gsel)
        gbest = jnp.where(take, gs[g], gbest)

    vb, vs = [], []
    for j in range(EPG):
        xb, xs = b[j], s[j]
        for g in range(1, NG):
            xb = jnp.where(gsel == g, b[g * EPG + j], xb)
            xs = jnp.where(gsel == g, s[g * EPG + j], xs)
        vb.append(xb)
        vs.append(xs)
    order = []
    for j in range(EPG):
        c = jnp.zeros((1, tm), I32)
        for m in range(EPG):
            if m == j:
                continue
            ahead = (vb[m] >= vb[j]) if m < j else (vb[m] > vb[j])
            c = c + jnp.where(ahead, 1, 0)
        order.append(c)
    zero = jnp.zeros((1, tm), F32)
    w0 = zero
    w1 = zero
    j0 = jnp.zeros((1, tm), I32)
    j1 = jnp.zeros((1, tm), I32)
    for j in range(EPG):
        w0 = jnp.where(order[j] == 0, vs[j], w0)
        w1 = jnp.where(order[j] == 1, vs[j], w1)
        j0 = jnp.where(order[j] == 0, j, j0)
        j1 = jnp.where(order[j] == 1, j, j1)
    tot = w0 + w1
    e0 = gsel * EPG + j0
    e1 = gsel * EPG + j1

    eid = lax.broadcasted_iota(I32, (NE, tm), 0)
    oh0 = eid == e0
    oh1 = eid == e1
    oh = jnp.where(oh0 | oh1, 1.0, 0.0)
    rank = jnp.dot(oh.astype(BF16), tri_ref[...], preferred_element_type=F32) + carry_ref[:, 0:1]
    r0 = jnp.sum(jnp.where(oh0, rank, 0.0), axis=0, keepdims=True)
    r1 = jnp.sum(jnp.where(oh1, rank, 0.0), axis=0, keepdims=True)
    new_carry = carry_ref[...] + jnp.sum(oh, axis=1, keepdims=True)
    carry_ref[...] = new_carry
    cnt_ref[...] = new_carry

    o_ref[0:1, :] = e0.astype(F32)
    o_ref[1:2, :] = e1.astype(F32)
    o_ref[2:3, :] = r0
    o_ref[3:4, :] = r1
    o_ref[4:5, :] = w0 / tot
    o_ref[5:6, :] = w1 / tot
    o_ref[6:8, :] = jnp.zeros((2, tm), F32)


def _route(lg, router_b, n_rows):
    tm = TM_TOK
    tri = jnp.asarray(np.triu(np.ones((tm, tm), np.float32), 1), BF16)
    return pl.pallas_call(
        _route_body,
        out_shape=[jax.ShapeDtypeStruct((8, n_rows), F32), jax.ShapeDtypeStruct((NE, LANES), F32)],
        grid=(n_rows // tm,),
        in_specs=[
            pl.BlockSpec((NE, tm), lambda i: (0, i)),
            pl.BlockSpec((NE, 1), lambda i: (0, 0)),
            pl.BlockSpec((tm, tm), lambda i: (0, 0)),
        ],
        out_specs=[pl.BlockSpec((8, tm), lambda i: (0, i)), pl.BlockSpec((NE, LANES), lambda i: (0, 0))],
        scratch_shapes=[pltpu.VMEM((NE, LANES), F32)],
        compiler_params=_cparams(("arbitrary",), 32),
        name="route",
    )(lg, router_b.reshape(NE, 1), tri)


def _row_copy(src_ref, src_row, dst_ref, dst_row, sem):
    return pltpu.make_async_copy(src_ref.at[pl.ds(src_row, 1)], dst_ref.at[pl.ds(dst_row, 1)], sem)


def _issue_rows(n, start_row):
    def trip(t, c):
        for u in range(ROW_UNROLL):
            start_row(t * ROW_UNROLL + u)
        return c

    lax.fori_loop(0, n // ROW_UNROLL, trip, 0)


def _dispatch_body(tail_ref, need_ref, pos_ref, h_ref, xs_ref, zero_ref, sem, zsem):
    tm = TM_TOK

    @pl.when(pl.program_id(0) == 0)
    def _():
        zero_ref[...] = jnp.zeros_like(zero_ref)

        def tail_copy(e):
            start = pl.multiple_of(tail_ref[e], TM_E)
            return pltpu.make_async_copy(zero_ref, xs_ref.at[pl.ds(start, TM_E)], zsem)

        for e in range(2 * NE):
            @pl.when(need_ref[e] > 0)
            def _():
                tail_copy(e).start()
        for e in range(2 * NE):
            @pl.when(need_ref[e] > 0)
            def _():
                tail_copy(e).wait()

    def start_row(r):
        _row_copy(h_ref, r, xs_ref, pos_ref[0, r], sem).start()
        _row_copy(h_ref, r, xs_ref, pos_ref[1, r], sem).start()

    _issue_rows(tm, start_row)
    for _ in range(2):
        pltpu.make_async_copy(h_ref, xs_ref.at[pl.ds(0, tm)], sem).wait()


def _dispatch(tail, need, pos, h2, n_rows):
    tm = TM_TOK
    return pl.pallas_call(
        _dispatch_body,
        out_shape=jax.ShapeDtypeStruct((_n_expert_tiles(n_rows) * TM_E, D), F32),
        grid_spec=pltpu.PrefetchScalarGridSpec(
            num_scalar_prefetch=2,
            grid=(n_rows // tm,),
            in_specs=[
                pl.BlockSpec((None, 2, tm), lambda i, t, n: (i, 0, 0), memory_space=pltpu.SMEM),
                pl.BlockSpec((tm, D), lambda i, t, n: (i, 0)),
            ],
            out_specs=pl.BlockSpec(memory_space=pl.ANY),
            scratch_shapes=[pltpu.VMEM((TM_E, D), F32), pltpu.SemaphoreType.DMA(()), pltpu.SemaphoreType.DMA(())],
        ),
        compiler_params=_cparams(("arbitrary",), 32),
        name="dispatch",
    )(tail, need, pos, h2)


def _moe_body(te_ref, nu_ref, xs_ref, wu_ref, wd_ref, ys_ref):
    del te_ref
    r = pl.program_id(0)

    @pl.when(r < nu_ref[0])
    def _():
        x = xs_ref[...].astype(BF16)
        hc = jnp.dot(x, wu_ref[...], preferred_element_type=F32)
        a = hc[:, :DFF]
        b = hc[:, DFF:]
        act = (a * _sigmoid(a) * b).astype(BF16)
        ys_ref[...] = jnp.dot(act, wd_ref[...], preferred_element_type=F32)

    @pl.when(r >= nu_ref[0])
    def _():
        ys_ref[...] = jnp.zeros_like(ys_ref)


def _moe(tile_expert, n_used, xs, wu, wd, layer):
    def row(r, te, nu):
        return (jnp.minimum(r, nu[0] - 1), 0)

    def wsel(r, te, nu):
        return (layer, te[jnp.minimum(r, nu[0] - 1)], 0, 0)

    return pl.pallas_call(
        _moe_body,
        out_shape=jax.ShapeDtypeStruct(xs.shape, F32),
        grid_spec=pltpu.PrefetchScalarGridSpec(
            num_scalar_prefetch=2,
            grid=(xs.shape[0] // TM_E,),
            in_specs=[
                pl.BlockSpec((TM_E, D), row),
                pl.BlockSpec((None, None, D, 2 * DFF), wsel),
                pl.BlockSpec((None, None, DFF, D), wsel),
            ],
            out_specs=pl.BlockSpec((TM_E, D), lambda r, te, nu: (r, 0)),
        ),
        compiler_params=_cparams(("arbitrary",), 48),
        name="moe",
    )(tile_expert, n_used, xs, wu, wd)


def _combine_body(pos_ref, ys_ref, x_ref, w_ref, mod_ref, o_ref, ybuf, sem):
    tm = TM_TOK

    def start_row(r):
        _row_copy(ys_ref, pos_ref[0, r], ybuf.at[0], r, sem).start()
        _row_copy(ys_ref, pos_ref[1, r], ybuf.at[1], r, sem).start()

    _issue_rows(tm, start_row)
    for k in range(2):
        pltpu.make_async_copy(ys_ref.at[pl.ds(0, tm)], ybuf.at[k], sem).wait()

    reps = D // LANES
    w0 = jnp.tile(w_ref[:, 0:LANES], (1, reps))
    w1 = jnp.tile(w_ref[:, LANES:2 * LANES], (1, reps))
    y = w0 * ybuf[0] + w1 * ybuf[1]
    o_ref[...] = x_ref[...] + mod_ref[5:6, :] * y


def _combine(pos, ys, x, wlanes, mods, layer, n_rows):
    tm = TM_TOK
    return pl.pallas_call(
        _combine_body,
        out_shape=jax.ShapeDtypeStruct((n_rows, D), F32),
        grid=(n_rows // tm,),
        in_specs=[
            pl.BlockSpec((None, 2, tm), lambda i: (i, 0, 0), memory_space=pltpu.SMEM),
            pl.BlockSpec(memory_space=pl.ANY),
            pl.BlockSpec((tm, D), lambda i: (i, 0)),
            pl.BlockSpec((tm, 2 * LANES), lambda i: (i, 0)),
            pl.BlockSpec((None, None, 6, D), lambda i: (layer, _mod_row(i, tm), 0, 0)),
        ],
        out_specs=pl.BlockSpec((tm, D), lambda i: (i, 0)),
        scratch_shapes=[pltpu.VMEM((2, tm, D), F32), pltpu.SemaphoreType.DMA(())],
        compiler_params=_cparams(("arbitrary",), 32),
        name="combine",
    )(pos, ys, x, wlanes, mods)


def _route_plan(route, cnt, n_rows):
    counts = cnt[:, 0].astype(I32)
    padded = ((counts + TM_E - 1) // TM_E) * TM_E
    ends = jnp.cumsum(padded)
    offs = ends - padded
    e01 = route[0:2].astype(I32)
    eids = jnp.arange(NE, dtype=I32)[:, None, None]
    off01 = jnp.sum(jnp.where(e01[None] == eids, offs[:, None, None], 0), axis=0)
    pos = off01 + route[2:4].astype(I32)
    pos = pos.reshape(2, n_rows // TM_TOK, TM_TOK).transpose(1, 0, 2)
    nt = _n_expert_tiles(n_rows)
    tile_start = jnp.arange(nt, dtype=I32) * TM_E
    tile_expert = jnp.minimum(jnp.sum((tile_start[:, None] >= ends[None, :]).astype(I32), axis=1), NE - 1)
    n_used = (ends[-1] // TM_E).reshape(1).astype(I32)
    spare = n_used[0] + jnp.arange(NE, dtype=I32)
    zstart = jnp.concatenate([offs + (counts // TM_E) * TM_E, jnp.minimum(spare, nt - 1) * TM_E])
    zneed = jnp.concatenate([counts % TM_E != 0, spare < nt]).astype(I32)
    wl = jnp.concatenate([jnp.broadcast_to(route[4][:, None], (n_rows, LANES)),
                          jnp.broadcast_to(route[5][:, None], (n_rows, LANES))], axis=1)
    return pos, tile_expert.astype(I32), n_used, zstart.astype(I32), zneed, wl


def kernel(x, c, ctx, c_ctx, ada_w, ada_b, norm1_g, norm2_g, w_in, attn_sink, sgu_ln_g, sgu_ln_b, sgu_w, sgu_b,
           sconv_w, conf_dw_w, conf_dw_b, conf_ln_g, conf_ln_b, w_branch, w_out, router_w, router_b,
           exp_w_up, exp_w_down, final_g):
    cvec = jnp.concatenate([c, c_ctx[None, :], jnp.zeros((8 - NB - 1, D), F32)], axis=0)
    mods = _ada(cvec, ada_w, ada_b).reshape(DEPTH, 8, 6, D)
    xa = jnp.concatenate([x.reshape(TX, D), ctx.reshape(TC, D)], axis=0)
    cos, sin = _rope_tables()
    consts = _rope_constants()
    rwt = router_w.T
    rwh = rwt.astype(BF16)
    rwl = (rwt - rwh.astype(F32)).astype(BF16)
    wb_all = w_branch.astype(BF16)
    wo_all = w_out.astype(BF16)
    wu_all = exp_w_up.astype(BF16)
    wd_all = exp_w_down.astype(BF16)

    for l in range(DEPTH):
        last = l == DEPTH - 1
        n_rows = TX if last else T
        n_x_tiles = TX // TM_TOK
        lp = (sgu_ln_g[l], sgu_ln_b[l], sgu_w[l], sgu_b[l], sconv_w[l], conf_dw_w[l], conf_dw_b[l],
              conf_ln_g[l], conf_ln_b[l])

        h = _norm_mod(xa, norm1_g[l], mods, l, T, 0, 1)
        if not last:
            p = _inproj(h, w_in, l, T, N_IN // TN_IN)
            qr, kr, vr = _prep(p, T, n_x_tiles, cos, sin, consts)
            ya = _attn(qr, kr, vr, kr, vr, TX // LC, attn_sink[l])
            yac = _ctx_attn(qr, kr, vr, attn_sink[l])
            ya = jnp.concatenate([ya, yac], axis=0)
        else:
            p = _inproj(h, w_in, l, TX, N_IN // TN_IN)
            pc = _inproj(h[TX:], w_in, l, TC, 1)
            qr, kr, vr = _prep(p, TX, n_x_tiles, cos, sin, consts)
            _, kc, vc = _prep(pc, TC, 0, cos, sin, consts)
            ya = _attn(qr, kr, vr, kc, vc, 0, attn_sink[l])
        yr = _mixers(p, n_rows, n_x_tiles, lp)
        xn, h2, lg = _post(p, ya, yr, wb_all[l], wo_all[l], xa, mods, l, norm2_g[l], rwh, rwl, n_rows)
        route, cnt = _route(lg, router_b, n_rows)
        pos, tile_expert, n_used, tail, need, wl = _route_plan(route, cnt, n_rows)
        xs = _dispatch(tail, need, pos, h2, n_rows)
        ys = _moe(tile_expert, n_used, xs, wu_all, wd_all, l)
        xa = _combine(pos, ys, xn, wl, mods, l, n_rows)

    return _final_norm(xa, final_g).reshape(NB, S, D)
```

```python
import functools

import numpy as np
import jax
import jax.numpy as jnp
from jax import lax
from jax.experimental import pallas as pl
from jax.experimental.pallas import tpu as pltpu

F32 = jnp.float32
BF16 = jnp.bfloat16
I32 = jnp.int32

D = 2048
NB = 4
S = 2048
LC = 256
DEPTH = 2
GRID_W = 64
BW = 512
HD = 64
NH = 8
NKV = 2
REP = NH // NKV
WINDOW = 128
QB = 128
ROPE_THETA = 10000.0
SGU_CHUNK = 128
SGU_GROUPS = 4
SCONV_K = 3
CONF_K = 31
NE = 16
NG = 4
EPG = NE // NG
DFF = D // 2
N_IN = BW + 2 * NKV * HD + 7 * BW + 4 * D
TX = NB * S
TC = NB * LC
T = TX + TC

OFF_Q, OFF_KV = 0, 512
OFF_SU, OFF_SV, OFF_CB, OFF_CC, OFF_CX, OFF_GA, OFF_GB, OFF_GATES = 768, 1280, 1792, 2304, 2816, 3328, 3840, 4352

LANES = 128
V7X_VMEM_BYTES = 64 * 1024 * 1024
MIB = 1024 * 1024

TM_IN = 512
INPROJ_ROW_CHUNK = 256
TN_IN = 1792
TM_TOK = 256
TM_E = 256
HALO = 16
ROW_UNROLL = 8
MOE_W_CHUNK = 512
MOE_UP_CHUNKS = D // MOE_W_CHUNK
MOE_CHUNKS = MOE_UP_CHUNKS + DFF // MOE_W_CHUNK


def _n_expert_tiles(n_rows):
    return (2 * n_rows) // TM_E + NE


def _cparams(sem, vmem_mib):
    return pltpu.CompilerParams(dimension_semantics=sem, vmem_limit_bytes=vmem_mib * MIB)


def _sigmoid(x):
    return 0.5 * jnp.tanh(0.5 * x) + 0.5


def _gelu_tanh(x):
    c = np.float32(np.sqrt(2.0 / np.pi))
    return 0.5 * x * (1.0 + jnp.tanh(c * (x + np.float32(0.044715) * (x * x * x))))


def _layernorm(x, g, b, eps=1e-5):
    mu = jnp.mean(x, axis=-1, keepdims=True)
    xc = x - mu
    var = jnp.mean(xc * xc, axis=-1, keepdims=True)
    return xc * lax.rsqrt(var + eps) * g + b


def _mod_row(i, tm):
    return jnp.where(i < TX // tm, (i * tm) // S, NB)


def _ada_body(c_ref, w_ref, b_ref, o_ref):
    c = c_ref[...]
    s = (c * _sigmoid(c)).astype(BF16)
    o_ref[...] = jnp.dot(s, w_ref[...].astype(BF16), preferred_element_type=F32) + b_ref[...]


def _ada(cvec, ada_w, ada_b):
    tn = 1024
    return pl.pallas_call(
        _ada_body,
        out_shape=jax.ShapeDtypeStruct((DEPTH, 8, 6 * D), F32),
        grid=(DEPTH, 6 * D // tn),
        in_specs=[
            pl.BlockSpec((8, D), lambda l, j: (0, 0)),
            pl.BlockSpec((None, D, tn), lambda l, j: (l, 0, j)),
            pl.BlockSpec((None, 1, tn), lambda l, j: (l, 0, j)),
        ],
        out_specs=pl.BlockSpec((None, 8, tn), lambda l, j: (l, 0, j)),
        compiler_params=_cparams(("arbitrary", "arbitrary"), 40),
        name="ada",
    )(cvec, ada_w, ada_b.reshape(DEPTH, 1, 6 * D))


def _stream_specs(tm, c_blk0, width=D):
    nx = TX // tm
    return [pl.BlockSpec((tm, width), lambda i: (jnp.minimum(i, nx - 1), 0)),
            pl.BlockSpec((tm, width), lambda i: (c_blk0 + jnp.maximum(i - nx, 0), 0))]


def _stream_tile(x_ref, c_ref, tm):
    return jnp.where(pl.program_id(0) < TX // tm, x_ref[...], c_ref[...])


def _norm_mod_body(x_ref, c_ref, g_ref, mod_ref, o_ref, *, sh_idx, sc_idx):
    x = _stream_tile(x_ref, c_ref, TM_IN)
    y = x * lax.rsqrt(jnp.mean(x * x, axis=-1, keepdims=True) + 1e-6) * g_ref[...]
    o_ref[...] = (y * (1.0 + mod_ref[sc_idx:sc_idx + 1, :]) + mod_ref[sh_idx:sh_idx + 1, :]).astype(o_ref.dtype)


def _norm_mod(x, cx, c_blk0, g, mods, layer, n_rows, sh_idx, sc_idx):
    tm = TM_IN
    return pl.pallas_call(
        functools.partial(_norm_mod_body, sh_idx=sh_idx, sc_idx=sc_idx),
        out_shape=jax.ShapeDtypeStruct((n_rows, D), BF16),
        grid=(n_rows // tm,),
        in_specs=_stream_specs(tm, c_blk0) + [
            pl.BlockSpec((1, D), lambda i: (0, 0)),
            pl.BlockSpec((None, None, 6, D), lambda i: (layer, _mod_row(i, tm), 0, 0)),
        ],
        out_specs=pl.BlockSpec((tm, D), lambda i: (i, 0)),
        compiler_params=_cparams(("arbitrary",), 32),
        name="norm_mod",
    )(x, cx, g.reshape(1, D), mods)


def _final_norm_body(x_ref, g_ref, o_ref):
    x = x_ref[...]
    o_ref[...] = x * lax.rsqrt(jnp.mean(x * x, axis=-1, keepdims=True) + 1e-6) * g_ref[...]


def _final_norm(x, g):
    tm = TM_IN
    return pl.pallas_call(
        _final_norm_body,
        out_shape=jax.ShapeDtypeStruct((TX, D), F32),
        grid=(TX // tm,),
        in_specs=[pl.BlockSpec((tm, D), lambda i: (i, 0)), pl.BlockSpec((1, D), lambda i: (0, 0))],
        out_specs=pl.BlockSpec((tm, D), lambda i: (i, 0)),
        compiler_params=_cparams(("arbitrary",), 32),
        name="final_norm",
    )(x, g.reshape(1, D))


def _inproj_body(h_ref, w_ref, o_ref, wbf_ref):
    j = pl.program_id(0)
    i = pl.program_id(1)

    @pl.when(i == 0)
    def _():
        wbf_ref[...] = w_ref[...].astype(BF16)

    col0 = j * TN_IN
    rows = INPROJ_ROW_CHUNK

    def run(epilogue):
        for c in range(h_ref.shape[0] // rows):
            sl = slice(c * rows, (c + 1) * rows)
            acc = jnp.dot(h_ref[sl, :], wbf_ref[...], preferred_element_type=F32)
            o_ref[sl, :] = epilogue(acc).astype(BF16)

    @pl.when(col0 + TN_IN <= OFF_GB)
    def _():
        run(lambda acc: acc)

    @pl.when(col0 >= OFF_GB)
    def _():
        run(_sigmoid)

    @pl.when(jnp.logical_and(col0 < OFF_GB, col0 + TN_IN > OFF_GB))
    def _():
        col = col0 + lax.broadcasted_iota(I32, (rows, TN_IN), 1)
        run(lambda acc: jnp.where(col >= OFF_GB, _sigmoid(acc), acc))


def _inproj(h, w_in, layer, n_rows, n_col_tiles):
    tm = TM_IN
    return pl.pallas_call(
        _inproj_body,
        out_shape=jax.ShapeDtypeStruct((n_rows, n_col_tiles * TN_IN), BF16),
        grid=(n_col_tiles, n_rows // tm),
        in_specs=[
            pl.BlockSpec((tm, D), lambda j, i: (i, 0)),
            pl.BlockSpec((None, D, TN_IN), lambda j, i: (layer, 0, j)),
        ],
        out_specs=pl.BlockSpec((tm, TN_IN), lambda j, i: (i, j)),
        scratch_shapes=[pltpu.VMEM((D, TN_IN), BF16)],
        compiler_params=_cparams(("arbitrary", "arbitrary"), 56),
        name="inproj",
    )(h, w_in)


def _rope_constants():
    rh = np.zeros((HD, HD), np.float32)
    for base in (0, 32):
        for d in range(16):
            rh[base + 16 + d, base + d] = -1.0
            rh[base + d, base + 16 + d] = 1.0
    rq = np.kron(np.eye(NH, dtype=np.float32), rh)
    rk = np.kron(np.eye(NKV, dtype=np.float32), rh)
    rep = np.zeros((NKV * HD, NH * HD), np.float32)
    for g in range(NKV):
        for r in range(REP):
            for d in range(HD):
                rep[g * HD + d, g * REP * HD + r * HD + d] = 1.0
    return jnp.asarray(rq, BF16), jnp.asarray(rep, BF16), jnp.asarray(rk @ rep, BF16)


def _rope_tables():
    half = HD // 2
    inv = 1.0 / (ROPE_THETA ** (jnp.arange(0, half, 2, dtype=F32) / half))
    pos = jnp.arange(S)
    ar = (pos // GRID_W).astype(F32)[:, None] * inv
    ac = (pos % GRID_W).astype(F32)[:, None] * inv
    cos = jnp.tile(jnp.concatenate([jnp.cos(ar), jnp.cos(ar), jnp.cos(ac), jnp.cos(ac)], axis=1), (1, NH))
    sin = jnp.tile(jnp.concatenate([jnp.sin(ar), jnp.sin(ar), jnp.sin(ac), jnp.sin(ac)], axis=1), (1, NH))
    cos = jnp.concatenate([cos, jnp.ones((TM_TOK, NH * HD), F32)], axis=0)
    sin = jnp.concatenate([sin, jnp.zeros((TM_TOK, NH * HD), F32)], axis=0)
    return cos, sin


def _prep_body(q_ref, kv_ref, cos_ref, sin_ref, rq_ref, rep_ref, rrep_ref, qo_ref, ko_ref, vo_ref):
    cos = cos_ref[...]
    sin = sin_ref[...]
    q = q_ref[...]
    qs = jnp.dot(q, rq_ref[...], preferred_element_type=F32)
    qo_ref[...] = ((q.astype(F32) * cos + qs * sin) * (HD ** -0.5)).astype(BF16)
    k = kv_ref[:, 0:NKV * HD]
    v = kv_ref[:, NKV * HD:2 * NKV * HD]
    kr = jnp.dot(k, rep_ref[...], preferred_element_type=F32)
    ks = jnp.dot(k, rrep_ref[...], preferred_element_type=F32)
    ko_ref[...] = (kr * cos + ks * sin).astype(BF16)
    vo_ref[...] = jnp.dot(v, rep_ref[...], preferred_element_type=F32).astype(BF16)


def _prep(p, n_rows, n_x_tiles, cos, sin, consts):
    tm = TM_TOK
    rq, rep, rrep = consts
    tps = S // tm

    def tab(i):
        return (jnp.where(i < n_x_tiles, i % tps, tps), 0)

    w = NH * HD
    return pl.pallas_call(
        _prep_body,
        out_shape=[jax.ShapeDtypeStruct((n_rows, w), BF16)] * 3,
        grid=(n_rows // tm,),
        in_specs=[
            pl.BlockSpec((tm, w), lambda i: (i, 0)),
            pl.BlockSpec((tm, 2 * NKV * HD), lambda i: (i, OFF_KV // (2 * NKV * HD))),
            pl.BlockSpec((tm, w), tab),
            pl.BlockSpec((tm, w), tab),
            pl.BlockSpec((w, w), lambda i: (0, 0)),
            pl.BlockSpec((NKV * HD, w), lambda i: (0, 0)),
            pl.BlockSpec((NKV * HD, w), lambda i: (0, 0)),
        ],
        out_specs=[pl.BlockSpec((tm, w), lambda i: (i, 0))] * 3,
        compiler_params=_cparams(("arbitrary",), 32),
        name="prep",
    )(p, p, cos, sin, rq, rep, rrep)


def _attn_group(q, kb, vb, sink_ref, g, valid, nq):
    gw = REP * HD
    lane_head = lax.broadcasted_iota(I32, (nq, gw), 1) // HD
    qg = q[:, g * gw:(g + 1) * gw]
    qs = jnp.concatenate([jnp.where(lane_head == r, qg, jnp.zeros_like(qg)) for r in range(REP)], axis=0)
    s = lax.dot_general(qs, kb, (((1,), (1,)), ((), ())), preferred_element_type=F32)
    if valid is not None:
        s = jnp.where(jnp.concatenate([valid] * REP, axis=0), s, -jnp.inf)
    row_head = lax.broadcasted_iota(I32, (REP * nq, 1), 0) // nq
    sink = jnp.zeros((REP * nq, 1), F32)
    for r in range(REP):
        sink = jnp.where(row_head == r, sink_ref[g * REP + r], sink)
    m = jnp.maximum(jnp.max(s, axis=-1, keepdims=True), sink)
    e = jnp.exp(s - m)
    den = jnp.sum(e, axis=-1, keepdims=True) + jnp.exp(sink - m)
    p = (e / den).astype(BF16)
    o = jnp.dot(p, vb, preferred_element_type=F32)
    og = jnp.zeros((nq, gw), F32)
    for r in range(REP):
        og = og + jnp.where(lane_head == r, o[r * nq:(r + 1) * nq, :], 0.0)
    return og


def _attn_body(sink_ref, q_ref, kp_ref, kc_ref, kn_ref, vp_ref, vc_ref, vn_ref, kx_ref, vx_ref, o_ref):
    n = pl.program_id(1)
    nblk = S // QB
    nk = 3 * QB + LC
    row = lax.broadcasted_iota(I32, (QB, nk), 0)
    col = lax.broadcasted_iota(I32, (QB, nk), 1)
    lo = jnp.where(n == 0, QB, 0)
    hi = jnp.where(n == nblk - 1, 2 * QB, 3 * QB)
    band = (col >= row) & (col <= row + 2 * WINDOW) & (col >= lo) & (col < hi)
    valid = band | (col >= 3 * QB)
    q = q_ref[...]
    gw = REP * HD
    outs = []
    for g in range(NKV):
        sl = slice(g * gw, (g + 1) * gw)
        kb = jnp.concatenate([kp_ref[:, sl], kc_ref[:, sl], kn_ref[:, sl], kx_ref[:, sl]], axis=0)
        vb = jnp.concatenate([vp_ref[:, sl], vc_ref[:, sl], vn_ref[:, sl], vx_ref[:, sl]], axis=0)
        outs.append(_attn_group(q, kb, vb, sink_ref, g, valid, QB))
    o_ref[...] = jnp.concatenate(outs, axis=1).astype(BF16)


def _attn(qr, kr, vr, kc_arr, vc_arr, ctx_blk0, sink):
    nblk = S // QB
    w = NH * HD

    def cur(b, n):
        return (b * nblk + n, 0)

    def prev(b, n):
        return (b * nblk + jnp.maximum(n - 1, 0), 0)

    def nxt(b, n):
        return (b * nblk + jnp.minimum(n + 1, nblk - 1), 0)

    def cx(b, n):
        return (ctx_blk0 + b, 0)

    blk = lambda f: pl.BlockSpec((QB, w), f)
    return pl.pallas_call(
        _attn_body,
        out_shape=jax.ShapeDtypeStruct((TX, w), BF16),
        grid=(NB, nblk),
        in_specs=[
            pl.BlockSpec(memory_space=pltpu.SMEM),
            blk(cur), blk(prev), blk(cur), blk(nxt), blk(prev), blk(cur), blk(nxt),
            pl.BlockSpec((LC, w), cx), pl.BlockSpec((LC, w), cx),
        ],
        out_specs=blk(cur),
        compiler_params=_cparams(("arbitrary", "arbitrary"), 32),
        name="window_attn",
    )(sink, qr, kr, kr, kr, vr, vr, vr, kc_arr, vc_arr)


def _ctx_attn_body(sink_ref, q_ref, kx_ref, vx_ref, o_ref):
    gw = REP * HD
    q = q_ref[...]
    outs = []
    for g in range(NKV):
        sl = slice(g * gw, (g + 1) * gw)
        outs.append(_attn_group(q, kx_ref[:, sl], vx_ref[:, sl], sink_ref, g, None, LC))
    o_ref[...] = jnp.concatenate(outs, axis=1).astype(BF16)


def _ctx_attn(qr, kr, vr, sink):
    w = NH * HD
    blk0 = TX // LC
    spec = pl.BlockSpec((LC, w), lambda b: (blk0 + b, 0))
    return pl.pallas_call(
        _ctx_attn_body,
        out_shape=jax.ShapeDtypeStruct((TC, w), BF16),
        grid=(NB,),
        in_specs=[pl.BlockSpec(memory_space=pltpu.SMEM), spec, spec, spec],
        out_specs=pl.BlockSpec((LC, w), lambda b: (b, 0)),
        compiler_params=_cparams(("arbitrary",), 32),
        name="ctx_attn",
    )(sink, qr, kr, vr)


def _mix_body(su_ref, sv_ref, cb_ref, cc_ref, cx_ref, ga_ref, gb_ref,
              ccp_ref, cxp_ref, gap_ref, gbp_ref, ccn_ref, cxn_ref, gan_ref, gbn_ref,
              lng_ref, lnb_ref, sw_ref, sbias_ref, scw_ref, dww_ref, dwb_ref, clg_ref, clb_ref,
              o_ref, m_scr, z_scr, *, n_x_tiles):
    tm = TM_TOK
    i = pl.program_id(0)
    tps = S // tm
    is_x = i < n_x_tiles
    first = jnp.logical_or(jnp.logical_not(is_x), (i % tps) == 0)
    last = jnp.logical_or(jnp.logical_not(is_x), (i % tps) == tps - 1)
    keep_p = jnp.where(first, 0.0, 1.0)
    keep_n = jnp.where(last, 0.0, 1.0)

    u = _gelu_tanh(su_ref[...].astype(F32))
    v = _layernorm(_gelu_tanh(sv_ref[...].astype(F32)), lng_ref[...], lnb_ref[...]).astype(BF16)
    gwid = BW // SGU_GROUPS
    chunks = []
    for c in range(tm // SGU_CHUNK):
        parts = []
        for g in range(SGU_GROUPS):
            vc = v[c * SGU_CHUNK:(c + 1) * SGU_CHUNK, g * gwid:(g + 1) * gwid]
            parts.append(jnp.dot(sw_ref[g], vc, preferred_element_type=F32))
        chunks.append(jnp.concatenate(parts, axis=1) + sbias_ref[...])
    yb = u * jnp.concatenate(chunks, axis=0)

    m_scr[HALO:HALO + tm, :] = cc_ref[...].astype(F32) * cx_ref[...].astype(F32)
    m_scr[0:HALO, :] = ccp_ref[...].astype(F32) * cxp_ref[...].astype(F32) * keep_p
    m_scr[HALO + tm:, :] = ccn_ref[...].astype(F32) * cxn_ref[...].astype(F32) * keep_n
    z = jnp.zeros((tm, BW), F32)
    for k in range(SCONV_K):
        z = z + scw_ref[k:k + 1, :] * m_scr[pl.ds(HALO - SCONV_K // 2 + k, tm), :]
    yc = cb_ref[...].astype(F32) * z

    z_scr[HALO:HALO + tm, :] = ga_ref[...].astype(F32) * gb_ref[...].astype(F32)
    z_scr[0:HALO, :] = gap_ref[...].astype(F32) * gbp_ref[...].astype(F32) * keep_p
    z_scr[HALO + tm:, :] = gan_ref[...].astype(F32) * gbn_ref[...].astype(F32) * keep_n
    acc = jnp.zeros((tm, BW), F32) + dwb_ref[...]
    for k in range(CONF_K):
        acc = acc + dww_ref[k:k + 1, :] * z_scr[pl.ds(HALO - CONF_K // 2 + k, tm), :]
    zn = _layernorm(acc, clg_ref[...], clb_ref[...])
    yd = zn * _sigmoid(zn)

    o_ref[...] = jnp.concatenate([yb, yc, yd], axis=1).astype(BF16)


def _mixers(p, n_rows, n_x_tiles, lp):
    tm = TM_TOK
    (sgu_ln_g, sgu_ln_b, sgu_w, sgu_b, sconv_w, conf_dw_w, conf_dw_b, conf_ln_g, conf_ln_b) = lp

    def sec(off):
        return pl.BlockSpec((pl.Element(tm), pl.Element(BW)), lambda i: (i * tm, off))

    hpt = tm // HALO

    def halo_prev(off):
        return pl.BlockSpec((pl.Element(HALO), pl.Element(BW)),
                            lambda i: (jnp.maximum(i * hpt - 1, 0) * HALO, off))

    def halo_next(off):
        return pl.BlockSpec((pl.Element(HALO), pl.Element(BW)),
                            lambda i: (jnp.minimum(i * hpt + hpt, n_rows // HALO - 1) * HALO, off))

    def const(shape):
        return pl.BlockSpec(shape, lambda i: (0,) * len(shape))

    sbias = jnp.repeat(sgu_b.T, BW // SGU_GROUPS, axis=1)
    row = lambda a: a.reshape(1, BW)
    conv_offs = (OFF_CC, OFF_CX, OFF_GA, OFF_GB)
    return pl.pallas_call(
        functools.partial(_mix_body, n_x_tiles=n_x_tiles),
        out_shape=jax.ShapeDtypeStruct((n_rows, 3 * BW), BF16),
        grid=(n_rows // tm,),
        in_specs=[sec(o) for o in (OFF_SU, OFF_SV, OFF_CB, OFF_CC, OFF_CX, OFF_GA, OFF_GB)]
        + [halo_prev(o) for o in conv_offs] + [halo_next(o) for o in conv_offs]
        + [const((1, BW)), const((1, BW)), const((SGU_GROUPS, SGU_CHUNK, SGU_CHUNK)), const((SGU_CHUNK, BW)),
           const((SCONV_K, BW)), const((CONF_K, BW)), const((1, BW)), const((1, BW)), const((1, BW))],
        out_specs=pl.BlockSpec((tm, 3 * BW), lambda i: (i, 0)),
        scratch_shapes=[pltpu.VMEM((tm + 2 * HALO, BW), F32), pltpu.VMEM((tm + 2 * HALO, BW), F32)],
        compiler_params=_cparams(("arbitrary",), 32),
        name="mixers",
    )(*([p] * 15), row(sgu_ln_g), row(sgu_ln_b), sgu_w.astype(BF16), sbias, sconv_w, conf_dw_w,
      row(conf_dw_b), row(conf_ln_g), row(conf_ln_b))


def _post_body(g_ref, ya_ref, yac_ref, yr_ref, wb_ref, wo_ref, x_ref, cx_ref, mod_ref, n2g_ref, rwh_ref, rwl_ref,
               xo_ref, h2_ref, lg_ref):
    merged = None
    for br in range(4):
        y = _stream_tile(ya_ref, yac_ref, TM_TOK) if br == 0 else yr_ref[:, (br - 1) * BW:br * BW]
        pr = jnp.dot(y, wb_ref[br], preferred_element_type=F32)
        term = g_ref[:, br * D:(br + 1) * D].astype(F32) * pr
        merged = term if merged is None else merged + term
    out = jnp.dot(merged.astype(BF16), wo_ref[...], preferred_element_type=F32)
    xn = _stream_tile(x_ref, cx_ref, TM_TOK) + mod_ref[2:3, :] * out
    xo_ref[...] = xn
    y2 = xn * lax.rsqrt(jnp.mean(xn * xn, axis=-1, keepdims=True) + 1e-6) * n2g_ref[...]
    h2 = y2 * (1.0 + mod_ref[4:5, :]) + mod_ref[3:4, :]
    h2_ref[...] = h2
    hi = h2.astype(BF16)
    lo = (h2 - hi.astype(F32)).astype(BF16)
    nt = (((1,), (1,)), ((), ()))
    lg = lax.dot_general(rwh_ref[...], hi, nt, preferred_element_type=F32)
    lg = lg + lax.dot_general(rwh_ref[...], lo, nt, preferred_element_type=F32)
    lg = lg + lax.dot_general(rwl_ref[...], hi, nt, preferred_element_type=F32)
    lg_ref[...] = lg


def _post(p, ya, yac, yr, wb, wo, x, cx, c_blk0, mods, layer, n2g, rwh, rwl, n_rows):
    tm = TM_TOK
    one = pl.Buffered(1)
    return pl.pallas_call(
        _post_body,
        out_shape=[jax.ShapeDtypeStruct((n_rows, D), F32), jax.ShapeDtypeStruct((n_rows, D), F32),
                   jax.ShapeDtypeStruct((NE, n_rows), F32)],
        grid=(n_rows // tm,),
        in_specs=[pl.BlockSpec((pl.Element(tm), pl.Element(4 * D)), lambda i: (i * tm, OFF_GATES))]
        + _stream_specs(tm, 0, BW) + [
            pl.BlockSpec((tm, 3 * BW), lambda i: (i, 0)),
            pl.BlockSpec((4, BW, D), lambda i: (0, 0, 0), pipeline_mode=one),
            pl.BlockSpec((D, D), lambda i: (0, 0), pipeline_mode=one)]
        + _stream_specs(tm, c_blk0) + [
            pl.BlockSpec((None, None, 6, D), lambda i: (layer, _mod_row(i, tm), 0, 0)),
            pl.BlockSpec((1, D), lambda i: (0, 0)),
            pl.BlockSpec((NE, D), lambda i: (0, 0)),
            pl.BlockSpec((NE, D), lambda i: (0, 0)),
        ],
        out_specs=[pl.BlockSpec((tm, D), lambda i: (i, 0)), pl.BlockSpec((tm, D), lambda i: (i, 0)),
                   pl.BlockSpec((NE, tm), lambda i: (0, i))],
        compiler_params=_cparams(("arbitrary",), 52),
        name="post",
    )(p, ya, yac, yr, wb, wo, x, cx, mods, n2g.reshape(1, D), rwh, rwl)


def _route_body(lg_ref, rb_ref, tri_ref, o_ref, cnt_ref, carry_ref):
    tm = TM_TOK
    i = pl.program_id(0)

    @pl.when(i == 0)
    def _():
        carry_ref[...] = jnp.zeros_like(carry_ref)

    lg = lg_ref[...]
    e = jnp.exp(lg - jnp.max(lg, axis=0, keepdims=True))
    sc = e / jnp.sum(e, axis=0, keepdims=True)
    bi = sc + rb_ref[...]
    b = [bi[k:k + 1, :] for k in range(NE)]
    s = [sc[k:k + 1, :] for k in range(NE)]

    gs = []
    for g in range(NG):
        v = b[g * EPG:(g + 1) * EPG]
        best = None
        for a in range(EPG):
            for c in range(a + 1, EPG):
                ps = v[a] + v[c]
                best = ps if best is None else jnp.maximum(best, ps)
        gs.append(best)
    gsel = jnp.zeros((1, tm), I32)
    gbest = gs[0]
    for g in range(1, NG):
        take = gs[g] > gbest
        gsel = jnp.where(take, g, gsel)
        gbest = jnp.where(take, gs[g], gbest)

    vb, vs = [], []
    for j in range(EPG):
        xb, xs = b[j], s[j]
        for g in range(1, NG):
            xb = jnp.where(gsel == g, b[g * EPG + j], xb)
            xs = jnp.where(gsel == g, s[g * EPG + j], xs)
        vb.append(xb)
        vs.append(xs)
    order = []
    for j in range(EPG):
        c = jnp.zeros((1, tm), I32)
        for m in range(EPG):
            if m == j:
                continue
            ahead = (vb[m] >= vb[j]) if m < j else (vb[m] > vb[j])
            c = c + jnp.where(ahead, 1, 0)
        order.append(c)
    zero = jnp.zeros((1, tm), F32)
    w0 = zero
    w1 = zero
    j0 = jnp.zeros((1, tm), I32)
    j1 = jnp.zeros((1, tm), I32)
    for j in range(EPG):
        w0 = jnp.where(order[j] == 0, vs[j], w0)
        w1 = jnp.where(order[j] == 1, vs[j], w1)
        j0 = jnp.where(order[j] == 0, j, j0)
        j1 = jnp.where(order[j] == 1, j, j1)
    tot = w0 + w1
    e0 = gsel * EPG + j0
    e1 = gsel * EPG + j1

    eid = lax.broadcasted_iota(I32, (NE, tm), 0)
    oh0 = eid == e0
    oh1 = eid == e1
    oh = jnp.where(oh0 | oh1, 1.0, 0.0)
    rank = jnp.dot(oh.astype(BF16), tri_ref[...], preferred_element_type=F32) + carry_ref[:, 0:1]
    r0 = jnp.sum(jnp.where(oh0, rank, 0.0), axis=0, keepdims=True)
    r1 = jnp.sum(jnp.where(oh1, rank, 0.0), axis=0, keepdims=True)
    new_carry = carry_ref[...] + jnp.sum(oh, axis=1, keepdims=True)
    carry_ref[...] = new_carry
    cnt_ref[...] = new_carry

    o_ref[0:1, :] = e0.astype(F32)
    o_ref[1:2, :] = e1.astype(F32)
    o_ref[2:3, :] = r0
    o_ref[3:4, :] = r1
    o_ref[4:5, :] = w0 / tot
    o_ref[5:6, :] = w1 / tot
    o_ref[6:8, :] = jnp.zeros((2, tm), F32)


def _route(lg, router_b, n_rows):
    tm = TM_TOK
    tri = jnp.asarray(np.triu(np.ones((tm, tm), np.float32), 1), BF16)
    return pl.pallas_call(
        _route_body,
        out_shape=[jax.ShapeDtypeStruct((8, n_rows), F32), jax.ShapeDtypeStruct((NE, LANES), F32)],
        grid=(n_rows // tm,),
        in_specs=[
            pl.BlockSpec((NE, tm), lambda i: (0, i)),
            pl.BlockSpec((NE, 1), lambda i: (0, 0)),
            pl.BlockSpec((tm, tm), lambda i: (0, 0)),
        ],
        out_specs=[pl.BlockSpec((8, tm), lambda i: (0, i)), pl.BlockSpec((NE, LANES), lambda i: (0, 0))],
        scratch_shapes=[pltpu.VMEM((NE, LANES), F32)],
        compiler_params=_cparams(("arbitrary",), 32),
        name="route",
    )(lg, router_b.reshape(NE, 1), tri)


def _row_copy(src_ref, src_row, dst_ref, dst_row, sem):
    return pltpu.make_async_copy(src_ref.at[pl.ds(src_row, 1)], dst_ref.at[pl.ds(dst_row, 1)], sem)


def _issue_rows(n, start_row):
    def trip(t, c):
        for u in range(ROW_UNROLL):
            start_row(t * ROW_UNROLL + u)
        return c

    lax.fori_loop(0, n // ROW_UNROLL, trip, 0)


def _dispatch_body(tail_ref, need_ref, pos_ref, h_ref, xs_ref, zero_ref, sem, zsem):
    tm = TM_TOK

    @pl.when(pl.program_id(0) == 0)
    def _():
        zero_ref[...] = jnp.zeros_like(zero_ref)

        def tail_copy(e):
            start = pl.multiple_of(tail_ref[e], TM_E)
            return pltpu.make_async_copy(zero_ref, xs_ref.at[pl.ds(start, TM_E)], zsem)

        for e in range(2 * NE):
            @pl.when(need_ref[e] > 0)
            def _():
                tail_copy(e).start()
        for e in range(2 * NE):
            @pl.when(need_ref[e] > 0)
            def _():
                tail_copy(e).wait()

    def start_row(r):
        _row_copy(h_ref, r, xs_ref, pos_ref[0, r], sem).start()
        _row_copy(h_ref, r, xs_ref, pos_ref[1, r], sem).start()

    _issue_rows(tm, start_row)
    for _ in range(2):
        pltpu.make_async_copy(h_ref, xs_ref.at[pl.ds(0, tm)], sem).wait()


def _dispatch(tail, need, pos, h2, n_rows):
    tm = TM_TOK
    return pl.pallas_call(
        _dispatch_body,
        out_shape=jax.ShapeDtypeStruct((_n_expert_tiles(n_rows) * TM_E, D), F32),
        grid_spec=pltpu.PrefetchScalarGridSpec(
            num_scalar_prefetch=2,
            grid=(n_rows // tm,),
            in_specs=[
                pl.BlockSpec((None, 2, tm), lambda i, t, n: (i, 0, 0), memory_space=pltpu.SMEM),
                pl.BlockSpec((tm, D), lambda i, t, n: (i, 0)),
            ],
            out_specs=pl.BlockSpec(memory_space=pl.ANY),
            scratch_shapes=[pltpu.VMEM((TM_E, D), F32), pltpu.SemaphoreType.DMA(()), pltpu.SemaphoreType.DMA(())],
        ),
        compiler_params=_cparams(("arbitrary",), 32),
        name="dispatch",
    )(tail, need, pos, h2)


def _moe_body(te_ref, nu_ref, first_ref, nxt_ref, rem_ref, xs_ref, wu_hbm, wd_hbm, ys_ref,
              wub, wdb, stage, sem, st, *, layer):
    r = pl.program_id(0)

    def chunk_copy(e, k):
        if k < MOE_UP_CHUNKS:
            src = wu_hbm.at[layer, e, pl.ds(k * MOE_W_CHUNK, MOE_W_CHUNK), :]
        else:
            src = wd_hbm.at[layer, e, pl.ds((k - MOE_UP_CHUNKS) * MOE_W_CHUNK, MOE_W_CHUNK), :]
        return pltpu.make_async_copy(src, stage.at[k % 2], sem.at[k % 2])

    def convert(e, k, slot):
        chunk_copy(e, k).wait()
        v = stage[k % 2].astype(BF16)
        if k < MOE_UP_CHUNKS:
            wub[slot, k * MOE_W_CHUNK:(k + 1) * MOE_W_CHUNK, :] = v
        else:
            kd = k - MOE_UP_CHUNKS
            wdb[slot, kd * MOE_W_CHUNK:(kd + 1) * MOE_W_CHUNK, :] = v
        if k + 2 < MOE_CHUNKS:
            chunk_copy(e, k + 2).start()

    def begin(e):
        chunk_copy(e, 0).start()
        chunk_copy(e, 1).start()
        st[1] = 0

    @pl.when(r < nu_ref[0])
    def _():
        e_cur = te_ref[r]
        e_nxt = nxt_ref[r]

        @pl.when(r == 0)
        def _():
            st[0] = 1
            begin(e_cur)

        @pl.when(first_ref[r] == 1)
        def _():
            slot = 1 - st[0]
            done = st[1]
            for k in range(MOE_CHUNKS):
                @pl.when(k >= done)
                def _():
                    convert(e_cur, k, slot)
            st[0] = slot
            st[1] = MOE_CHUNKS

            @pl.when(e_nxt >= 0)
            def _():
                begin(e_nxt)

        cur = st[0]
        x = xs_ref[...].astype(BF16)
        hc = jnp.dot(x, wub[cur], preferred_element_type=F32)
        a = hc[:, :DFF]
        b = hc[:, DFF:]
        act = (a * _sigmoid(a) * b).astype(BF16)
        ys_ref[...] = jnp.dot(act, wdb[cur], preferred_element_type=F32)

        @pl.when(e_nxt >= 0)
        def _():
            done = st[1]
            share = (MOE_CHUNKS - done + rem_ref[r] - 1) // rem_ref[r]
            for k in range(MOE_CHUNKS):
                @pl.when(jnp.logical_and(k >= done, k < done + share))
                def _():
                    convert(e_nxt, k, 1 - cur)
            st[1] = done + share

    @pl.when(r >= nu_ref[0])
    def _():
        ys_ref[...] = jnp.zeros_like(ys_ref)


def _moe(plan, xs, wu, wd, layer):
    def row(r, te, nu, *_):
        return (jnp.minimum(r, nu[0] - 1), 0)

    return pl.pallas_call(
        functools.partial(_moe_body, layer=layer),
        out_shape=jax.ShapeDtypeStruct(xs.shape, F32),
        grid_spec=pltpu.PrefetchScalarGridSpec(
            num_scalar_prefetch=5,
            grid=(xs.shape[0] // TM_E,),
            in_specs=[
                pl.BlockSpec((TM_E, D), row),
                pl.BlockSpec(memory_space=pl.ANY),
                pl.BlockSpec(memory_space=pl.ANY),
            ],
            out_specs=pl.BlockSpec((TM_E, D), lambda r, *_: (r, 0)),
            scratch_shapes=[
                pltpu.VMEM((2, D, 2 * DFF), BF16),
                pltpu.VMEM((2, DFF, D), BF16),
                pltpu.VMEM((2, MOE_W_CHUNK, D), F32),
                pltpu.SemaphoreType.DMA((2,)),
                pltpu.SMEM((2,), I32),
            ],
        ),
        compiler_params=_cparams(("arbitrary",), 56),
        name="moe",
    )(*plan, xs, wu, wd)


def _combine_body(pos_ref, ys_ref, x_ref, w_ref, mod_ref, o_ref, ybuf, sem):
    tm = TM_TOK

    def start_row(r):
        _row_copy(ys_ref, pos_ref[0, r], ybuf.at[0], r, sem).start()
        _row_copy(ys_ref, pos_ref[1, r], ybuf.at[1], r, sem).start()

    _issue_rows(tm, start_row)
    for k in range(2):
        pltpu.make_async_copy(ys_ref.at[pl.ds(0, tm)], ybuf.at[k], sem).wait()

    reps = D // LANES
    w0 = jnp.tile(w_ref[:, 0:LANES], (1, reps))
    w1 = jnp.tile(w_ref[:, LANES:2 * LANES], (1, reps))
    y = w0 * ybuf[0] + w1 * ybuf[1]
    o_ref[...] = x_ref[...] + mod_ref[5:6, :] * y


def _combine(pos, ys, x, wlanes, mods, layer, n_rows):
    tm = TM_TOK
    return pl.pallas_call(
        _combine_body,
        out_shape=jax.ShapeDtypeStruct((n_rows, D), F32),
        grid=(n_rows // tm,),
        in_specs=[
            pl.BlockSpec((None, 2, tm), lambda i: (i, 0, 0), memory_space=pltpu.SMEM),
            pl.BlockSpec(memory_space=pl.ANY),
            pl.BlockSpec((tm, D), lambda i: (i, 0)),
            pl.BlockSpec((tm, 2 * LANES), lambda i: (i, 0)),
            pl.BlockSpec((None, None, 6, D), lambda i: (layer, _mod_row(i, tm), 0, 0)),
        ],
        out_specs=pl.BlockSpec((tm, D), lambda i: (i, 0)),
        scratch_shapes=[pltpu.VMEM((2, tm, D), F32), pltpu.SemaphoreType.DMA(())],
        compiler_params=_cparams(("arbitrary",), 32),
        name="combine",
    )(pos, ys, x, wlanes, mods)


def _route_plan(route, cnt, n_rows):
    counts = cnt[:, 0].astype(I32)
    padded = ((counts + TM_E - 1) // TM_E) * TM_E
    ends = jnp.cumsum(padded)
    offs = ends - padded
    e01 = route[0:2].astype(I32)
    eids = jnp.arange(NE, dtype=I32)[:, None, None]
    off01 = jnp.sum(jnp.where(e01[None] == eids, offs[:, None, None], 0), axis=0)
    pos = off01 + route[2:4].astype(I32)
    pos = pos.reshape(2, n_rows // TM_TOK, TM_TOK).transpose(1, 0, 2)
    nt = _n_expert_tiles(n_rows)
    tile_start = jnp.arange(nt, dtype=I32) * TM_E
    tile_expert = jnp.minimum(jnp.sum((tile_start[:, None] >= ends[None, :]).astype(I32), axis=1), NE - 1)
    n_used = (ends[-1] // TM_E).reshape(1).astype(I32)
    spare = n_used[0] + jnp.arange(NE, dtype=I32)
    zstart = jnp.concatenate([offs + (counts // TM_E) * TM_E, jnp.minimum(spare, nt - 1) * TM_E])
    zneed = jnp.concatenate([counts % TM_E != 0, spare < nt]).astype(I32)
    wl = jnp.concatenate([jnp.broadcast_to(route[4][:, None], (n_rows, LANES)),
                          jnp.broadcast_to(route[5][:, None], (n_rows, LANES))], axis=1)
    ntile = padded // TM_E
    ecol = jnp.arange(NE, dtype=I32)
    onehot = tile_expert[:, None] == ecol[None, :]
    pick = lambda v: jnp.sum(jnp.where(onehot, v[None, :], 0), axis=1)
    j_in = jnp.arange(nt, dtype=I32) - pick(offs // TM_E)
    first = (j_in == 0).astype(I32)
    rem = jnp.maximum(pick(ntile) - j_in, 1)
    later = (ecol[None, :] > ecol[:, None]) & (ntile[None, :] > 0)
    nxt_e = jnp.min(jnp.where(later, ecol[None, :], NE), axis=1)
    nxt = pick(jnp.where(nxt_e < NE, nxt_e, -1))
    moe_plan = (tile_expert.astype(I32), n_used, first, nxt.astype(I32), rem.astype(I32))
    return pos, moe_plan, zstart.astype(I32), zneed, wl


def kernel(x, c, ctx, c_ctx, ada_w, ada_b, norm1_g, norm2_g, w_in, attn_sink, sgu_ln_g, sgu_ln_b, sgu_w, sgu_b,
           sconv_w, conf_dw_w, conf_dw_b, conf_ln_g, conf_ln_b, w_branch, w_out, router_w, router_b,
           exp_w_up, exp_w_down, final_g):
    cvec = jnp.concatenate([c, c_ctx[None, :], jnp.zeros((8 - NB - 1, D), F32)], axis=0)
    mods = _ada(cvec, ada_w, ada_b).reshape(DEPTH, 8, 6, D)
    xa, cxa = x.reshape(TX, D), ctx.reshape(TC, D)
    cos, sin = _rope_tables()
    consts = _rope_constants()
    rwt = router_w.T
    rwh = rwt.astype(BF16)
    rwl = (rwt - rwh.astype(F32)).astype(BF16)
    wb_all = w_branch.astype(BF16)
    wo_all = w_out.astype(BF16)

    for l in range(DEPTH):
        last = l == DEPTH - 1
        n_rows = TX if last else T
        n_x_tiles = TX // TM_TOK
        lp = (sgu_ln_g[l], sgu_ln_b[l], sgu_w[l], sgu_b[l], sconv_w[l], conf_dw_w[l], conf_dw_b[l],
              conf_ln_g[l], conf_ln_b[l])
        c_rows0 = 0 if l == 0 else TX

        h = _norm_mod(xa, cxa, c_rows0 // TM_IN, norm1_g[l], mods, l, T, 0, 1)
        if not last:
            p = _inproj(h, w_in, l, T, N_IN // TN_IN)
            qr, kr, vr = _prep(p, T, n_x_tiles, cos, sin, consts)
            ya = _attn(qr, kr, vr, kr, vr, TX // LC, attn_sink[l])
            yac = _ctx_attn(qr, kr, vr, attn_sink[l])
        else:
            p = _inproj(h, w_in, l, TX, N_IN // TN_IN)
            pc = _inproj(h[TX:], w_in, l, TC, 1)
            qr, kr, vr = _prep(p, TX, n_x_tiles, cos, sin, consts)
            _, kc, vc = _prep(pc, TC, 0, cos, sin, consts)
            ya = _attn(qr, kr, vr, kc, vc, 0, attn_sink[l])
            yac = ya
        yr = _mixers(p, n_rows, n_x_tiles, lp)
        xn, h2, lg = _post(p, ya, yac, yr, wb_all[l], wo_all[l], xa, cxa, c_rows0 // TM_TOK, mods, l,
                           norm2_g[l], rwh, rwl, n_rows)
        route, cnt = _route(lg, router_b, n_rows)
        pos, moe_plan, zstart, zneed, wl = _route_plan(route, cnt, n_rows)
        xs = _dispatch(zstart, zneed, pos, h2, n_rows)
        ys = _moe(moe_plan, xs, exp_w_up, exp_w_down, l)
        xa = _combine(pos, ys, xn, wl, mods, l, n_rows)
        cxa = xa

    return _final_norm(xa, final_g).reshape(NB, S, D)
```

```python
import functools

import numpy as np
import jax
import jax.numpy as jnp
from jax import lax
from jax.experimental import pallas as pl
from jax.experimental.pallas import tpu as pltpu

F32 = jnp.float32
BF16 = jnp.bfloat16
I32 = jnp.int32

D = 2048
NB = 4
S = 2048
LC = 256
DEPTH = 2
GRID_W = 64
BW = 512
HD = 64
NH = 8
NKV = 2
REP = NH // NKV
WINDOW = 128
QB = 128
ROPE_THETA = 10000.0
SGU_CHUNK = 128
SGU_GROUPS = 4
SCONV_K = 3
CONF_K = 31
NE = 16
NG = 4
EPG = NE // NG
DFF = D // 2
N_IN = BW + 2 * NKV * HD + 7 * BW + 4 * D
TX = NB * S
TC = NB * LC
T = TX + TC

OFF_Q, OFF_KV = 0, 512
OFF_SU, OFF_SV, OFF_CB, OFF_CC, OFF_CX, OFF_GA, OFF_GB, OFF_GATES = 768, 1280, 1792, 2304, 2816, 3328, 3840, 4352

LANES = 128
SUBLANES = 8
V7X_VMEM_BYTES = 64 * 1024 * 1024
MIB = 1024 * 1024

TM_NORM = 512
TM_IN = 1024
INPROJ_ROW_CHUNK = 512
IN_W_CHUNK = 256
IN_CHUNKS = D // IN_W_CHUNK
TN_IN = 1792
TM_TOK = 256
TM_E = 256
HALO = 16
ROW_UNROLL = 8
MOE_W_CHUNK = 512
MOE_UP_CHUNKS = D // MOE_W_CHUNK
MOE_CHUNKS = MOE_UP_CHUNKS + DFF // MOE_W_CHUNK


def _n_expert_tiles(n_rows):
    return (2 * n_rows) // TM_E + NE


def _cparams(sem, vmem_mib):
    return pltpu.CompilerParams(dimension_semantics=sem, vmem_limit_bytes=vmem_mib * MIB)


def _sigmoid(x):
    return 0.5 * jnp.tanh(0.5 * x) + 0.5


def _gelu_tanh(x):
    c = np.float32(np.sqrt(2.0 / np.pi))
    return 0.5 * x * (1.0 + jnp.tanh(c * (x + np.float32(0.044715) * (x * x * x))))


def _layernorm(x, g, b, eps=1e-5):
    mu = jnp.mean(x, axis=-1, keepdims=True)
    xc = x - mu
    var = jnp.mean(xc * xc, axis=-1, keepdims=True)
    return xc * lax.rsqrt(var + eps) * g + b


def _mod_row(i, tm):
    return jnp.where(i < TX // tm, (i * tm) // S, NB)


def _ada_body(c_ref, w_ref, b_ref, o_ref):
    c = c_ref[...]
    s = (c * _sigmoid(c)).astype(BF16)
    o_ref[...] = jnp.dot(s, w_ref[...].astype(BF16), preferred_element_type=F32) + b_ref[...]


def _ada(cvec, ada_w, ada_b):
    tn = 1024
    return pl.pallas_call(
        _ada_body,
        out_shape=jax.ShapeDtypeStruct((DEPTH, 8, 6 * D), F32),
        grid=(DEPTH, 6 * D // tn),
        in_specs=[
            pl.BlockSpec((8, D), lambda l, j: (0, 0)),
            pl.BlockSpec((None, D, tn), lambda l, j: (l, 0, j)),
            pl.BlockSpec((None, 1, tn), lambda l, j: (l, 0, j)),
        ],
        out_specs=pl.BlockSpec((None, 8, tn), lambda l, j: (l, 0, j)),
        compiler_params=_cparams(("arbitrary", "arbitrary"), 40),
        name="ada",
    )(cvec, ada_w, ada_b.reshape(DEPTH, 1, 6 * D))


def _stream_specs(tm, c_blk0, width=D):
    nx = TX // tm
    return [pl.BlockSpec((tm, width), lambda i: (jnp.minimum(i, nx - 1), 0)),
            pl.BlockSpec((tm, width), lambda i: (c_blk0 + jnp.maximum(i - nx, 0), 0))]


def _stream_tile(x_ref, c_ref, tm):
    return jnp.where(pl.program_id(0) < TX // tm, x_ref[...], c_ref[...])


def _rms(x, g):
    return x * lax.rsqrt(jnp.mean(x * x, axis=-1, keepdims=True) + 1e-6) * g


def _norm_mod_body(x_ref, c_ref, g_ref, mod_ref, o_ref):
    y = _rms(_stream_tile(x_ref, c_ref, TM_NORM), g_ref[...])
    o_ref[...] = (y * (1.0 + mod_ref[1:2, :]) + mod_ref[0:1, :]).astype(o_ref.dtype)


def _norm_mod(x, cx, g, mods, layer):
    tm = TM_NORM
    return pl.pallas_call(
        _norm_mod_body,
        out_shape=jax.ShapeDtypeStruct((T, D), BF16),
        grid=(T // tm,),
        in_specs=_stream_specs(tm, 0) + [
            pl.BlockSpec((1, D), lambda i: (0, 0)),
            pl.BlockSpec((None, None, 6, D), lambda i: (layer, _mod_row(i, tm), 0, 0)),
        ],
        out_specs=pl.BlockSpec((tm, D), lambda i: (i, 0)),
        compiler_params=_cparams(("arbitrary",), 32),
        name="norm_mod",
    )(x, cx, g.reshape(1, D), mods)


def _inproj_body(h_ref, w_hbm, o_ref, wbf, stage, sem, st, *, layer):
    j = pl.program_id(0)
    i = pl.program_id(1)
    nj = pl.num_programs(0)
    ni = pl.num_programs(1)

    def chunk_copy(jj, k):
        cols = pl.ds(pl.multiple_of(jj * TN_IN, LANES), TN_IN)
        src = w_hbm.at[layer, pl.ds(k * IN_W_CHUNK, IN_W_CHUNK), cols]
        return pltpu.make_async_copy(src, stage.at[k % 2], sem.at[k % 2])

    def convert(jj, k, slot):
        chunk_copy(jj, k).wait()
        wbf[slot, k * IN_W_CHUNK:(k + 1) * IN_W_CHUNK, :] = stage[k % 2].astype(BF16)
        if k + 2 < IN_CHUNKS:
            chunk_copy(jj, k + 2).start()

    def begin(jj):
        chunk_copy(jj, 0).start()
        chunk_copy(jj, 1).start()
        st[1] = 0

    @pl.when(jnp.logical_and(i == 0, j == 0))
    def _():
        st[0] = 1
        begin(0)

    @pl.when(i == 0)
    def _():
        slot = 1 - st[0]
        done = st[1]
        for k in range(IN_CHUNKS):
            @pl.when(k >= done)
            def _():
                convert(j, k, slot)
        st[0] = slot
        st[1] = IN_CHUNKS

        @pl.when(j + 1 < nj)
        def _():
            begin(j + 1)

    cur = st[0]
    col0 = j * TN_IN
    rows = INPROJ_ROW_CHUNK

    def run(epilogue):
        for c in range(h_ref.shape[0] // rows):
            sl = slice(c * rows, (c + 1) * rows)
            acc = jnp.dot(h_ref[sl, :], wbf[cur], preferred_element_type=F32)
            o_ref[sl, :] = epilogue(acc).astype(BF16)

    @pl.when(col0 + TN_IN <= OFF_GB)
    def _():
        run(lambda acc: acc)

    @pl.when(col0 >= OFF_GB)
    def _():
        run(_sigmoid)

    @pl.when(jnp.logical_and(col0 < OFF_GB, col0 + TN_IN > OFF_GB))
    def _():
        col = col0 + lax.broadcasted_iota(I32, (rows, TN_IN), 1)
        run(lambda acc: jnp.where(col >= OFF_GB, _sigmoid(acc), acc))

    @pl.when(j + 1 < nj)
    def _():
        done = st[1]
        left = ni - i
        share = (IN_CHUNKS - done + left - 1) // left
        for k in range(IN_CHUNKS):
            @pl.when(jnp.logical_and(k >= done, k < done + share))
            def _():
                convert(j + 1, k, 1 - cur)
        st[1] = done + share


def _inproj(h, w_in, layer, n_rows, n_col_tiles):
    tm = TM_IN
    return pl.pallas_call(
        functools.partial(_inproj_body, layer=layer),
        out_shape=jax.ShapeDtypeStruct((n_rows, n_col_tiles * TN_IN), BF16),
        grid=(n_col_tiles, n_rows // tm),
        in_specs=[
            pl.BlockSpec((tm, D), lambda j, i: (i, 0)),
            pl.BlockSpec(memory_space=pl.ANY),
        ],
        out_specs=pl.BlockSpec((tm, TN_IN), lambda j, i: (i, j)),
        scratch_shapes=[
            pltpu.VMEM((2, D, TN_IN), BF16),
            pltpu.VMEM((2, IN_W_CHUNK, TN_IN), F32),
            pltpu.SemaphoreType.DMA((2,)),
            pltpu.SMEM((2,), I32),
        ],
        compiler_params=_cparams(("arbitrary", "arbitrary"), 56),
        name="inproj",
    )(h, w_in)


def _rope_constants():
    rh = np.zeros((HD, HD), np.float32)
    for base in (0, 32):
        for d in range(16):
            rh[base + 16 + d, base + d] = -1.0
            rh[base + d, base + 16 + d] = 1.0
    rq = np.kron(np.eye(NH, dtype=np.float32), rh)
    rk = np.kron(np.eye(NKV, dtype=np.float32), rh)
    rep = np.zeros((NKV * HD, NH * HD), np.float32)
    for g in range(NKV):
        for r in range(REP):
            for d in range(HD):
                rep[g * HD + d, g * REP * HD + r * HD + d] = 1.0
    return jnp.asarray(rq, BF16), jnp.asarray(rep, BF16), jnp.asarray(rk @ rep, BF16)


def _rope_tables():
    half = HD // 2
    inv = 1.0 / (ROPE_THETA ** (jnp.arange(0, half, 2, dtype=F32) / half))
    pos = jnp.arange(S)
    ar = (pos // GRID_W).astype(F32)[:, None] * inv
    ac = (pos % GRID_W).astype(F32)[:, None] * inv
    cos = jnp.tile(jnp.concatenate([jnp.cos(ar), jnp.cos(ar), jnp.cos(ac), jnp.cos(ac)], axis=1), (1, NH))
    sin = jnp.tile(jnp.concatenate([jnp.sin(ar), jnp.sin(ar), jnp.sin(ac), jnp.sin(ac)], axis=1), (1, NH))
    cos = jnp.concatenate([cos, jnp.ones((TM_TOK, NH * HD), F32)], axis=0)
    sin = jnp.concatenate([sin, jnp.zeros((TM_TOK, NH * HD), F32)], axis=0)
    return cos, sin


def _prep_body(q_ref, kv_ref, cos_ref, sin_ref, rq_ref, rep_ref, rrep_ref, qo_ref, ko_ref, vo_ref):
    cos = cos_ref[...]
    sin = sin_ref[...]
    q = q_ref[...]
    qs = jnp.dot(q, rq_ref[...], preferred_element_type=F32)
    qo_ref[...] = ((q.astype(F32) * cos + qs * sin) * (HD ** -0.5)).astype(BF16)
    k = kv_ref[:, 0:NKV * HD]
    v = kv_ref[:, NKV * HD:2 * NKV * HD]
    kr = jnp.dot(k, rep_ref[...], preferred_element_type=F32)
    ks = jnp.dot(k, rrep_ref[...], preferred_element_type=F32)
    ko_ref[...] = (kr * cos + ks * sin).astype(BF16)
    vo_ref[...] = jnp.dot(v, rep_ref[...], preferred_element_type=F32).astype(BF16)


def _prep(p, n_rows, n_x_tiles, cos, sin, consts):
    tm = TM_TOK
    rq, rep, rrep = consts
    tps = S // tm

    def tab(i):
        return (jnp.where(i < n_x_tiles, i % tps, tps), 0)

    w = NH * HD
    return pl.pallas_call(
        _prep_body,
        out_shape=[jax.ShapeDtypeStruct((n_rows, w), BF16)] * 3,
        grid=(n_rows // tm,),
        in_specs=[
            pl.BlockSpec((tm, w), lambda i: (i, 0)),
            pl.BlockSpec((tm, 2 * NKV * HD), lambda i: (i, OFF_KV // (2 * NKV * HD))),
            pl.BlockSpec((tm, w), tab),
            pl.BlockSpec((tm, w), tab),
            pl.BlockSpec((w, w), lambda i: (0, 0)),
            pl.BlockSpec((NKV * HD, w), lambda i: (0, 0)),
            pl.BlockSpec((NKV * HD, w), lambda i: (0, 0)),
        ],
        out_specs=[pl.BlockSpec((tm, w), lambda i: (i, 0))] * 3,
        compiler_params=_cparams(("arbitrary",), 32),
        name="prep",
    )(p, p, cos, sin, rq, rep, rrep)


def _attn_group(q, kb, vb, sink_ref, g, valid, nq):
    gw = REP * HD
    lane_head = lax.broadcasted_iota(I32, (nq, gw), 1) // HD
    qg = q[:, g * gw:(g + 1) * gw]
    qs = jnp.concatenate([jnp.where(lane_head == r, qg, jnp.zeros_like(qg)) for r in range(REP)], axis=0)
    s = lax.dot_general(qs, kb, (((1,), (1,)), ((), ())), preferred_element_type=F32)
    if valid is not None:
        s = jnp.where(jnp.concatenate([valid] * REP, axis=0), s, -jnp.inf)
    row_head = lax.broadcasted_iota(I32, (REP * nq, 1), 0) // nq
    sink = jnp.zeros((REP * nq, 1), F32)
    for r in range(REP):
        sink = jnp.where(row_head == r, sink_ref[g * REP + r], sink)
    m = jnp.maximum(jnp.max(s, axis=-1, keepdims=True), sink)
    e = jnp.exp(s - m)
    den = jnp.sum(e, axis=-1, keepdims=True) + jnp.exp(sink - m)
    p = (e / den).astype(BF16)
    o = jnp.dot(p, vb, preferred_element_type=F32)
    og = jnp.zeros((nq, gw), F32)
    for r in range(REP):
        og = og + jnp.where(lane_head == r, o[r * nq:(r + 1) * nq, :], 0.0)
    return og


def _attn_body(sink_ref, q_ref, kp_ref, kc_ref, kn_ref, vp_ref, vc_ref, vn_ref, kx_ref, vx_ref, o_ref):
    n = pl.program_id(1)
    nblk = S // QB
    nk = 3 * QB + LC
    row = lax.broadcasted_iota(I32, (QB, nk), 0)
    col = lax.broadcasted_iota(I32, (QB, nk), 1)
    lo = jnp.where(n == 0, QB, 0)
    hi = jnp.where(n == nblk - 1, 2 * QB, 3 * QB)
    band = (col >= row) & (col <= row + 2 * WINDOW) & (col >= lo) & (col < hi)
    valid = band | (col >= 3 * QB)
    q = q_ref[...]
    gw = REP * HD
    outs = []
    for g in range(NKV):
        sl = slice(g * gw, (g + 1) * gw)
        kb = jnp.concatenate([kp_ref[:, sl], kc_ref[:, sl], kn_ref[:, sl], kx_ref[:, sl]], axis=0)
        vb = jnp.concatenate([vp_ref[:, sl], vc_ref[:, sl], vn_ref[:, sl], vx_ref[:, sl]], axis=0)
        outs.append(_attn_group(q, kb, vb, sink_ref, g, valid, QB))
    o_ref[...] = jnp.concatenate(outs, axis=1).astype(BF16)


def _attn(qr, kr, vr, kc_arr, vc_arr, ctx_blk0, sink):
    nblk = S // QB
    w = NH * HD

    def cur(b, n):
        return (b * nblk + n, 0)

    def prev(b, n):
        return (b * nblk + jnp.maximum(n - 1, 0), 0)

    def nxt(b, n):
        return (b * nblk + jnp.minimum(n + 1, nblk - 1), 0)

    def cx(b, n):
        return (ctx_blk0 + b, 0)

    blk = lambda f: pl.BlockSpec((QB, w), f)
    return pl.pallas_call(
        _attn_body,
        out_shape=jax.ShapeDtypeStruct((TX, w), BF16),
        grid=(NB, nblk),
        in_specs=[
            pl.BlockSpec(memory_space=pltpu.SMEM),
            blk(cur), blk(prev), blk(cur), blk(nxt), blk(prev), blk(cur), blk(nxt),
            pl.BlockSpec((LC, w), cx), pl.BlockSpec((LC, w), cx),
        ],
        out_specs=blk(cur),
        compiler_params=_cparams(("arbitrary", "arbitrary"), 32),
        name="window_attn",
    )(sink, qr, kr, kr, kr, vr, vr, vr, kc_arr, vc_arr)


def _ctx_attn_body(sink_ref, q_ref, kx_ref, vx_ref, o_ref):
    gw = REP * HD
    q = q_ref[...]
    outs = []
    for g in range(NKV):
        sl = slice(g * gw, (g + 1) * gw)
        outs.append(_attn_group(q, kx_ref[:, sl], vx_ref[:, sl], sink_ref, g, None, LC))
    o_ref[...] = jnp.concatenate(outs, axis=1).astype(BF16)


def _ctx_attn(qr, kr, vr, sink):
    w = NH * HD
    blk0 = TX // LC
    spec = pl.BlockSpec((LC, w), lambda b: (blk0 + b, 0))
    return pl.pallas_call(
        _ctx_attn_body,
        out_shape=jax.ShapeDtypeStruct((TC, w), BF16),
        grid=(NB,),
        in_specs=[pl.BlockSpec(memory_space=pltpu.SMEM), spec, spec, spec],
        out_specs=pl.BlockSpec((LC, w), lambda b: (b, 0)),
        compiler_params=_cparams(("arbitrary",), 32),
        name="ctx_attn",
    )(sink, qr, kr, vr)


def _mix_body(su_ref, sv_ref, cb_ref, cc_ref, cx_ref, ga_ref, gb_ref,
              ccp_ref, cxp_ref, gap_ref, gbp_ref, ccn_ref, cxn_ref, gan_ref, gbn_ref,
              lng_ref, lnb_ref, sw_ref, sbias_ref, scw_ref, dww_ref, dwb_ref, clg_ref, clb_ref,
              o_ref, m_scr, z_scr, *, n_x_tiles):
    tm = TM_TOK
    i = pl.program_id(0)
    tps = S // tm
    is_x = i < n_x_tiles
    first = jnp.logical_or(jnp.logical_not(is_x), (i % tps) == 0)
    last = jnp.logical_or(jnp.logical_not(is_x), (i % tps) == tps - 1)
    keep_p = jnp.where(first, 0.0, 1.0)
    keep_n = jnp.where(last, 0.0, 1.0)

    u = _gelu_tanh(su_ref[...].astype(F32))
    v = _layernorm(_gelu_tanh(sv_ref[...].astype(F32)), lng_ref[...], lnb_ref[...]).astype(BF16)
    gwid = BW // SGU_GROUPS
    chunks = []
    for c in range(tm // SGU_CHUNK):
        parts = []
        for g in range(SGU_GROUPS):
            vc = v[c * SGU_CHUNK:(c + 1) * SGU_CHUNK, g * gwid:(g + 1) * gwid]
            parts.append(jnp.dot(sw_ref[g], vc, preferred_element_type=F32))
        chunks.append(jnp.concatenate(parts, axis=1) + sbias_ref[...])
    yb = u * jnp.concatenate(chunks, axis=0)

    m_scr[HALO:HALO + tm, :] = cc_ref[...].astype(F32) * cx_ref[...].astype(F32)
    m_scr[0:HALO, :] = ccp_ref[...].astype(F32) * cxp_ref[...].astype(F32) * keep_p
    m_scr[HALO + tm:, :] = ccn_ref[...].astype(F32) * cxn_ref[...].astype(F32) * keep_n
    z = jnp.zeros((tm, BW), F32)
    for k in range(SCONV_K):
        z = z + scw_ref[k:k + 1, :] * m_scr[pl.ds(HALO - SCONV_K // 2 + k, tm), :]
    yc = cb_ref[...].astype(F32) * z

    z_scr[0, HALO:HALO + tm, :] = ga_ref[...].astype(F32) * gb_ref[...].astype(F32)
    z_scr[0, 0:HALO, :] = gap_ref[...].astype(F32) * gbp_ref[...].astype(F32) * keep_p
    z_scr[0, HALO + tm:, :] = gan_ref[...].astype(F32) * gbn_ref[...].astype(F32) * keep_n
    n_sh = tm + 2 * HALO - SUBLANES
    for s in range(1, SUBLANES):
        z_scr[s, 0:n_sh, :] = z_scr[0, pl.ds(s, n_sh), :]
    acc = jnp.zeros((tm, BW), F32) + dwb_ref[...]
    for k in range(CONF_K):
        off = HALO - CONF_K // 2 + k
        base = off - off % SUBLANES
        acc = acc + dww_ref[k:k + 1, :] * z_scr[off % SUBLANES, base:base + tm, :]
    zn = _layernorm(acc, clg_ref[...], clb_ref[...])
    yd = zn * _sigmoid(zn)

    o_ref[...] = jnp.concatenate([yb, yc, yd], axis=1).astype(BF16)


def _mixers(p, n_rows, n_x_tiles, lp):
    tm = TM_TOK
    (sgu_ln_g, sgu_ln_b, sgu_w, sgu_b, sconv_w, conf_dw_w, conf_dw_b, conf_ln_g, conf_ln_b) = lp

    def sec(off):
        return pl.BlockSpec((pl.Element(tm), pl.Element(BW)), lambda i: (i * tm, off))

    hpt = tm // HALO

    def halo_prev(off):
        return pl.BlockSpec((pl.Element(HALO), pl.Element(BW)),
                            lambda i: (jnp.maximum(i * hpt - 1, 0) * HALO, off))

    def halo_next(off):
        return pl.BlockSpec((pl.Element(HALO), pl.Element(BW)),
                            lambda i: (jnp.minimum(i * hpt + hpt, n_rows // HALO - 1) * HALO, off))

    def const(shape):
        return pl.BlockSpec(shape, lambda i: (0,) * len(shape))

    sbias = jnp.repeat(sgu_b.T, BW // SGU_GROUPS, axis=1)
    row = lambda a: a.reshape(1, BW)
    conv_offs = (OFF_CC, OFF_CX, OFF_GA, OFF_GB)
    return pl.pallas_call(
        functools.partial(_mix_body, n_x_tiles=n_x_tiles),
        out_shape=jax.ShapeDtypeStruct((n_rows, 3 * BW), BF16),
        grid=(n_rows // tm,),
        in_specs=[sec(o) for o in (OFF_SU, OFF_SV, OFF_CB, OFF_CC, OFF_CX, OFF_GA, OFF_GB)]
        + [halo_prev(o) for o in conv_offs] + [halo_next(o) for o in conv_offs]
        + [const((1, BW)), const((1, BW)), const((SGU_GROUPS, SGU_CHUNK, SGU_CHUNK)), const((SGU_CHUNK, BW)),
           const((SCONV_K, BW)), const((CONF_K, BW)), const((1, BW)), const((1, BW)), const((1, BW))],
        out_specs=pl.BlockSpec((tm, 3 * BW), lambda i: (i, 0)),
        scratch_shapes=[pltpu.VMEM((tm + 2 * HALO, BW), F32), pltpu.VMEM((SUBLANES, tm + 2 * HALO, BW), F32)],
        compiler_params=_cparams(("arbitrary",), 32),
        name="mixers",
    )(*([p] * 15), row(sgu_ln_g), row(sgu_ln_b), sgu_w.astype(BF16), sbias, sconv_w, conf_dw_w,
      row(conf_dw_b), row(conf_ln_g), row(conf_ln_b))


def _post_body(g_ref, ya_ref, yac_ref, yr_ref, wb_ref, wo_ref, x_ref, cx_ref, mod_ref, n2g_ref, rwh_ref, rwl_ref,
               xo_ref, h2_ref, lg_ref):
    merged = None
    for br in range(4):
        y = _stream_tile(ya_ref, yac_ref, TM_TOK) if br == 0 else yr_ref[:, (br - 1) * BW:br * BW]
        pr = jnp.dot(y, wb_ref[br], preferred_element_type=F32)
        term = g_ref[:, br * D:(br + 1) * D].astype(F32) * pr
        merged = term if merged is None else merged + term
    out = jnp.dot(merged.astype(BF16), wo_ref[...], preferred_element_type=F32)
    xn = _stream_tile(x_ref, cx_ref, TM_TOK) + mod_ref[2:3, :] * out
    xo_ref[...] = xn
    y2 = xn * lax.rsqrt(jnp.mean(xn * xn, axis=-1, keepdims=True) + 1e-6) * n2g_ref[...]
    h2 = y2 * (1.0 + mod_ref[4:5, :]) + mod_ref[3:4, :]
    h2_ref[...] = h2
    hi = h2.astype(BF16)
    lo = (h2 - hi.astype(F32)).astype(BF16)
    nt = (((1,), (1,)), ((), ()))
    lg = lax.dot_general(rwh_ref[...], hi, nt, preferred_element_type=F32)
    lg = lg + lax.dot_general(rwh_ref[...], lo, nt, preferred_element_type=F32)
    lg = lg + lax.dot_general(rwl_ref[...], hi, nt, preferred_element_type=F32)
    lg_ref[...] = lg


def _post(p, ya, yac, yr, wb, wo, x, cx, c_blk0, mods, layer, n2g, rwh, rwl, n_rows):
    tm = TM_TOK
    one = pl.Buffered(1)
    return pl.pallas_call(
        _post_body,
        out_shape=[jax.ShapeDtypeStruct((n_rows, D), F32), jax.ShapeDtypeStruct((n_rows, D), F32),
                   jax.ShapeDtypeStruct((NE, n_rows), F32)],
        grid=(n_rows // tm,),
        in_specs=[pl.BlockSpec((pl.Element(tm), pl.Element(4 * D)), lambda i: (i * tm, OFF_GATES))]
        + _stream_specs(tm, 0, BW) + [
            pl.BlockSpec((tm, 3 * BW), lambda i: (i, 0)),
            pl.BlockSpec((4, BW, D), lambda i: (0, 0, 0), pipeline_mode=one),
            pl.BlockSpec((D, D), lambda i: (0, 0), pipeline_mode=one)]
        + _stream_specs(tm, c_blk0) + [
            pl.BlockSpec((None, None, 6, D), lambda i: (layer, _mod_row(i, tm), 0, 0)),
            pl.BlockSpec((1, D), lambda i: (0, 0)),
            pl.BlockSpec((NE, D), lambda i: (0, 0)),
            pl.BlockSpec((NE, D), lambda i: (0, 0)),
        ],
        out_specs=[pl.BlockSpec((tm, D), lambda i: (i, 0)), pl.BlockSpec((tm, D), lambda i: (i, 0)),
                   pl.BlockSpec((NE, tm), lambda i: (0, i))],
        compiler_params=_cparams(("arbitrary",), 52),
        name="post",
    )(p, ya, yac, yr, wb, wo, x, cx, mods, n2g.reshape(1, D), rwh, rwl)


def _route_body(lg_ref, rb_ref, tri_ref, o_ref, cnt_ref, carry_ref):
    tm = TM_TOK
    i = pl.program_id(0)

    @pl.when(i == 0)
    def _():
        carry_ref[...] = jnp.zeros_like(carry_ref)

    lg = lg_ref[...]
    e = jnp.exp(lg - jnp.max(lg, axis=0, keepdims=True))
    sc = e / jnp.sum(e, axis=0, keepdims=True)
    bi = sc + rb_ref[...]
    b = [bi[k:k + 1, :] for k in range(NE)]
    s = [sc[k:k + 1, :] for k in range(NE)]

    gs = []
    for g in range(NG):
        v = b[g * EPG:(g + 1) * EPG]
        best = None
        for a in range(EPG):
            for c in range(a + 1, EPG):
                ps = v[a] + v[c]
                best = ps if best is None else jnp.maximum(best, ps)
        gs.append(best)
    gsel = jnp.zeros((1, tm), I32)
    gbest = gs[0]
    for g in range(1, NG):
        take = gs[g] > gbest
        gsel = jnp.where(take, g, gsel)
        gbest = jnp.where(take, gs[g], gbest)

    vb, vs = [], []
    for j in range(EPG):
        xb, xs = b[j], s[j]
        for g in range(1, NG):
            xb = jnp.where(gsel == g, b[g * EPG + j], xb)
            xs = jnp.where(gsel == g, s[g * EPG + j], xs)
        vb.append(xb)
        vs.append(xs)
    order = []
    for j in range(EPG):
        c = jnp.zeros((1, tm), I32)
        for m in range(EPG):
            if m == j:
                continue
            ahead = (vb[m] >= vb[j]) if m < j else (vb[m] > vb[j])
            c = c + jnp.where(ahead, 1, 0)
        order.append(c)
    zero = jnp.zeros((1, tm), F32)
    w0 = zero
    w1 = zero
    j0 = jnp.zeros((1, tm), I32)
    j1 = jnp.zeros((1, tm), I32)
    for j in range(EPG):
        w0 = jnp.where(order[j] == 0, vs[j], w0)
        w1 = jnp.where(order[j] == 1, vs[j], w1)
        j0 = jnp.where(order[j] == 0, j, j0)
        j1 = jnp.where(order[j] == 1, j, j1)
    tot = w0 + w1
    e0 = gsel * EPG + j0
    e1 = gsel * EPG + j1

    eid = lax.broadcasted_iota(I32, (NE, tm), 0)
    oh0 = eid == e0
    oh1 = eid == e1
    oh = jnp.where(oh0 | oh1, 1.0, 0.0)
    rank = jnp.dot(oh.astype(BF16), tri_ref[...], preferred_element_type=F32) + carry_ref[:, 0:1]
    r0 = jnp.sum(jnp.where(oh0, rank, 0.0), axis=0, keepdims=True)
    r1 = jnp.sum(jnp.where(oh1, rank, 0.0), axis=0, keepdims=True)
    new_carry = carry_ref[...] + jnp.sum(oh, axis=1, keepdims=True)
    carry_ref[...] = new_carry
    cnt_ref[...] = new_carry

    o_ref[0:1, :] = e0.astype(F32)
    o_ref[1:2, :] = e1.astype(F32)
    o_ref[2:3, :] = r0
    o_ref[3:4, :] = r1
    o_ref[4:5, :] = w0 / tot
    o_ref[5:6, :] = w1 / tot
    o_ref[6:8, :] = jnp.zeros((2, tm), F32)


def _route(lg, router_b, n_rows):
    tm = TM_TOK
    tri = jnp.asarray(np.triu(np.ones((tm, tm), np.float32), 1), BF16)
    return pl.pallas_call(
        _route_body,
        out_shape=[jax.ShapeDtypeStruct((8, n_rows), F32), jax.ShapeDtypeStruct((NE, LANES), F32)],
        grid=(n_rows // tm,),
        in_specs=[
            pl.BlockSpec((NE, tm), lambda i: (0, i)),
            pl.BlockSpec((NE, 1), lambda i: (0, 0)),
            pl.BlockSpec((tm, tm), lambda i: (0, 0)),
        ],
        out_specs=[pl.BlockSpec((8, tm), lambda i: (0, i)), pl.BlockSpec((NE, LANES), lambda i: (0, 0))],
        scratch_shapes=[pltpu.VMEM((NE, LANES), F32)],
        compiler_params=_cparams(("arbitrary",), 32),
        name="route",
    )(lg, router_b.reshape(NE, 1), tri)


def _row_copy(src_ref, src_row, dst_ref, dst_row, sem):
    return pltpu.make_async_copy(src_ref.at[pl.ds(src_row, 1)], dst_ref.at[pl.ds(dst_row, 1)], sem)


def _issue_rows(n, start_row):
    def trip(t, c):
        for u in range(ROW_UNROLL):
            start_row(t * ROW_UNROLL + u)
        return c

    lax.fori_loop(0, n // ROW_UNROLL, trip, 0)


def _dispatch_body(tail_ref, need_ref, pos_ref, h_ref, xs_ref, zero_ref, sem, zsem):
    tm = TM_TOK

    @pl.when(pl.program_id(0) == 0)
    def _():
        zero_ref[...] = jnp.zeros_like(zero_ref)

        def tail_copy(e):
            start = pl.multiple_of(tail_ref[e], TM_E)
            return pltpu.make_async_copy(zero_ref, xs_ref.at[pl.ds(start, TM_E)], zsem)

        for e in range(2 * NE):
            @pl.when(need_ref[e] > 0)
            def _():
                tail_copy(e).start()
        for e in range(2 * NE):
            @pl.when(need_ref[e] > 0)
            def _():
                tail_copy(e).wait()

    def start_row(r):
        _row_copy(h_ref, r, xs_ref, pos_ref[0, r], sem).start()
        _row_copy(h_ref, r, xs_ref, pos_ref[1, r], sem).start()

    _issue_rows(tm, start_row)
    for _ in range(2):
        pltpu.make_async_copy(h_ref, xs_ref.at[pl.ds(0, tm)], sem).wait()


def _dispatch(tail, need, pos, h2, n_rows):
    tm = TM_TOK
    return pl.pallas_call(
        _dispatch_body,
        out_shape=jax.ShapeDtypeStruct((_n_expert_tiles(n_rows) * TM_E, D), F32),
        grid_spec=pltpu.PrefetchScalarGridSpec(
            num_scalar_prefetch=2,
            grid=(n_rows // tm,),
            in_specs=[
                pl.BlockSpec((None, 2, tm), lambda i, t, n: (i, 0, 0), memory_space=pltpu.SMEM),
                pl.BlockSpec((tm, D), lambda i, t, n: (i, 0)),
            ],
            out_specs=pl.BlockSpec(memory_space=pl.ANY),
            scratch_shapes=[pltpu.VMEM((TM_E, D), F32), pltpu.SemaphoreType.DMA(()), pltpu.SemaphoreType.DMA(())],
        ),
        compiler_params=_cparams(("arbitrary",), 32),
        name="dispatch",
    )(tail, need, pos, h2)


def _moe_body(te_ref, nu_ref, first_ref, nxt_ref, rem_ref, xs_ref, wu_hbm, wd_hbm, ys_ref,
              wub, wdb, stage, sem, st, *, layer):
    r = pl.program_id(0)

    def chunk_copy(e, k):
        if k < MOE_UP_CHUNKS:
            src = wu_hbm.at[layer, e, pl.ds(k * MOE_W_CHUNK, MOE_W_CHUNK), :]
        else:
            src = wd_hbm.at[layer, e, pl.ds((k - MOE_UP_CHUNKS) * MOE_W_CHUNK, MOE_W_CHUNK), :]
        return pltpu.make_async_copy(src, stage.at[k % 2], sem.at[k % 2])

    def convert(e, k, slot):
        chunk_copy(e, k).wait()
        v = stage[k % 2].astype(BF16)
        if k < MOE_UP_CHUNKS:
            wub[slot, k * MOE_W_CHUNK:(k + 1) * MOE_W_CHUNK, :] = v
        else:
            kd = k - MOE_UP_CHUNKS
            wdb[slot, kd * MOE_W_CHUNK:(kd + 1) * MOE_W_CHUNK, :] = v
        if k + 2 < MOE_CHUNKS:
            chunk_copy(e, k + 2).start()

    def begin(e):
        chunk_copy(e, 0).start()
        chunk_copy(e, 1).start()
        st[1] = 0

    @pl.when(r < nu_ref[0])
    def _():
        e_cur = te_ref[r]
        e_nxt = nxt_ref[r]

        @pl.when(r == 0)
        def _():
            st[0] = 1
            begin(e_cur)

        @pl.when(first_ref[r] == 1)
        def _():
            slot = 1 - st[0]
            done = st[1]
            for k in range(MOE_CHUNKS):
                @pl.when(k >= done)
                def _():
                    convert(e_cur, k, slot)
            st[0] = slot
            st[1] = MOE_CHUNKS

            @pl.when(e_nxt >= 0)
            def _():
                begin(e_nxt)

        cur = st[0]
        x = xs_ref[...].astype(BF16)
        hc = jnp.dot(x, wub[cur], preferred_element_type=F32)
        a = hc[:, :DFF]
        b = hc[:, DFF:]
        act = (a * _sigmoid(a) * b).astype(BF16)
        ys_ref[...] = jnp.dot(act, wdb[cur], preferred_element_type=F32)

        @pl.when(e_nxt >= 0)
        def _():
            done = st[1]
            share = (MOE_CHUNKS - done + rem_ref[r] - 1) // rem_ref[r]
            for k in range(MOE_CHUNKS):
                @pl.when(jnp.logical_and(k >= done, k < done + share))
                def _():
                    convert(e_nxt, k, 1 - cur)
            st[1] = done + share

    @pl.when(r >= nu_ref[0])
    def _():
        ys_ref[...] = jnp.zeros_like(ys_ref)


def _moe(plan, xs, wu, wd, layer):
    def row(r, te, nu, *_):
        return (jnp.minimum(r, nu[0] - 1), 0)

    return pl.pallas_call(
        functools.partial(_moe_body, layer=layer),
        out_shape=jax.ShapeDtypeStruct(xs.shape, F32),
        grid_spec=pltpu.PrefetchScalarGridSpec(
            num_scalar_prefetch=5,
            grid=(xs.shape[0] // TM_E,),
            in_specs=[
                pl.BlockSpec((TM_E, D), row),
                pl.BlockSpec(memory_space=pl.ANY),
                pl.BlockSpec(memory_space=pl.ANY),
            ],
            out_specs=pl.BlockSpec((TM_E, D), lambda r, *_: (r, 0)),
            scratch_shapes=[
                pltpu.VMEM((2, D, 2 * DFF), BF16),
                pltpu.VMEM((2, DFF, D), BF16),
                pltpu.VMEM((2, MOE_W_CHUNK, D), F32),
                pltpu.SemaphoreType.DMA((2,)),
                pltpu.SMEM((2,), I32),
            ],
        ),
        compiler_params=_cparams(("arbitrary",), 56),
        name="moe",
    )(*plan, xs, wu, wd)


def _combine_body(pos_ref, posn_ref, ys_ref, x_ref, w_ref, mod_ref, g_ref, nmod_ref, *rest, final):
    tm = TM_TOK
    outs, (ybuf, sem) = rest[:-2], rest[-2:]
    i = pl.program_id(0)
    slot = i % 2

    def fetch(p_ref, s):
        def start_row(r):
            _row_copy(ys_ref, p_ref[0, r], ybuf.at[s, 0], r, sem.at[s]).start()
            _row_copy(ys_ref, p_ref[1, r], ybuf.at[s, 1], r, sem.at[s]).start()

        _issue_rows(tm, start_row)

    @pl.when(i == 0)
    def _():
        fetch(pos_ref, 0)

    @pl.when(i + 1 < pl.num_programs(0))
    def _():
        fetch(posn_ref, 1 - slot)

    for k in range(2):
        pltpu.make_async_copy(ys_ref.at[pl.ds(0, tm)], ybuf.at[slot, k], sem.at[slot]).wait()

    reps = D // LANES
    w0 = jnp.tile(w_ref[:, 0:LANES], (1, reps))
    w1 = jnp.tile(w_ref[:, LANES:2 * LANES], (1, reps))
    xn = x_ref[...] + mod_ref[5:6, :] * (w0 * ybuf[slot, 0] + w1 * ybuf[slot, 1])
    if final:
        outs[0][...] = _rms(xn, g_ref[...])
    else:
        outs[0][...] = xn
        outs[1][...] = (_rms(xn, g_ref[...]) * (1.0 + nmod_ref[1:2, :]) + nmod_ref[0:1, :]).astype(BF16)


def _combine(pos, ys, x, wlanes, mods, layer, n_rows, g_next, final):
    tm = TM_TOK
    nt = n_rows // tm
    next_layer = min(layer + 1, DEPTH - 1)
    tile = pl.BlockSpec((tm, D), lambda i: (i, 0))
    out_shape = [jax.ShapeDtypeStruct((n_rows, D), F32)]
    if not final:
        out_shape.append(jax.ShapeDtypeStruct((n_rows, D), BF16))
    return pl.pallas_call(
        functools.partial(_combine_body, final=final),
        out_shape=out_shape,
        grid=(nt,),
        in_specs=[
            pl.BlockSpec((None, 2, tm), lambda i: (i, 0, 0), memory_space=pltpu.SMEM),
            pl.BlockSpec((None, 2, tm), lambda i: (jnp.minimum(i + 1, nt - 1), 0, 0), memory_space=pltpu.SMEM),
            pl.BlockSpec(memory_space=pl.ANY),
            tile,
            pl.BlockSpec((tm, 2 * LANES), lambda i: (i, 0)),
            pl.BlockSpec((None, None, 6, D), lambda i: (layer, _mod_row(i, tm), 0, 0)),
            pl.BlockSpec((1, D), lambda i: (0, 0)),
            pl.BlockSpec((None, None, 6, D), lambda i: (next_layer, _mod_row(i, tm), 0, 0)),
        ],
        out_specs=[tile] * len(out_shape),
        scratch_shapes=[pltpu.VMEM((2, 2, tm, D), F32), pltpu.SemaphoreType.DMA((2,))],
        compiler_params=_cparams(("arbitrary",), 40),
        name="combine",
    )(pos, pos, ys, x, wlanes, mods, g_next.reshape(1, D), mods)


def _route_plan(route, cnt, n_rows):
    counts = cnt[:, 0].astype(I32)
    padded = ((counts + TM_E - 1) // TM_E) * TM_E
    ends = jnp.cumsum(padded)
    offs = ends - padded
    e01 = route[0:2].astype(I32)
    eids = jnp.arange(NE, dtype=I32)[:, None, None]
    off01 = jnp.sum(jnp.where(e01[None] == eids, offs[:, None, None], 0), axis=0)
    pos = off01 + route[2:4].astype(I32)
    pos = pos.reshape(2, n_rows // TM_TOK, TM_TOK).transpose(1, 0, 2)
    nt = _n_expert_tiles(n_rows)
    tile_start = jnp.arange(nt, dtype=I32) * TM_E
    tile_expert = jnp.minimum(jnp.sum((tile_start[:, None] >= ends[None, :]).astype(I32), axis=1), NE - 1)
    n_used = (ends[-1] // TM_E).reshape(1).astype(I32)
    spare = n_used[0] + jnp.arange(NE, dtype=I32)
    zstart = jnp.concatenate([offs + (counts // TM_E) * TM_E, jnp.minimum(spare, nt - 1) * TM_E])
    zneed = jnp.concatenate([counts % TM_E != 0, spare < nt]).astype(I32)
    wl = jnp.concatenate([jnp.broadcast_to(route[4][:, None], (n_rows, LANES)),
                          jnp.broadcast_to(route[5][:, None], (n_rows, LANES))], axis=1)
    ntile = padded // TM_E
    ecol = jnp.arange(NE, dtype=I32)
    onehot = tile_expert[:, None] == ecol[None, :]
    pick = lambda v: jnp.sum(jnp.where(onehot, v[None, :], 0), axis=1)
    j_in = jnp.arange(nt, dtype=I32) - pick(offs // TM_E)
    first = (j_in == 0).astype(I32)
    rem = jnp.maximum(pick(ntile) - j_in, 1)
    later = (ecol[None, :] > ecol[:, None]) & (ntile[None, :] > 0)
    nxt_e = jnp.min(jnp.where(later, ecol[None, :], NE), axis=1)
    nxt = pick(jnp.where(nxt_e < NE, nxt_e, -1))
    moe_plan = (tile_expert.astype(I32), n_used, first, nxt.astype(I32), rem.astype(I32))
    return pos, moe_plan, zstart.astype(I32), zneed, wl


def kernel(x, c, ctx, c_ctx, ada_w, ada_b, norm1_g, norm2_g, w_in, attn_sink, sgu_ln_g, sgu_ln_b, sgu_w, sgu_b,
           sconv_w, conf_dw_w, conf_dw_b, conf_ln_g, conf_ln_b, w_branch, w_out, router_w, router_b,
           exp_w_up, exp_w_down, final_g):
    cvec = jnp.concatenate([c, c_ctx[None, :], jnp.zeros((8 - NB - 1, D), F32)], axis=0)
    mods = _ada(cvec, ada_w, ada_b).reshape(DEPTH, 8, 6, D)
    xa, cxa = x.reshape(TX, D), ctx.reshape(TC, D)
    cos, sin = _rope_tables()
    consts = _rope_constants()
    rwt = router_w.T
    rwh = rwt.astype(BF16)
    rwl = (rwt - rwh.astype(F32)).astype(BF16)
    wb_all = w_branch.astype(BF16)
    wo_all = w_out.astype(BF16)

    for l in range(DEPTH):
        last = l == DEPTH - 1
        n_rows = TX if last else T
        n_x_tiles = TX // TM_TOK
        lp = (sgu_ln_g[l], sgu_ln_b[l], sgu_w[l], sgu_b[l], sconv_w[l], conf_dw_w[l], conf_dw_b[l],
              conf_ln_g[l], conf_ln_b[l])
        c_rows0 = 0 if l == 0 else TX

        if l == 0:
            h = _norm_mod(xa, cxa, norm1_g[l], mods, l)
        if not last:
            p = _inproj(h, w_in, l, T, N_IN // TN_IN)
            qr, kr, vr = _prep(p, T, n_x_tiles, cos, sin, consts)
            ya = _attn(qr, kr, vr, kr, vr, TX // LC, attn_sink[l])
            yac = _ctx_attn(qr, kr, vr, attn_sink[l])
        else:
            p = _inproj(h, w_in, l, TX, N_IN // TN_IN)
            pc = _inproj(h[TX:], w_in, l, TC, 1)
            qr, kr, vr = _prep(p, TX, n_x_tiles, cos, sin, consts)
            _, kc, vc = _prep(pc, TC, 0, cos, sin, consts)
            ya = _attn(qr, kr, vr, kc, vc, 0, attn_sink[l])
            yac = ya
        yr = _mixers(p, n_rows, n_x_tiles, lp)
        xn, h2, lg = _post(p, ya, yac, yr, wb_all[l], wo_all[l], xa, cxa, c_rows0 // TM_TOK, mods, l,
                           norm2_g[l], rwh, rwl, n_rows)
        route, cnt = _route(lg, router_b, n_rows)
        pos, moe_plan, zstart, zneed, wl = _route_plan(route, cnt, n_rows)
        xs = _dispatch(zstart, zneed, pos, h2, n_rows)
        ys = _moe(moe_plan, xs, exp_w_up, exp_w_down, l)
        if last:
            (out,) = _combine(pos, ys, xn, wl, mods, l, n_rows, final_g, True)
        else:
            xa, h = _combine(pos, ys, xn, wl, mods, l, n_rows, norm1_g[l + 1], False)
            cxa = xa

    return out.reshape(NB, S, D)
```

```python
import functools

import numpy as np
import jax
import jax.numpy as jnp
from jax import lax
from jax.experimental import pallas as pl
from jax.experimental.pallas import tpu as pltpu

F32 = jnp.float32
BF16 = jnp.bfloat16
I32 = jnp.int32

D = 2048
NB = 4
S = 2048
LC = 256
DEPTH = 2
GRID_W = 64
BW = 512
HD = 64
NH = 8
NKV = 2
REP = NH // NKV
WINDOW = 128
QB = 128
ROPE_THETA = 10000.0
SGU_CHUNK = 128
SGU_GROUPS = 4
SCONV_K = 3
CONF_K = 31
NE = 16
NG = 4
EPG = NE // NG
DFF = D // 2
N_IN = BW + 2 * NKV * HD + 7 * BW + 4 * D
TX = NB * S
TC = NB * LC
T = TX + TC

OFF_Q, OFF_KV = 0, 512
OFF_SU, OFF_SV, OFF_CB, OFF_CC, OFF_CX, OFF_GA, OFF_GB, OFF_GATES = 768, 1280, 1792, 2304, 2816, 3328, 3840, 4352

LANES = 128
SUBLANES = 8
V7X_VMEM_BYTES = 64 * 1024 * 1024
MIB = 1024 * 1024

TM_NORM = 512
TM_IN = 1024
INPROJ_ROW_CHUNK = 512
IN_W_CHUNK = 256
IN_CHUNKS = D // IN_W_CHUNK
TN_IN = 1792
TM_TOK = 256
TM_POST = 512
POST_ROW_CHUNK = 256
TM_E = 256
HALO = 16
ROW_UNROLL = 8
MOE_W_CHUNK = 512
MOE_UP_CHUNKS = D // MOE_W_CHUNK
MOE_CHUNKS = MOE_UP_CHUNKS + DFF // MOE_W_CHUNK


def _n_expert_tiles(n_rows):
    return (2 * n_rows) // TM_E + NE


def _cparams(sem, vmem_mib):
    return pltpu.CompilerParams(dimension_semantics=sem, vmem_limit_bytes=vmem_mib * MIB)


def _sigmoid(x):
    return 0.5 * jnp.tanh(0.5 * x) + 0.5


def _gelu_tanh(x):
    c = np.float32(np.sqrt(2.0 / np.pi))
    return 0.5 * x * (1.0 + jnp.tanh(c * (x + np.float32(0.044715) * (x * x * x))))


def _layernorm(x, g, b, eps=1e-5):
    mu = jnp.mean(x, axis=-1, keepdims=True)
    xc = x - mu
    var = jnp.mean(xc * xc, axis=-1, keepdims=True)
    return xc * lax.rsqrt(var + eps) * g + b


def _mod_row(i, tm):
    return jnp.where(i < TX // tm, (i * tm) // S, NB)


def _ada_body(c_ref, w_ref, b_ref, o_ref):
    c = c_ref[...]
    s = (c * _sigmoid(c)).astype(BF16)
    o_ref[...] = jnp.dot(s, w_ref[...].astype(BF16), preferred_element_type=F32) + b_ref[...]


def _ada(cvec, ada_w, ada_b):
    tn = 1024
    return pl.pallas_call(
        _ada_body,
        out_shape=jax.ShapeDtypeStruct((DEPTH, 8, 6 * D), F32),
        grid=(DEPTH, 6 * D // tn),
        in_specs=[
            pl.BlockSpec((8, D), lambda l, j: (0, 0)),
            pl.BlockSpec((None, D, tn), lambda l, j: (l, 0, j)),
            pl.BlockSpec((None, 1, tn), lambda l, j: (l, 0, j)),
        ],
        out_specs=pl.BlockSpec((None, 8, tn), lambda l, j: (l, 0, j)),
        compiler_params=_cparams(("arbitrary", "arbitrary"), 40),
        name="ada",
    )(cvec, ada_w, ada_b.reshape(DEPTH, 1, 6 * D))


def _stream_specs(tm, c_blk0, width=D):
    nx = TX // tm
    return [pl.BlockSpec((tm, width), lambda i, *_: (jnp.minimum(i, nx - 1), 0)),
            pl.BlockSpec((tm, width), lambda i, *_: (c_blk0 + jnp.maximum(i - nx, 0), 0))]


def _stream_tile(x_ref, c_ref, tm):
    return jnp.where(pl.program_id(0) < TX // tm, x_ref[...], c_ref[...])


def _rms(x, g):
    return x * lax.rsqrt(jnp.mean(x * x, axis=-1, keepdims=True) + 1e-6) * g


def _norm_mod_body(x_ref, c_ref, g_ref, mod_ref, o_ref):
    y = _rms(_stream_tile(x_ref, c_ref, TM_NORM), g_ref[...])
    o_ref[...] = (y * (1.0 + mod_ref[1:2, :]) + mod_ref[0:1, :]).astype(o_ref.dtype)


def _norm_mod(x, cx, g, mods, layer):
    tm = TM_NORM
    return pl.pallas_call(
        _norm_mod_body,
        out_shape=jax.ShapeDtypeStruct((T, D), BF16),
        grid=(T // tm,),
        in_specs=_stream_specs(tm, 0) + [
            pl.BlockSpec((1, D), lambda i: (0, 0)),
            pl.BlockSpec((None, None, 6, D), lambda i: (layer, _mod_row(i, tm), 0, 0)),
        ],
        out_specs=pl.BlockSpec((tm, D), lambda i: (i, 0)),
        compiler_params=_cparams(("arbitrary",), 32),
        name="norm_mod",
    )(x, cx, g.reshape(1, D), mods)


def _inproj_body(h_ref, w_hbm, o_ref, wbf, stage, sem, st, *, layer):
    j = pl.program_id(0)
    i = pl.program_id(1)
    nj = pl.num_programs(0)
    ni = pl.num_programs(1)

    def chunk_copy(jj, k):
        cols = pl.ds(pl.multiple_of(jj * TN_IN, LANES), TN_IN)
        src = w_hbm.at[layer, pl.ds(k * IN_W_CHUNK, IN_W_CHUNK), cols]
        return pltpu.make_async_copy(src, stage.at[k % 2], sem.at[k % 2])

    def convert(jj, k, slot):
        chunk_copy(jj, k).wait()
        wbf[slot, k * IN_W_CHUNK:(k + 1) * IN_W_CHUNK, :] = stage[k % 2].astype(BF16)
        if k + 2 < IN_CHUNKS:
            chunk_copy(jj, k + 2).start()

    def begin(jj):
        chunk_copy(jj, 0).start()
        chunk_copy(jj, 1).start()
        st[1] = 0

    @pl.when(jnp.logical_and(i == 0, j == 0))
    def _():
        st[0] = 1
        begin(0)

    @pl.when(i == 0)
    def _():
        slot = 1 - st[0]
        done = st[1]
        for k in range(IN_CHUNKS):
            @pl.when(k >= done)
            def _():
                convert(j, k, slot)
        st[0] = slot
        st[1] = IN_CHUNKS

        @pl.when(j + 1 < nj)
        def _():
            begin(j + 1)

    cur = st[0]
    col0 = j * TN_IN
    rows = INPROJ_ROW_CHUNK

    def run(epilogue):
        for c in range(h_ref.shape[0] // rows):
            sl = slice(c * rows, (c + 1) * rows)
            acc = jnp.dot(h_ref[sl, :], wbf[cur], preferred_element_type=F32)
            o_ref[sl, :] = epilogue(acc).astype(BF16)

    @pl.when(col0 + TN_IN <= OFF_GB)
    def _():
        run(lambda acc: acc)

    @pl.when(col0 >= OFF_GB)
    def _():
        run(_sigmoid)

    @pl.when(jnp.logical_and(col0 < OFF_GB, col0 + TN_IN > OFF_GB))
    def _():
        col = col0 + lax.broadcasted_iota(I32, (rows, TN_IN), 1)
        run(lambda acc: jnp.where(col >= OFF_GB, _sigmoid(acc), acc))

    @pl.when(j + 1 < nj)
    def _():
        done = st[1]
        left = ni - i
        share = (IN_CHUNKS - done + left - 1) // left
        for k in range(IN_CHUNKS):
            @pl.when(jnp.logical_and(k >= done, k < done + share))
            def _():
                convert(j + 1, k, 1 - cur)
        st[1] = done + share


def _inproj(h, w_in, layer, n_rows, n_col_tiles):
    tm = TM_IN
    return pl.pallas_call(
        functools.partial(_inproj_body, layer=layer),
        out_shape=jax.ShapeDtypeStruct((n_rows, n_col_tiles * TN_IN), BF16),
        grid=(n_col_tiles, n_rows // tm),
        in_specs=[
            pl.BlockSpec((tm, D), lambda j, i: (i, 0)),
            pl.BlockSpec(memory_space=pl.ANY),
        ],
        out_specs=pl.BlockSpec((tm, TN_IN), lambda j, i: (i, j)),
        scratch_shapes=[
            pltpu.VMEM((2, D, TN_IN), BF16),
            pltpu.VMEM((2, IN_W_CHUNK, TN_IN), F32),
            pltpu.SemaphoreType.DMA((2,)),
            pltpu.SMEM((2,), I32),
        ],
        compiler_params=_cparams(("arbitrary", "arbitrary"), 56),
        name="inproj",
    )(h, w_in)


def _rope_constants():
    rh = np.zeros((HD, HD), np.float32)
    for base in (0, 32):
        for d in range(16):
            rh[base + 16 + d, base + d] = -1.0
            rh[base + d, base + 16 + d] = 1.0
    rq = np.kron(np.eye(NH, dtype=np.float32), rh)
    rk = np.kron(np.eye(NKV, dtype=np.float32), rh)
    rep = np.zeros((NKV * HD, NH * HD), np.float32)
    for g in range(NKV):
        for r in range(REP):
            for d in range(HD):
                rep[g * HD + d, g * REP * HD + r * HD + d] = 1.0
    return jnp.asarray(rq, BF16), jnp.asarray(rep, BF16), jnp.asarray(rk @ rep, BF16)


def _rope_tables():
    half = HD // 2
    inv = 1.0 / (ROPE_THETA ** (jnp.arange(0, half, 2, dtype=F32) / half))
    pos = jnp.arange(S)
    ar = (pos // GRID_W).astype(F32)[:, None] * inv
    ac = (pos % GRID_W).astype(F32)[:, None] * inv
    cos = jnp.tile(jnp.concatenate([jnp.cos(ar), jnp.cos(ar), jnp.cos(ac), jnp.cos(ac)], axis=1), (1, NH))
    sin = jnp.tile(jnp.concatenate([jnp.sin(ar), jnp.sin(ar), jnp.sin(ac), jnp.sin(ac)], axis=1), (1, NH))
    cos = jnp.concatenate([cos, jnp.ones((TM_TOK, NH * HD), F32)], axis=0)
    sin = jnp.concatenate([sin, jnp.zeros((TM_TOK, NH * HD), F32)], axis=0)
    return cos, sin


def _prep_body(q_ref, kv_ref, cos_ref, sin_ref, rq_ref, rep_ref, rrep_ref, qo_ref, ko_ref, vo_ref):
    cos = cos_ref[...]
    sin = sin_ref[...]
    q = q_ref[...]
    qs = jnp.dot(q, rq_ref[...], preferred_element_type=F32)
    qo_ref[...] = ((q.astype(F32) * cos + qs * sin) * (HD ** -0.5)).astype(BF16)
    k = kv_ref[:, 0:NKV * HD]
    v = kv_ref[:, NKV * HD:2 * NKV * HD]
    kr = jnp.dot(k, rep_ref[...], preferred_element_type=F32)
    ks = jnp.dot(k, rrep_ref[...], preferred_element_type=F32)
    ko_ref[...] = (kr * cos + ks * sin).astype(BF16)
    vo_ref[...] = jnp.dot(v, rep_ref[...], preferred_element_type=F32).astype(BF16)


def _prep(p, n_rows, n_x_tiles, cos, sin, consts):
    tm = TM_TOK
    rq, rep, rrep = consts
    tps = S // tm

    def tab(i):
        return (jnp.where(i < n_x_tiles, i % tps, tps), 0)

    w = NH * HD
    return pl.pallas_call(
        _prep_body,
        out_shape=[jax.ShapeDtypeStruct((n_rows, w), BF16)] * 3,
        grid=(n_rows // tm,),
        in_specs=[
            pl.BlockSpec((tm, w), lambda i: (i, 0)),
            pl.BlockSpec((tm, 2 * NKV * HD), lambda i: (i, OFF_KV // (2 * NKV * HD))),
            pl.BlockSpec((tm, w), tab),
            pl.BlockSpec((tm, w), tab),
            pl.BlockSpec((w, w), lambda i: (0, 0)),
            pl.BlockSpec((NKV * HD, w), lambda i: (0, 0)),
            pl.BlockSpec((NKV * HD, w), lambda i: (0, 0)),
        ],
        out_specs=[pl.BlockSpec((tm, w), lambda i: (i, 0))] * 3,
        compiler_params=_cparams(("arbitrary",), 32),
        name="prep",
    )(p, p, cos, sin, rq, rep, rrep)


def _attn_group(q, kb, vb, sink_ref, g, valid, nq):
    gw = REP * HD
    lane_head = lax.broadcasted_iota(I32, (nq, gw), 1) // HD
    qg = q[:, g * gw:(g + 1) * gw]
    qs = jnp.concatenate([jnp.where(lane_head == r, qg, jnp.zeros_like(qg)) for r in range(REP)], axis=0)
    s = lax.dot_general(qs, kb, (((1,), (1,)), ((), ())), preferred_element_type=F32)
    if valid is not None:
        s = jnp.where(jnp.concatenate([valid] * REP, axis=0), s, -jnp.inf)
    row_head = lax.broadcasted_iota(I32, (REP * nq, 1), 0) // nq
    sink = jnp.zeros((REP * nq, 1), F32)
    for r in range(REP):
        sink = jnp.where(row_head == r, sink_ref[g * REP + r], sink)
    m = jnp.maximum(jnp.max(s, axis=-1, keepdims=True), sink)
    e = jnp.exp(s - m)
    den = jnp.sum(e, axis=-1, keepdims=True) + jnp.exp(sink - m)
    p = (e / den).astype(BF16)
    o = jnp.dot(p, vb, preferred_element_type=F32)
    og = jnp.zeros((nq, gw), F32)
    for r in range(REP):
        og = og + jnp.where(lane_head == r, o[r * nq:(r + 1) * nq, :], 0.0)
    return og


def _attn_body(sink_ref, q_ref, kp_ref, kc_ref, kn_ref, vp_ref, vc_ref, vn_ref, kx_ref, vx_ref, o_ref):
    n = pl.program_id(1)
    nblk = S // QB
    nk = 3 * QB + LC
    row = lax.broadcasted_iota(I32, (QB, nk), 0)
    col = lax.broadcasted_iota(I32, (QB, nk), 1)
    lo = jnp.where(n == 0, QB, 0)
    hi = jnp.where(n == nblk - 1, 2 * QB, 3 * QB)
    band = (col >= row) & (col <= row + 2 * WINDOW) & (col >= lo) & (col < hi)
    valid = band | (col >= 3 * QB)
    q = q_ref[...]
    gw = REP * HD
    outs = []
    for g in range(NKV):
        sl = slice(g * gw, (g + 1) * gw)
        kb = jnp.concatenate([kp_ref[:, sl], kc_ref[:, sl], kn_ref[:, sl], kx_ref[:, sl]], axis=0)
        vb = jnp.concatenate([vp_ref[:, sl], vc_ref[:, sl], vn_ref[:, sl], vx_ref[:, sl]], axis=0)
        outs.append(_attn_group(q, kb, vb, sink_ref, g, valid, QB))
    o_ref[...] = jnp.concatenate(outs, axis=1).astype(BF16)


def _attn(qr, kr, vr, kc_arr, vc_arr, ctx_blk0, sink):
    nblk = S // QB
    w = NH * HD

    def cur(b, n):
        return (b * nblk + n, 0)

    def prev(b, n):
        return (b * nblk + jnp.maximum(n - 1, 0), 0)

    def nxt(b, n):
        return (b * nblk + jnp.minimum(n + 1, nblk - 1), 0)

    def cx(b, n):
        return (ctx_blk0 + b, 0)

    blk = lambda f: pl.BlockSpec((QB, w), f)
    return pl.pallas_call(
        _attn_body,
        out_shape=jax.ShapeDtypeStruct((TX, w), BF16),
        grid=(NB, nblk),
        in_specs=[
            pl.BlockSpec(memory_space=pltpu.SMEM),
            blk(cur), blk(prev), blk(cur), blk(nxt), blk(prev), blk(cur), blk(nxt),
            pl.BlockSpec((LC, w), cx), pl.BlockSpec((LC, w), cx),
        ],
        out_specs=blk(cur),
        compiler_params=_cparams(("arbitrary", "arbitrary"), 32),
        name="window_attn",
    )(sink, qr, kr, kr, kr, vr, vr, vr, kc_arr, vc_arr)


def _ctx_attn_body(sink_ref, q_ref, kx_ref, vx_ref, o_ref):
    gw = REP * HD
    q = q_ref[...]
    outs = []
    for g in range(NKV):
        sl = slice(g * gw, (g + 1) * gw)
        outs.append(_attn_group(q, kx_ref[:, sl], vx_ref[:, sl], sink_ref, g, None, LC))
    o_ref[...] = jnp.concatenate(outs, axis=1).astype(BF16)


def _ctx_attn(qr, kr, vr, sink):
    w = NH * HD
    blk0 = TX // LC
    spec = pl.BlockSpec((LC, w), lambda b: (blk0 + b, 0))
    return pl.pallas_call(
        _ctx_attn_body,
        out_shape=jax.ShapeDtypeStruct((TC, w), BF16),
        grid=(NB,),
        in_specs=[pl.BlockSpec(memory_space=pltpu.SMEM), spec, spec, spec],
        out_specs=pl.BlockSpec((LC, w), lambda b: (b, 0)),
        compiler_params=_cparams(("arbitrary",), 32),
        name="ctx_attn",
    )(sink, qr, kr, vr)


def _mix_body(su_ref, sv_ref, cb_ref, cc_ref, cx_ref, ga_ref, gb_ref,
              ccp_ref, cxp_ref, gap_ref, gbp_ref, ccn_ref, cxn_ref, gan_ref, gbn_ref,
              lng_ref, lnb_ref, sw_ref, sbias_ref, scw_ref, dww_ref, dwb_ref, clg_ref, clb_ref,
              o_ref, m_scr, z_scr, *, n_x_tiles):
    tm = TM_TOK
    i = pl.program_id(0)
    tps = S // tm
    is_x = i < n_x_tiles
    first = jnp.logical_or(jnp.logical_not(is_x), (i % tps) == 0)
    last = jnp.logical_or(jnp.logical_not(is_x), (i % tps) == tps - 1)
    keep_p = jnp.where(first, 0.0, 1.0)
    keep_n = jnp.where(last, 0.0, 1.0)

    u = _gelu_tanh(su_ref[...].astype(F32))
    v = _layernorm(_gelu_tanh(sv_ref[...].astype(F32)), lng_ref[...], lnb_ref[...]).astype(BF16)
    gwid = BW // SGU_GROUPS
    chunks = []
    for c in range(tm // SGU_CHUNK):
        parts = []
        for g in range(SGU_GROUPS):
            vc = v[c * SGU_CHUNK:(c + 1) * SGU_CHUNK, g * gwid:(g + 1) * gwid]
            parts.append(jnp.dot(sw_ref[g], vc, preferred_element_type=F32))
        chunks.append(jnp.concatenate(parts, axis=1) + sbias_ref[...])
    yb = u * jnp.concatenate(chunks, axis=0)

    m_scr[HALO:HALO + tm, :] = cc_ref[...].astype(F32) * cx_ref[...].astype(F32)
    m_scr[0:HALO, :] = ccp_ref[...].astype(F32) * cxp_ref[...].astype(F32) * keep_p
    m_scr[HALO + tm:, :] = ccn_ref[...].astype(F32) * cxn_ref[...].astype(F32) * keep_n
    z = jnp.zeros((tm, BW), F32)
    for k in range(SCONV_K):
        z = z + scw_ref[k:k + 1, :] * m_scr[pl.ds(HALO - SCONV_K // 2 + k, tm), :]
    yc = cb_ref[...].astype(F32) * z

    z_scr[0, HALO:HALO + tm, :] = ga_ref[...].astype(F32) * gb_ref[...].astype(F32)
    z_scr[0, 0:HALO, :] = gap_ref[...].astype(F32) * gbp_ref[...].astype(F32) * keep_p
    z_scr[0, HALO + tm:, :] = gan_ref[...].astype(F32) * gbn_ref[...].astype(F32) * keep_n
    n_sh = tm + 2 * HALO - SUBLANES
    for s in range(1, SUBLANES):
        z_scr[s, 0:n_sh, :] = z_scr[0, pl.ds(s, n_sh), :]
    acc = jnp.zeros((tm, BW), F32) + dwb_ref[...]
    for k in range(CONF_K):
        off = HALO - CONF_K // 2 + k
        base = off - off % SUBLANES
        acc = acc + dww_ref[k:k + 1, :] * z_scr[off % SUBLANES, base:base + tm, :]
    zn = _layernorm(acc, clg_ref[...], clb_ref[...])
    yd = zn * _sigmoid(zn)

    o_ref[...] = jnp.concatenate([yb, yc, yd], axis=1).astype(BF16)


def _mixers(p, n_rows, n_x_tiles, lp):
    tm = TM_TOK
    (sgu_ln_g, sgu_ln_b, sgu_w, sgu_b, sconv_w, conf_dw_w, conf_dw_b, conf_ln_g, conf_ln_b) = lp

    def sec(off):
        return pl.BlockSpec((pl.Element(tm), pl.Element(BW)), lambda i: (i * tm, off))

    hpt = tm // HALO

    def halo_prev(off):
        return pl.BlockSpec((pl.Element(HALO), pl.Element(BW)),
                            lambda i: (jnp.maximum(i * hpt - 1, 0) * HALO, off))

    def halo_next(off):
        return pl.BlockSpec((pl.Element(HALO), pl.Element(BW)),
                            lambda i: (jnp.minimum(i * hpt + hpt, n_rows // HALO - 1) * HALO, off))

    def const(shape):
        return pl.BlockSpec(shape, lambda i: (0,) * len(shape))

    sbias = jnp.repeat(sgu_b.T, BW // SGU_GROUPS, axis=1)
    row = lambda a: a.reshape(1, BW)
    conv_offs = (OFF_CC, OFF_CX, OFF_GA, OFF_GB)
    return pl.pallas_call(
        functools.partial(_mix_body, n_x_tiles=n_x_tiles),
        out_shape=jax.ShapeDtypeStruct((n_rows, 3 * BW), BF16),
        grid=(n_rows // tm,),
        in_specs=[sec(o) for o in (OFF_SU, OFF_SV, OFF_CB, OFF_CC, OFF_CX, OFF_GA, OFF_GB)]
        + [halo_prev(o) for o in conv_offs] + [halo_next(o) for o in conv_offs]
        + [const((1, BW)), const((1, BW)), const((SGU_GROUPS, SGU_CHUNK, SGU_CHUNK)), const((SGU_CHUNK, BW)),
           const((SCONV_K, BW)), const((CONF_K, BW)), const((1, BW)), const((1, BW)), const((1, BW))],
        out_specs=pl.BlockSpec((tm, 3 * BW), lambda i: (i, 0)),
        scratch_shapes=[pltpu.VMEM((tm + 2 * HALO, BW), F32), pltpu.VMEM((SUBLANES, tm + 2 * HALO, BW), F32)],
        compiler_params=_cparams(("arbitrary",), 32),
        name="mixers",
    )(*([p] * 15), row(sgu_ln_g), row(sgu_ln_b), sgu_w.astype(BF16), sbias, sconv_w, conf_dw_w,
      row(conf_dw_b), row(conf_ln_g), row(conf_ln_b))


def _norm2(xn, n2g, mod):
    return _rms(xn, n2g) * (1.0 + mod[4:5, :]) + mod[3:4, :]


def _post_body(g_ref, ya_ref, yac_ref, yr_ref, wb_ref, wo_ref, x_ref, cx_ref, mod_ref, n2g_ref, rw_ref,
               xo_ref, lg_ref, acc_ref):
    tm = TM_POST
    br = pl.program_id(1)
    y = jnp.where(br == 0, _stream_tile(ya_ref, yac_ref, tm), yr_ref[...])
    term = g_ref[...].astype(F32) * jnp.dot(y, wb_ref[...], preferred_element_type=F32)

    @pl.when(br == 0)
    def _():
        acc_ref[...] = term

    @pl.when(br > 0)
    def _():
        acc_ref[...] += term

    @pl.when(br == pl.num_programs(1) - 1)
    def _():
        nt = (((1,), (1,)), ((), ()))
        rows = POST_ROW_CHUNK
        for c in range(tm // rows):
            sl = slice(c * rows, (c + 1) * rows)
            out = jnp.dot(acc_ref[sl, :].astype(BF16), wo_ref[...], preferred_element_type=F32)
            xn = jnp.where(pl.program_id(0) < TX // tm, x_ref[sl, :], cx_ref[sl, :]) + mod_ref[2:3, :] * out
            xo_ref[sl, :] = xn
            h2 = _norm2(xn, n2g_ref[...], mod_ref).astype(BF16)
            lg_ref[:, sl] = lax.dot_general(rw_ref[...], h2, nt, preferred_element_type=F32)


def _post(p, ya, yac, yr, wb, wo, x, cx, c_blk0, mods, layer, n2g, rw, n_rows):
    tm = TM_POST
    one = pl.Buffered(1)
    cx_spec = _stream_specs(tm, c_blk0)
    cx_spec[1] = pl.BlockSpec((tm, D), cx_spec[1].index_map, pipeline_mode=one)
    return pl.pallas_call(
        _post_body,
        out_shape=[jax.ShapeDtypeStruct((n_rows, D), F32), jax.ShapeDtypeStruct((NE, n_rows), F32)],
        grid=(n_rows // tm, 4),
        in_specs=[pl.BlockSpec((pl.Element(tm), pl.Element(D)),
                               lambda i, b: (i * tm, (OFF_GATES // LANES + b * (D // LANES)) * LANES))]
        + _stream_specs(tm, 0, BW) + [
            pl.BlockSpec((tm, BW), lambda i, b: (i, jnp.maximum(b - 1, 0))),
            pl.BlockSpec((None, None, BW, D), lambda i, b: (layer, b, 0, 0)),
            pl.BlockSpec((None, D, D), lambda i, b: (layer, 0, 0), pipeline_mode=one)]
        + cx_spec + [
            pl.BlockSpec((None, None, 6, D), lambda i, b: (layer, _mod_row(i, tm), 0, 0)),
            pl.BlockSpec((1, D), lambda i, b: (0, 0)),
            pl.BlockSpec((NE, D), lambda i, b: (0, 0)),
        ],
        out_specs=[pl.BlockSpec((tm, D), lambda i, b: (i, 0)), pl.BlockSpec((NE, tm), lambda i, b: (0, i))],
        scratch_shapes=[pltpu.VMEM((tm, D), F32)],
        compiler_params=_cparams(("arbitrary", "arbitrary"), 56),
        name="post",
    )(p, ya, yac, yr, wb, wo, x, cx, mods, n2g.reshape(1, D), rw)


def _route_body(lg_ref, rb_ref, tri_ref, o_ref, cnt_ref, carry_ref):
    tm = TM_TOK
    i = pl.program_id(0)

    @pl.when(i == 0)
    def _():
        carry_ref[...] = jnp.zeros_like(carry_ref)

    lg = lg_ref[...]
    e = jnp.exp(lg - jnp.max(lg, axis=0, keepdims=True))
    sc = e / jnp.sum(e, axis=0, keepdims=True)
    bi = sc + rb_ref[...]
    b = [bi[k:k + 1, :] for k in range(NE)]
    s = [sc[k:k + 1, :] for k in range(NE)]

    gs = []
    for g in range(NG):
        v = b[g * EPG:(g + 1) * EPG]
        best = None
        for a in range(EPG):
            for c in range(a + 1, EPG):
                ps = v[a] + v[c]
                best = ps if best is None else jnp.maximum(best, ps)
        gs.append(best)
    gsel = jnp.zeros((1, tm), I32)
    gbest = gs[0]
    for g in range(1, NG):
        take = gs[g] > gbest
        gsel = jnp.where(take, g, gsel)
        gbest = jnp.where(take, gs[g], gbest)

    vb, vs = [], []
    for j in range(EPG):
        xb, xs = b[j], s[j]
        for g in range(1, NG):
            xb = jnp.where(gsel == g, b[g * EPG + j], xb)
            xs = jnp.where(gsel == g, s[g * EPG + j], xs)
        vb.append(xb)
        vs.append(xs)
    order = []
    for j in range(EPG):
        c = jnp.zeros((1, tm), I32)
        for m in range(EPG):
            if m == j:
                continue
            ahead = (vb[m] >= vb[j]) if m < j else (vb[m] > vb[j])
            c = c + jnp.where(ahead, 1, 0)
        order.append(c)
    zero = jnp.zeros((1, tm), F32)
    w0 = zero
    w1 = zero
    j0 = jnp.zeros((1, tm), I32)
    j1 = jnp.zeros((1, tm), I32)
    for j in range(EPG):
        w0 = jnp.where(order[j] == 0, vs[j], w0)
        w1 = jnp.where(order[j] == 1, vs[j], w1)
        j0 = jnp.where(order[j] == 0, j, j0)
        j1 = jnp.where(order[j] == 1, j, j1)
    tot = w0 + w1
    e0 = gsel * EPG + j0
    e1 = gsel * EPG + j1

    eid = lax.broadcasted_iota(I32, (NE, tm), 0)
    oh0 = eid == e0
    oh1 = eid == e1
    oh = jnp.where(oh0 | oh1, 1.0, 0.0)
    rank = jnp.dot(oh.astype(BF16), tri_ref[...], preferred_element_type=F32) + carry_ref[:, 0:1]
    r0 = jnp.sum(jnp.where(oh0, rank, 0.0), axis=0, keepdims=True)
    r1 = jnp.sum(jnp.where(oh1, rank, 0.0), axis=0, keepdims=True)
    new_carry = carry_ref[...] + jnp.sum(oh, axis=1, keepdims=True)
    carry_ref[...] = new_carry
    cnt_ref[...] = new_carry

    o_ref[0:1, :] = e0.astype(F32)
    o_ref[1:2, :] = e1.astype(F32)
    o_ref[2:3, :] = r0
    o_ref[3:4, :] = r1
    o_ref[4:5, :] = w0 / tot
    o_ref[5:6, :] = w1 / tot
    o_ref[6:8, :] = jnp.zeros((2, tm), F32)


def _route(lg, router_b, n_rows):
    tm = TM_TOK
    tri = jnp.asarray(np.triu(np.ones((tm, tm), np.float32), 1), BF16)
    return pl.pallas_call(
        _route_body,
        out_shape=[jax.ShapeDtypeStruct((8, n_rows), F32), jax.ShapeDtypeStruct((NE, LANES), F32)],
        grid=(n_rows // tm,),
        in_specs=[
            pl.BlockSpec((NE, tm), lambda i: (0, i)),
            pl.BlockSpec((NE, 1), lambda i: (0, 0)),
            pl.BlockSpec((tm, tm), lambda i: (0, 0)),
        ],
        out_specs=[pl.BlockSpec((8, tm), lambda i: (0, i)), pl.BlockSpec((NE, LANES), lambda i: (0, 0))],
        scratch_shapes=[pltpu.VMEM((NE, LANES), F32)],
        compiler_params=_cparams(("arbitrary",), 32),
        name="route",
    )(lg, router_b.reshape(NE, 1), tri)


def _row_copy(src_ref, src_row, dst_ref, dst_row, sem):
    return pltpu.make_async_copy(src_ref.at[pl.ds(src_row, 1)], dst_ref.at[pl.ds(dst_row, 1)], sem)


def _issue_rows(n, start_row):
    def trip(t, c):
        base = pl.multiple_of(t * ROW_UNROLL, ROW_UNROLL)
        for u in range(ROW_UNROLL):
            start_row(base + u)
        return c

    lax.fori_loop(0, n // ROW_UNROLL, trip, 0)


def _dispatch_body(tail_ref, need_ref, pos_ref, x_ref, mod_ref, n2g_ref, xs_ref, h_ref, zero_ref, sem, zsem):
    tm = TM_TOK
    i = pl.program_id(0)
    slot = i % 2
    h_ref[slot] = _norm2(x_ref[...], n2g_ref[...], mod_ref)

    @pl.when(pl.program_id(0) == 0)
    def _():
        zero_ref[...] = jnp.zeros_like(zero_ref)

        def tail_copy(e):
            start = pl.multiple_of(tail_ref[e], TM_E)
            return pltpu.make_async_copy(zero_ref, xs_ref.at[pl.ds(start, TM_E)], zsem)

        for e in range(2 * NE):
            @pl.when(need_ref[e] > 0)
            def _():
                tail_copy(e).start()
        for e in range(2 * NE):
            @pl.when(need_ref[e] > 0)
            def _():
                tail_copy(e).wait()

    def start_row(r):
        _row_copy(h_ref.at[slot], r, xs_ref, pos_ref[0, r], sem.at[slot]).start()
        _row_copy(h_ref.at[slot], r, xs_ref, pos_ref[1, r], sem.at[slot]).start()

    _issue_rows(tm, start_row)

    def drain(s):
        for _ in range(2):
            pltpu.make_async_copy(h_ref.at[s], xs_ref.at[pl.ds(0, tm)], sem.at[s]).wait()

    @pl.when(i > 0)
    def _():
        drain(1 - slot)

    @pl.when(i == pl.num_programs(0) - 1)
    def _():
        drain(slot)


def _dispatch(tail, need, pos, xn, mods, layer, n2g, n_rows):
    tm = TM_TOK
    return pl.pallas_call(
        _dispatch_body,
        out_shape=jax.ShapeDtypeStruct((_n_expert_tiles(n_rows) * TM_E, D), F32),
        grid_spec=pltpu.PrefetchScalarGridSpec(
            num_scalar_prefetch=2,
            grid=(n_rows // tm,),
            in_specs=[
                pl.BlockSpec((None, 2, tm), lambda i, t, n: (i, 0, 0), memory_space=pltpu.SMEM),
                pl.BlockSpec((tm, D), lambda i, t, n: (i, 0)),
                pl.BlockSpec((None, None, 6, D), lambda i, t, n: (layer, _mod_row(i, tm), 0, 0)),
                pl.BlockSpec((1, D), lambda i, t, n: (0, 0)),
            ],
            out_specs=pl.BlockSpec(memory_space=pl.ANY),
            scratch_shapes=[pltpu.VMEM((2, tm, D), F32), pltpu.VMEM((TM_E, D), F32),
                            pltpu.SemaphoreType.DMA((2,)), pltpu.SemaphoreType.DMA(())],
        ),
        compiler_params=_cparams(("arbitrary",), 32),
        name="dispatch",
    )(tail, need, pos, xn, mods, n2g.reshape(1, D))


def _moe_body(te_ref, nu_ref, first_ref, nxt_ref, rem_ref, xs_ref, wu_hbm, wd_hbm, ys_ref,
              wub, wdb, stage, sem, st, *, layer):
    r = pl.program_id(0)

    def chunk_copy(e, k):
        if k < MOE_UP_CHUNKS:
            src = wu_hbm.at[layer, e, pl.ds(k * MOE_W_CHUNK, MOE_W_CHUNK), :]
        else:
            src = wd_hbm.at[layer, e, pl.ds((k - MOE_UP_CHUNKS) * MOE_W_CHUNK, MOE_W_CHUNK), :]
        return pltpu.make_async_copy(src, stage.at[k % 2], sem.at[k % 2])

    def convert(e, k, slot):
        chunk_copy(e, k).wait()
        v = stage[k % 2].astype(BF16)
        if k < MOE_UP_CHUNKS:
            wub[slot, k * MOE_W_CHUNK:(k + 1) * MOE_W_CHUNK, :] = v
        else:
            kd = k - MOE_UP_CHUNKS
            wdb[slot, kd * MOE_W_CHUNK:(kd + 1) * MOE_W_CHUNK, :] = v
        if k + 2 < MOE_CHUNKS:
            chunk_copy(e, k + 2).start()

    def begin(e):
        chunk_copy(e, 0).start()
        chunk_copy(e, 1).start()
        st[1] = 0

    @pl.when(r < nu_ref[0])
    def _():
        e_cur = te_ref[r]
        e_nxt = nxt_ref[r]

        @pl.when(r == 0)
        def _():
            st[0] = 1
            begin(e_cur)

        @pl.when(first_ref[r] == 1)
        def _():
            slot = 1 - st[0]
            done = st[1]
            for k in range(MOE_CHUNKS):
                @pl.when(k >= done)
                def _():
                    convert(e_cur, k, slot)
            st[0] = slot
            st[1] = MOE_CHUNKS

            @pl.when(e_nxt >= 0)
            def _():
                begin(e_nxt)

        cur = st[0]
        x = xs_ref[...].astype(BF16)
        hc = jnp.dot(x, wub[cur], preferred_element_type=F32)
        a = hc[:, :DFF]
        b = hc[:, DFF:]
        act = (a * _sigmoid(a) * b).astype(BF16)
        ys_ref[...] = jnp.dot(act, wdb[cur], preferred_element_type=F32)

        @pl.when(e_nxt >= 0)
        def _():
            done = st[1]
            share = (MOE_CHUNKS - done + rem_ref[r] - 1) // rem_ref[r]
            for k in range(MOE_CHUNKS):
                @pl.when(jnp.logical_and(k >= done, k < done + share))
                def _():
                    convert(e_nxt, k, 1 - cur)
            st[1] = done + share

    @pl.when(r >= nu_ref[0])
    def _():
        ys_ref[...] = jnp.zeros_like(ys_ref)


def _moe(plan, xs, wu, wd, layer):
    def row(r, te, nu, *_):
        return (jnp.minimum(r, nu[0] - 1), 0)

    return pl.pallas_call(
        functools.partial(_moe_body, layer=layer),
        out_shape=jax.ShapeDtypeStruct(xs.shape, F32),
        grid_spec=pltpu.PrefetchScalarGridSpec(
            num_scalar_prefetch=5,
            grid=(xs.shape[0] // TM_E,),
            in_specs=[
                pl.BlockSpec((TM_E, D), row),
                pl.BlockSpec(memory_space=pl.ANY),
                pl.BlockSpec(memory_space=pl.ANY),
            ],
            out_specs=pl.BlockSpec((TM_E, D), lambda r, *_: (r, 0)),
            scratch_shapes=[
                pltpu.VMEM((2, D, 2 * DFF), BF16),
                pltpu.VMEM((2, DFF, D), BF16),
                pltpu.VMEM((2, MOE_W_CHUNK, D), F32),
                pltpu.SemaphoreType.DMA((2,)),
                pltpu.SMEM((2,), I32),
            ],
        ),
        compiler_params=_cparams(("arbitrary",), 56),
        name="moe",
    )(*plan, xs, wu, wd)


def _combine_body(pos_ref, posn_ref, ys_ref, x_ref, w_ref, mod_ref, g_ref, nmod_ref, *rest, final):
    tm = TM_TOK
    outs, (ybuf, sem) = rest[:-2], rest[-2:]
    i = pl.program_id(0)
    slot = i % 2

    def fetch(p_ref, s):
        def start_row(r):
            _row_copy(ys_ref, p_ref[0, r], ybuf.at[s, 0], r, sem.at[s]).start()
            _row_copy(ys_ref, p_ref[1, r], ybuf.at[s, 1], r, sem.at[s]).start()

        _issue_rows(tm, start_row)

    @pl.when(i == 0)
    def _():
        fetch(pos_ref, 0)

    @pl.when(i + 1 < pl.num_programs(0))
    def _():
        fetch(posn_ref, 1 - slot)

    for k in range(2):
        pltpu.make_async_copy(ys_ref.at[pl.ds(0, tm)], ybuf.at[slot, k], sem.at[slot]).wait()

    reps = D // LANES
    w0 = jnp.tile(w_ref[:, 0:LANES], (1, reps))
    w1 = jnp.tile(w_ref[:, LANES:2 * LANES], (1, reps))
    xn = x_ref[...] + mod_ref[5:6, :] * (w0 * ybuf[slot, 0] + w1 * ybuf[slot, 1])
    if final:
        outs[0][...] = _rms(xn, g_ref[...])
    else:
        outs[0][...] = xn
        outs[1][...] = (_rms(xn, g_ref[...]) * (1.0 + nmod_ref[1:2, :]) + nmod_ref[0:1, :]).astype(BF16)


def _combine(pos, ys, x, wlanes, mods, layer, n_rows, g_next, final):
    tm = TM_TOK
    nt = n_rows // tm
    next_layer = min(layer + 1, DEPTH - 1)
    tile = pl.BlockSpec((tm, D), lambda i: (i, 0))
    out_shape = [jax.ShapeDtypeStruct((n_rows, D), F32)]
    if not final:
        out_shape.append(jax.ShapeDtypeStruct((n_rows, D), BF16))
    return pl.pallas_call(
        functools.partial(_combine_body, final=final),
        out_shape=out_shape,
        grid=(nt,),
        in_specs=[
            pl.BlockSpec((None, 2, tm), lambda i: (i, 0, 0), memory_space=pltpu.SMEM),
            pl.BlockSpec((None, 2, tm), lambda i: (jnp.minimum(i + 1, nt - 1), 0, 0), memory_space=pltpu.SMEM),
            pl.BlockSpec(memory_space=pl.ANY),
            tile,
            pl.BlockSpec((tm, 2 * LANES), lambda i: (i, 0)),
            pl.BlockSpec((None, None, 6, D), lambda i: (layer, _mod_row(i, tm), 0, 0)),
            pl.BlockSpec((1, D), lambda i: (0, 0)),
            pl.BlockSpec((None, None, 6, D), lambda i: (next_layer, _mod_row(i, tm), 0, 0)),
        ],
        out_specs=[tile] * len(out_shape),
        scratch_shapes=[pltpu.VMEM((2, 2, tm, D), F32), pltpu.SemaphoreType.DMA((2,))],
        compiler_params=_cparams(("arbitrary",), 40),
        name="combine",
    )(pos, pos, ys, x, wlanes, mods, g_next.reshape(1, D), mods)


def _route_plan(route, cnt, n_rows):
    counts = cnt[:, 0].astype(I32)
    padded = ((counts + TM_E - 1) // TM_E) * TM_E
    ends = jnp.cumsum(padded)
    offs = ends - padded
    e01 = route[0:2].astype(I32)
    eids = jnp.arange(NE, dtype=I32)[:, None, None]
    off01 = jnp.sum(jnp.where(e01[None] == eids, offs[:, None, None], 0), axis=0)
    pos = off01 + route[2:4].astype(I32)
    pos = pos.reshape(2, n_rows // TM_TOK, TM_TOK).transpose(1, 0, 2)
    nt = _n_expert_tiles(n_rows)
    tile_start = jnp.arange(nt, dtype=I32) * TM_E
    tile_expert = jnp.minimum(jnp.sum((tile_start[:, None] >= ends[None, :]).astype(I32), axis=1), NE - 1)
    n_used = (ends[-1] // TM_E).reshape(1).astype(I32)
    spare = n_used[0] + jnp.arange(NE, dtype=I32)
    zstart = jnp.concatenate([offs + (counts // TM_E) * TM_E, jnp.minimum(spare, nt - 1) * TM_E])
    zneed = jnp.concatenate([counts % TM_E != 0, spare < nt]).astype(I32)
    wl = jnp.concatenate([jnp.broadcast_to(route[4][:, None], (n_rows, LANES)),
                          jnp.broadcast_to(route[5][:, None], (n_rows, LANES))], axis=1)
    ntile = padded // TM_E
    ecol = jnp.arange(NE, dtype=I32)
    onehot = tile_expert[:, None] == ecol[None, :]
    pick = lambda v: jnp.sum(jnp.where(onehot, v[None, :], 0), axis=1)
    j_in = jnp.arange(nt, dtype=I32) - pick(offs // TM_E)
    first = (j_in == 0).astype(I32)
    rem = jnp.maximum(pick(ntile) - j_in, 1)
    later = (ecol[None, :] > ecol[:, None]) & (ntile[None, :] > 0)
    nxt_e = jnp.min(jnp.where(later, ecol[None, :], NE), axis=1)
    nxt = pick(jnp.where(nxt_e < NE, nxt_e, -1))
    moe_plan = (tile_expert.astype(I32), n_used, first, nxt.astype(I32), rem.astype(I32))
    return pos, moe_plan, zstart.astype(I32), zneed, wl


def kernel(x, c, ctx, c_ctx, ada_w, ada_b, norm1_g, norm2_g, w_in, attn_sink, sgu_ln_g, sgu_ln_b, sgu_w, sgu_b,
           sconv_w, conf_dw_w, conf_dw_b, conf_ln_g, conf_ln_b, w_branch, w_out, router_w, router_b,
           exp_w_up, exp_w_down, final_g):
    cvec = jnp.concatenate([c, c_ctx[None, :], jnp.zeros((8 - NB - 1, D), F32)], axis=0)
    mods = _ada(cvec, ada_w, ada_b).reshape(DEPTH, 8, 6, D)
    xa, cxa = x.reshape(TX, D), ctx.reshape(TC, D)
    cos, sin = _rope_tables()
    consts = _rope_constants()
    rw = router_w.T.astype(BF16)
    wb_all = w_branch.astype(BF16)
    wo_all = w_out.astype(BF16)

    for l in range(DEPTH):
        last = l == DEPTH - 1
        n_rows = TX if last else T
        n_x_tiles = TX // TM_TOK
        lp = (sgu_ln_g[l], sgu_ln_b[l], sgu_w[l], sgu_b[l], sconv_w[l], conf_dw_w[l], conf_dw_b[l],
              conf_ln_g[l], conf_ln_b[l])
        c_rows0 = 0 if l == 0 else TX

        if l == 0:
            h = _norm_mod(xa, cxa, norm1_g[l], mods, l)
        if not last:
            p = _inproj(h, w_in, l, T, N_IN // TN_IN)
            qr, kr, vr = _prep(p, T, n_x_tiles, cos, sin, consts)
            ya = _attn(qr, kr, vr, kr, vr, TX // LC, attn_sink[l])
            yac = _ctx_attn(qr, kr, vr, attn_sink[l])
        else:
            p = _inproj(h, w_in, l, TX, N_IN // TN_IN)
            pc = _inproj(h[TX:], w_in, l, TC, 1)
            qr, kr, vr = _prep(p, TX, n_x_tiles, cos, sin, consts)
            _, kc, vc = _prep(pc, TC, 0, cos, sin, consts)
            ya = _attn(qr, kr, vr, kc, vc, 0, attn_sink[l])
            yac = ya
        yr = _mixers(p, n_rows, n_x_tiles, lp)
        xn, lg = _post(p, ya, yac, yr, wb_all, wo_all, xa, cxa, c_rows0 // TM_POST, mods, l,
                       norm2_g[l], rw, n_rows)
        route, cnt = _route(lg, router_b, n_rows)
        pos, moe_plan, zstart, zneed, wl = _route_plan(route, cnt, n_rows)
        xs = _dispatch(zstart, zneed, pos, xn, mods, l, norm2_g[l], n_rows)
        ys = _moe(moe_plan, xs, exp_w_up, exp_w_down, l)
        if last:
            (out,) = _combine(pos, ys, xn, wl, mods, l, n_rows, final_g, True)
        else:
            xa, h = _combine(pos, ys, xn, wl, mods, l, n_rows, norm1_g[l + 1], False)
            cxa = xa

    return out.reshape(NB, S, D)
```

```python
import functools

import numpy as np
import jax
import jax.numpy as jnp
from jax import lax
from jax.experimental import pallas as pl
from jax.experimental.pallas import tpu as pltpu

F32 = jnp.float32
BF16 = jnp.bfloat16
I32 = jnp.int32

D = 2048
NB = 4
S = 2048
LC = 256
DEPTH = 2
GRID_W = 64
BW = 512
HD = 64
NH = 8
NKV = 2
REP = NH // NKV
WINDOW = 128
QB = 128
ROPE_THETA = 10000.0
SGU_CHUNK = 128
SGU_GROUPS = 4
SCONV_K = 3
CONF_K = 31
NE = 16
NG = 4
EPG = NE // NG
DFF = D // 2
N_IN = BW + 2 * NKV * HD + 7 * BW + 4 * D
TX = NB * S
TC = NB * LC
T = TX + TC

OFF_Q, OFF_KV = 0, 512
OFF_SU, OFF_SV, OFF_CB, OFF_CC, OFF_CX, OFF_GA, OFF_GB, OFF_GATES = 768, 1280, 1792, 2304, 2816, 3328, 3840, 4352

LANES = 128
SUBLANES = 8
V7X_VMEM_BYTES = 64 * 1024 * 1024
MIB = 1024 * 1024

TM_NORM = 512
TM_IN = 1024
INPROJ_ROW_CHUNK = 512
IN_W_CHUNK = 256
IN_CHUNKS = D // IN_W_CHUNK
TN_IN = 1792
TM_TOK = 256
TM_POST = 512
POST_ROW_CHUNK = 256
TM_E = 256
HALO = 16
ROW_UNROLL = 8
MOE_W_CHUNK = 512
MOE_UP_CHUNKS = D // MOE_W_CHUNK
MOE_CHUNKS = MOE_UP_CHUNKS + DFF // MOE_W_CHUNK


def _n_expert_tiles(n_rows):
    return (2 * n_rows) // TM_E + NE


def _cparams(sem, vmem_mib):
    return pltpu.CompilerParams(dimension_semantics=sem, vmem_limit_bytes=vmem_mib * MIB)


def _sigmoid(x):
    return 0.5 * jnp.tanh(0.5 * x) + 0.5


def _gelu_tanh(x):
    c = np.float32(np.sqrt(2.0 / np.pi))
    return 0.5 * x * (1.0 + jnp.tanh(c * (x + np.float32(0.044715) * (x * x * x))))


def _layernorm(x, g, b, eps=1e-5):
    mu = jnp.mean(x, axis=-1, keepdims=True)
    xc = x - mu
    var = jnp.mean(xc * xc, axis=-1, keepdims=True)
    return xc * lax.rsqrt(var + eps) * g + b


def _mod_row(i, tm):
    return jnp.where(i < TX // tm, (i * tm) // S, NB)


def _ada_body(c_ref, w_ref, b_ref, o_ref):
    c = c_ref[...]
    s = (c * _sigmoid(c)).astype(BF16)
    o_ref[...] = jnp.dot(s, w_ref[...].astype(BF16), preferred_element_type=F32) + b_ref[...]


def _ada(cvec, ada_w, ada_b):
    tn = 1024
    return pl.pallas_call(
        _ada_body,
        out_shape=jax.ShapeDtypeStruct((DEPTH, 8, 6 * D), F32),
        grid=(DEPTH, 6 * D // tn),
        in_specs=[
            pl.BlockSpec((8, D), lambda l, j: (0, 0)),
            pl.BlockSpec((None, D, tn), lambda l, j: (l, 0, j)),
            pl.BlockSpec((None, 1, tn), lambda l, j: (l, 0, j)),
        ],
        out_specs=pl.BlockSpec((None, 8, tn), lambda l, j: (l, 0, j)),
        compiler_params=_cparams(("arbitrary", "arbitrary"), 40),
        name="ada",
    )(cvec, ada_w, ada_b.reshape(DEPTH, 1, 6 * D))


def _stream_specs(tm, c_blk0, width=D):
    nx = TX // tm
    return [pl.BlockSpec((tm, width), lambda i, *_: (jnp.minimum(i, nx - 1), 0)),
            pl.BlockSpec((tm, width), lambda i, *_: (c_blk0 + jnp.maximum(i - nx, 0), 0))]


def _stream_tile(x_ref, c_ref, tm):
    return jnp.where(pl.program_id(0) < TX // tm, x_ref[...], c_ref[...])


def _rms(x, g):
    return x * lax.rsqrt(jnp.mean(x * x, axis=-1, keepdims=True) + 1e-6) * g


def _norm_mod_body(x_ref, c_ref, g_ref, mod_ref, o_ref):
    y = _rms(_stream_tile(x_ref, c_ref, TM_NORM), g_ref[...])
    o_ref[...] = (y * (1.0 + mod_ref[1:2, :]) + mod_ref[0:1, :]).astype(o_ref.dtype)


def _norm_mod(x, cx, g, mods, layer):
    tm = TM_NORM
    return pl.pallas_call(
        _norm_mod_body,
        out_shape=jax.ShapeDtypeStruct((T, D), BF16),
        grid=(T // tm,),
        in_specs=_stream_specs(tm, 0) + [
            pl.BlockSpec((1, D), lambda i: (0, 0)),
            pl.BlockSpec((None, None, 6, D), lambda i: (layer, _mod_row(i, tm), 0, 0)),
        ],
        out_specs=pl.BlockSpec((tm, D), lambda i: (i, 0)),
        compiler_params=_cparams(("arbitrary",), 32),
        name="norm_mod",
    )(x, cx, g.reshape(1, D), mods)


def _inproj_body(h_ref, w_hbm, o_ref, wbf, stage, sem, st, *, layer):
    j = pl.program_id(0)
    i = pl.program_id(1)
    nj = pl.num_programs(0)
    ni = pl.num_programs(1)

    def chunk_copy(jj, k):
        cols = pl.ds(pl.multiple_of(jj * TN_IN, LANES), TN_IN)
        src = w_hbm.at[layer, pl.ds(k * IN_W_CHUNK, IN_W_CHUNK), cols]
        return pltpu.make_async_copy(src, stage.at[k % 2], sem.at[k % 2])

    def convert(jj, k, slot):
        chunk_copy(jj, k).wait()
        wbf[slot, k * IN_W_CHUNK:(k + 1) * IN_W_CHUNK, :] = stage[k % 2].astype(BF16)
        if k + 2 < IN_CHUNKS:
            chunk_copy(jj, k + 2).start()

    def begin(jj):
        chunk_copy(jj, 0).start()
        chunk_copy(jj, 1).start()
        st[1] = 0

    @pl.when(jnp.logical_and(i == 0, j == 0))
    def _():
        st[0] = 1
        begin(0)

    @pl.when(i == 0)
    def _():
        slot = 1 - st[0]
        done = st[1]
        for k in range(IN_CHUNKS):
            @pl.when(k >= done)
            def _():
                convert(j, k, slot)
        st[0] = slot
        st[1] = IN_CHUNKS

        @pl.when(j + 1 < nj)
        def _():
            begin(j + 1)

    cur = st[0]
    col0 = j * TN_IN
    rows = INPROJ_ROW_CHUNK

    def run(epilogue):
        for c in range(h_ref.shape[0] // rows):
            sl = slice(c * rows, (c + 1) * rows)
            acc = jnp.dot(h_ref[sl, :], wbf[cur], preferred_element_type=F32)
            o_ref[sl, :] = epilogue(acc).astype(BF16)

    @pl.when(col0 + TN_IN <= OFF_GB)
    def _():
        run(lambda acc: acc)

    @pl.when(col0 >= OFF_GB)
    def _():
        run(_sigmoid)

    @pl.when(jnp.logical_and(col0 < OFF_GB, col0 + TN_IN > OFF_GB))
    def _():
        col = col0 + lax.broadcasted_iota(I32, (rows, TN_IN), 1)
        run(lambda acc: jnp.where(col >= OFF_GB, _sigmoid(acc), acc))

    @pl.when(j + 1 < nj)
    def _():
        done = st[1]
        left = ni - i
        share = (IN_CHUNKS - done + left - 1) // left
        for k in range(IN_CHUNKS):
            @pl.when(jnp.logical_and(k >= done, k < done + share))
            def _():
                convert(j + 1, k, 1 - cur)
        st[1] = done + share


def _inproj(h, w_in, layer, n_rows, n_col_tiles):
    tm = TM_IN
    return pl.pallas_call(
        functools.partial(_inproj_body, layer=layer),
        out_shape=jax.ShapeDtypeStruct((n_rows, n_col_tiles * TN_IN), BF16),
        grid=(n_col_tiles, n_rows // tm),
        in_specs=[
            pl.BlockSpec((tm, D), lambda j, i: (i, 0)),
            pl.BlockSpec(memory_space=pl.ANY),
        ],
        out_specs=pl.BlockSpec((tm, TN_IN), lambda j, i: (i, j)),
        scratch_shapes=[
            pltpu.VMEM((2, D, TN_IN), BF16),
            pltpu.VMEM((2, IN_W_CHUNK, TN_IN), F32),
            pltpu.SemaphoreType.DMA((2,)),
            pltpu.SMEM((2,), I32),
        ],
        compiler_params=_cparams(("arbitrary", "arbitrary"), 56),
        name="inproj",
    )(h, w_in)


def _rope_constants():
    rh = np.zeros((HD, HD), np.float32)
    for base in (0, 32):
        for d in range(16):
            rh[base + 16 + d, base + d] = -1.0
            rh[base + d, base + 16 + d] = 1.0
    rq = np.kron(np.eye(NH, dtype=np.float32), rh)
    rk = np.kron(np.eye(NKV, dtype=np.float32), rh)
    rep = np.zeros((NKV * HD, NH * HD), np.float32)
    for g in range(NKV):
        for r in range(REP):
            for d in range(HD):
                rep[g * HD + d, g * REP * HD + r * HD + d] = 1.0
    return jnp.asarray(rq, BF16), jnp.asarray(rep, BF16), jnp.asarray(rk @ rep, BF16)


def _rope_tables():
    half = HD // 2
    inv = 1.0 / (ROPE_THETA ** (jnp.arange(0, half, 2, dtype=F32) / half))
    pos = jnp.arange(S)
    ar = (pos // GRID_W).astype(F32)[:, None] * inv
    ac = (pos % GRID_W).astype(F32)[:, None] * inv
    cos = jnp.tile(jnp.concatenate([jnp.cos(ar), jnp.cos(ar), jnp.cos(ac), jnp.cos(ac)], axis=1), (1, NH))
    sin = jnp.tile(jnp.concatenate([jnp.sin(ar), jnp.sin(ar), jnp.sin(ac), jnp.sin(ac)], axis=1), (1, NH))
    cos = jnp.concatenate([cos, jnp.ones((TM_TOK, NH * HD), F32)], axis=0)
    sin = jnp.concatenate([sin, jnp.zeros((TM_TOK, NH * HD), F32)], axis=0)
    return cos, sin


def _prep_body(q_ref, kv_ref, cos_ref, sin_ref, rq_ref, rep_ref, rrep_ref, qo_ref, ko_ref, vo_ref):
    cos = cos_ref[...]
    sin = sin_ref[...]
    q = q_ref[...]
    qs = jnp.dot(q, rq_ref[...], preferred_element_type=F32)
    qo_ref[...] = ((q.astype(F32) * cos + qs * sin) * (HD ** -0.5)).astype(BF16)
    k = kv_ref[:, 0:NKV * HD]
    v = kv_ref[:, NKV * HD:2 * NKV * HD]
    kr = jnp.dot(k, rep_ref[...], preferred_element_type=F32)
    ks = jnp.dot(k, rrep_ref[...], preferred_element_type=F32)
    ko_ref[...] = (kr * cos + ks * sin).astype(BF16)
    vo_ref[...] = jnp.dot(v, rep_ref[...], preferred_element_type=F32).astype(BF16)


def _prep(p, n_rows, n_x_tiles, cos, sin, consts):
    tm = TM_TOK
    rq, rep, rrep = consts
    tps = S // tm

    def tab(i):
        return (jnp.where(i < n_x_tiles, i % tps, tps), 0)

    w = NH * HD
    return pl.pallas_call(
        _prep_body,
        out_shape=[jax.ShapeDtypeStruct((n_rows, w), BF16)] * 3,
        grid=(n_rows // tm,),
        in_specs=[
            pl.BlockSpec((tm, w), lambda i: (i, 0)),
            pl.BlockSpec((tm, 2 * NKV * HD), lambda i: (i, OFF_KV // (2 * NKV * HD))),
            pl.BlockSpec((tm, w), tab),
            pl.BlockSpec((tm, w), tab),
            pl.BlockSpec((w, w), lambda i: (0, 0)),
            pl.BlockSpec((NKV * HD, w), lambda i: (0, 0)),
            pl.BlockSpec((NKV * HD, w), lambda i: (0, 0)),
        ],
        out_specs=[pl.BlockSpec((tm, w), lambda i: (i, 0))] * 3,
        compiler_params=_cparams(("arbitrary",), 32),
        name="prep",
    )(p, p, cos, sin, rq, rep, rrep)


def _attn_group(q, kb, vb, sink_ref, g, valid, nq):
    gw = REP * HD
    lane_head = lax.broadcasted_iota(I32, (nq, gw), 1) // HD
    qg = q[:, g * gw:(g + 1) * gw]
    qs = jnp.concatenate([jnp.where(lane_head == r, qg, jnp.zeros_like(qg)) for r in range(REP)], axis=0)
    s = lax.dot_general(qs, kb, (((1,), (1,)), ((), ())), preferred_element_type=F32)
    if valid is not None:
        s = jnp.where(jnp.concatenate([valid] * REP, axis=0), s, -jnp.inf)
    row_head = lax.broadcasted_iota(I32, (REP * nq, 1), 0) // nq
    sink = jnp.zeros((REP * nq, 1), F32)
    for r in range(REP):
        sink = jnp.where(row_head == r, sink_ref[g * REP + r], sink)
    m = jnp.maximum(jnp.max(s, axis=-1, keepdims=True), sink)
    e = jnp.exp(s - m)
    den = jnp.sum(e, axis=-1, keepdims=True) + jnp.exp(sink - m)
    p = (e / den).astype(BF16)
    o = jnp.dot(p, vb, preferred_element_type=F32)
    og = jnp.zeros((nq, gw), F32)
    for r in range(REP):
        og = og + jnp.where(lane_head == r, o[r * nq:(r + 1) * nq, :], 0.0)
    return og


def _attn_body(sink_ref, q_ref, kp_ref, kc_ref, kn_ref, vp_ref, vc_ref, vn_ref, kx_ref, vx_ref, o_ref):
    n = pl.program_id(1)
    nblk = S // QB
    nk = 3 * QB + LC
    row = lax.broadcasted_iota(I32, (QB, nk), 0)
    col = lax.broadcasted_iota(I32, (QB, nk), 1)
    lo = jnp.where(n == 0, QB, 0)
    hi = jnp.where(n == nblk - 1, 2 * QB, 3 * QB)
    band = (col >= row) & (col <= row + 2 * WINDOW) & (col >= lo) & (col < hi)
    valid = band | (col >= 3 * QB)
    q = q_ref[...]
    gw = REP * HD
    outs = []
    for g in range(NKV):
        sl = slice(g * gw, (g + 1) * gw)
        kb = jnp.concatenate([kp_ref[:, sl], kc_ref[:, sl], kn_ref[:, sl], kx_ref[:, sl]], axis=0)
        vb = jnp.concatenate([vp_ref[:, sl], vc_ref[:, sl], vn_ref[:, sl], vx_ref[:, sl]], axis=0)
        outs.append(_attn_group(q, kb, vb, sink_ref, g, valid, QB))
    o_ref[...] = jnp.concatenate(outs, axis=1).astype(BF16)


def _attn(qr, kr, vr, kc_arr, vc_arr, ctx_blk0, sink):
    nblk = S // QB
    w = NH * HD

    def cur(b, n):
        return (b * nblk + n, 0)

    def prev(b, n):
        return (b * nblk + jnp.maximum(n - 1, 0), 0)

    def nxt(b, n):
        return (b * nblk + jnp.minimum(n + 1, nblk - 1), 0)

    def cx(b, n):
        return (ctx_blk0 + b, 0)

    blk = lambda f: pl.BlockSpec((QB, w), f)
    return pl.pallas_call(
        _attn_body,
        out_shape=jax.ShapeDtypeStruct((TX, w), BF16),
        grid=(NB, nblk),
        in_specs=[
            pl.BlockSpec(memory_space=pltpu.SMEM),
            blk(cur), blk(prev), blk(cur), blk(nxt), blk(prev), blk(cur), blk(nxt),
            pl.BlockSpec((LC, w), cx), pl.BlockSpec((LC, w), cx),
        ],
        out_specs=blk(cur),
        compiler_params=_cparams(("arbitrary", "arbitrary"), 32),
        name="window_attn",
    )(sink, qr, kr, kr, kr, vr, vr, vr, kc_arr, vc_arr)


def _ctx_attn_body(sink_ref, q_ref, kx_ref, vx_ref, o_ref):
    gw = REP * HD
    q = q_ref[...]
    outs = []
    for g in range(NKV):
        sl = slice(g * gw, (g + 1) * gw)
        outs.append(_attn_group(q, kx_ref[:, sl], vx_ref[:, sl], sink_ref, g, None, LC))
    o_ref[...] = jnp.concatenate(outs, axis=1).astype(BF16)


def _ctx_attn(qr, kr, vr, sink):
    w = NH * HD
    blk0 = TX // LC
    spec = pl.BlockSpec((LC, w), lambda b: (blk0 + b, 0))
    return pl.pallas_call(
        _ctx_attn_body,
        out_shape=jax.ShapeDtypeStruct((TC, w), BF16),
        grid=(NB,),
        in_specs=[pl.BlockSpec(memory_space=pltpu.SMEM), spec, spec, spec],
        out_specs=pl.BlockSpec((LC, w), lambda b: (b, 0)),
        compiler_params=_cparams(("arbitrary",), 32),
        name="ctx_attn",
    )(sink, qr, kr, vr)


def _mix_body(su_ref, sv_ref, cb_ref, cc_ref, cx_ref, ga_ref, gb_ref,
              ccp_ref, cxp_ref, gap_ref, gbp_ref, ccn_ref, cxn_ref, gan_ref, gbn_ref,
              lng_ref, lnb_ref, sw_ref, sbias_ref, scw_ref, dww_ref, dwb_ref, clg_ref, clb_ref,
              o_ref, m_scr, z_scr, *, n_x_tiles):
    tm = TM_TOK
    i = pl.program_id(0)
    tps = S // tm
    is_x = i < n_x_tiles
    first = jnp.logical_or(jnp.logical_not(is_x), (i % tps) == 0)
    last = jnp.logical_or(jnp.logical_not(is_x), (i % tps) == tps - 1)
    keep_p = jnp.where(first, 0.0, 1.0)
    keep_n = jnp.where(last, 0.0, 1.0)

    u = _gelu_tanh(su_ref[...].astype(F32))
    v = _layernorm(_gelu_tanh(sv_ref[...].astype(F32)), lng_ref[...], lnb_ref[...]).astype(BF16)
    gwid = BW // SGU_GROUPS
    chunks = []
    for c in range(tm // SGU_CHUNK):
        parts = []
        for g in range(SGU_GROUPS):
            vc = v[c * SGU_CHUNK:(c + 1) * SGU_CHUNK, g * gwid:(g + 1) * gwid]
            parts.append(jnp.dot(sw_ref[g], vc, preferred_element_type=F32))
        chunks.append(jnp.concatenate(parts, axis=1) + sbias_ref[...])
    yb = u * jnp.concatenate(chunks, axis=0)

    m_scr[HALO:HALO + tm, :] = cc_ref[...].astype(F32) * cx_ref[...].astype(F32)
    m_scr[0:HALO, :] = ccp_ref[...].astype(F32) * cxp_ref[...].astype(F32) * keep_p
    m_scr[HALO + tm:, :] = ccn_ref[...].astype(F32) * cxn_ref[...].astype(F32) * keep_n
    z = jnp.zeros((tm, BW), F32)
    for k in range(SCONV_K):
        z = z + scw_ref[k:k + 1, :] * m_scr[pl.ds(HALO - SCONV_K // 2 + k, tm), :]
    yc = cb_ref[...].astype(F32) * z

    z_scr[0, HALO:HALO + tm, :] = ga_ref[...].astype(F32) * gb_ref[...].astype(F32)
    z_scr[0, 0:HALO, :] = gap_ref[...].astype(F32) * gbp_ref[...].astype(F32) * keep_p
    z_scr[0, HALO + tm:, :] = gan_ref[...].astype(F32) * gbn_ref[...].astype(F32) * keep_n
    n_sh = tm + 2 * HALO - SUBLANES
    for s in range(1, SUBLANES):
        z_scr[s, 0:n_sh, :] = z_scr[0, pl.ds(s, n_sh), :]
    acc = jnp.zeros((tm, BW), F32) + dwb_ref[...]
    for k in range(CONF_K):
        off = HALO - CONF_K // 2 + k
        base = off - off % SUBLANES
        acc = acc + dww_ref[k:k + 1, :] * z_scr[off % SUBLANES, base:base + tm, :]
    zn = _layernorm(acc, clg_ref[...], clb_ref[...])
    yd = zn * _sigmoid(zn)

    o_ref[...] = jnp.concatenate([yb, yc, yd], axis=1).astype(BF16)


def _mixers(p, n_rows, n_x_tiles, lp):
    tm = TM_TOK
    (sgu_ln_g, sgu_ln_b, sgu_w, sgu_b, sconv_w, conf_dw_w, conf_dw_b, conf_ln_g, conf_ln_b) = lp

    def sec(off):
        return pl.BlockSpec((pl.Element(tm), pl.Element(BW)), lambda i: (i * tm, off))

    hpt = tm // HALO

    def halo_prev(off):
        return pl.BlockSpec((pl.Element(HALO), pl.Element(BW)),
                            lambda i: (jnp.maximum(i * hpt - 1, 0) * HALO, off))

    def halo_next(off):
        return pl.BlockSpec((pl.Element(HALO), pl.Element(BW)),
                            lambda i: (jnp.minimum(i * hpt + hpt, n_rows // HALO - 1) * HALO, off))

    def const(shape):
        return pl.BlockSpec(shape, lambda i: (0,) * len(shape))

    sbias = jnp.repeat(sgu_b.T, BW // SGU_GROUPS, axis=1)
    row = lambda a: a.reshape(1, BW)
    conv_offs = (OFF_CC, OFF_CX, OFF_GA, OFF_GB)
    return pl.pallas_call(
        functools.partial(_mix_body, n_x_tiles=n_x_tiles),
        out_shape=jax.ShapeDtypeStruct((n_rows, 3 * BW), BF16),
        grid=(n_rows // tm,),
        in_specs=[sec(o) for o in (OFF_SU, OFF_SV, OFF_CB, OFF_CC, OFF_CX, OFF_GA, OFF_GB)]
        + [halo_prev(o) for o in conv_offs] + [halo_next(o) for o in conv_offs]
        + [const((1, BW)), const((1, BW)), const((SGU_GROUPS, SGU_CHUNK, SGU_CHUNK)), const((SGU_CHUNK, BW)),
           const((SCONV_K, BW)), const((CONF_K, BW)), const((1, BW)), const((1, BW)), const((1, BW))],
        out_specs=pl.BlockSpec((tm, 3 * BW), lambda i: (i, 0)),
        scratch_shapes=[pltpu.VMEM((tm + 2 * HALO, BW), F32), pltpu.VMEM((SUBLANES, tm + 2 * HALO, BW), F32)],
        compiler_params=_cparams(("arbitrary",), 32),
        name="mixers",
    )(*([p] * 15), row(sgu_ln_g), row(sgu_ln_b), sgu_w.astype(BF16), sbias, sconv_w, conf_dw_w,
      row(conf_dw_b), row(conf_ln_g), row(conf_ln_b))


def _norm2(xn, n2g, mod):
    return _rms(xn, n2g) * (1.0 + mod[4:5, :]) + mod[3:4, :]


def _post_body(g_ref, ya_ref, yac_ref, yr_ref, wb_ref, wo_ref, x_ref, cx_ref, mod_ref, n2g_ref, rw_ref,
               xo_ref, h2_ref, lg_ref):
    tm = TM_TOK
    merged = None
    for br in range(4):
        y = _stream_tile(ya_ref, yac_ref, tm) if br == 0 else yr_ref[:, (br - 1) * BW:br * BW]
        pr = jnp.dot(y, wb_ref[br], preferred_element_type=F32)
        term = g_ref[:, br * D:(br + 1) * D].astype(F32) * pr
        merged = term if merged is None else merged + term
    out = jnp.dot(merged.astype(BF16), wo_ref[...], preferred_element_type=F32)
    xn = _stream_tile(x_ref, cx_ref, tm) + mod_ref[2:3, :] * out
    xo_ref[...] = xn
    h2 = _norm2(xn, n2g_ref[...], mod_ref)
    h2_ref[...] = h2
    nt = (((1,), (1,)), ((), ()))
    lg_ref[...] = lax.dot_general(rw_ref[...], h2.astype(BF16), nt, preferred_element_type=F32)


def _post(p, ya, yac, yr, wb, wo, x, cx, c_blk0, mods, layer, n2g, rw, n_rows):
    tm = TM_TOK
    one = pl.Buffered(1)
    return pl.pallas_call(
        _post_body,
        out_shape=[jax.ShapeDtypeStruct((n_rows, D), F32), jax.ShapeDtypeStruct((n_rows, D), F32),
                   jax.ShapeDtypeStruct((NE, n_rows), F32)],
        grid=(n_rows // tm,),
        in_specs=[pl.BlockSpec((pl.Element(tm), pl.Element(4 * D)), lambda i: (i * tm, OFF_GATES))]
        + _stream_specs(tm, 0, BW) + [
            pl.BlockSpec((tm, 3 * BW), lambda i: (i, 0)),
            pl.BlockSpec((None, 4, BW, D), lambda i: (layer, 0, 0, 0), pipeline_mode=one),
            pl.BlockSpec((None, D, D), lambda i: (layer, 0, 0), pipeline_mode=one)]
        + _stream_specs(tm, c_blk0) + [
            pl.BlockSpec((None, None, 6, D), lambda i: (layer, _mod_row(i, tm), 0, 0)),
            pl.BlockSpec((1, D), lambda i: (0, 0)),
            pl.BlockSpec((NE, D), lambda i: (0, 0)),
        ],
        out_specs=[pl.BlockSpec((tm, D), lambda i: (i, 0)), pl.BlockSpec((tm, D), lambda i: (i, 0)),
                   pl.BlockSpec((NE, tm), lambda i: (0, i))],
        compiler_params=_cparams(("arbitrary",), 52),
        name="post",
    )(p, ya, yac, yr, wb, wo, x, cx, mods, n2g.reshape(1, D), rw)


def _route_body(lg_ref, rb_ref, tri_ref, o_ref, cnt_ref, carry_ref):
    tm = TM_TOK
    i = pl.program_id(0)

    @pl.when(i == 0)
    def _():
        carry_ref[...] = jnp.zeros_like(carry_ref)

    lg = lg_ref[...]
    e = jnp.exp(lg - jnp.max(lg, axis=0, keepdims=True))
    sc = e / jnp.sum(e, axis=0, keepdims=True)
    bi = sc + rb_ref[...]
    b = [bi[k:k + 1, :] for k in range(NE)]
    s = [sc[k:k + 1, :] for k in range(NE)]

    gs = []
    for g in range(NG):
        v = b[g * EPG:(g + 1) * EPG]
        best = None
        for a in range(EPG):
            for c in range(a + 1, EPG):
                ps = v[a] + v[c]
                best = ps if best is None else jnp.maximum(best, ps)
        gs.append(best)
    gsel = jnp.zeros((1, tm), I32)
    gbest = gs[0]
    for g in range(1, NG):
        take = gs[g] > gbest
        gsel = jnp.where(take, g, gsel)
        gbest = jnp.where(take, gs[g], gbest)

    vb, vs = [], []
    for j in range(EPG):
        xb, xs = b[j], s[j]
        for g in range(1, NG):
            xb = jnp.where(gsel == g, b[g * EPG + j], xb)
            xs = jnp.where(gsel == g, s[g * EPG + j], xs)
        vb.append(xb)
        vs.append(xs)
    order = []
    for j in range(EPG):
        c = jnp.zeros((1, tm), I32)
        for m in range(EPG):
            if m == j:
                continue
            ahead = (vb[m] >= vb[j]) if m < j else (vb[m] > vb[j])
            c = c + jnp.where(ahead, 1, 0)
        order.append(c)
    zero = jnp.zeros((1, tm), F32)
    w0 = zero
    w1 = zero
    j0 = jnp.zeros((1, tm), I32)
    j1 = jnp.zeros((1, tm), I32)
    for j in range(EPG):
        w0 = jnp.where(order[j] == 0, vs[j], w0)
        w1 = jnp.where(order[j] == 1, vs[j], w1)
        j0 = jnp.where(order[j] == 0, j, j0)
        j1 = jnp.where(order[j] == 1, j, j1)
    tot = w0 + w1
    e0 = gsel * EPG + j0
    e1 = gsel * EPG + j1

    eid = lax.broadcasted_iota(I32, (NE, tm), 0)
    oh0 = eid == e0
    oh1 = eid == e1
    oh = jnp.where(oh0 | oh1, 1.0, 0.0)
    rank = jnp.dot(oh.astype(BF16), tri_ref[...], preferred_element_type=F32) + carry_ref[:, 0:1]
    r0 = jnp.sum(jnp.where(oh0, rank, 0.0), axis=0, keepdims=True)
    r1 = jnp.sum(jnp.where(oh1, rank, 0.0), axis=0, keepdims=True)
    new_carry = carry_ref[...] + jnp.sum(oh, axis=1, keepdims=True)
    carry_ref[...] = new_carry
    cnt_ref[...] = new_carry

    o_ref[0:1, :] = e0.astype(F32)
    o_ref[1:2, :] = e1.astype(F32)
    o_ref[2:3, :] = r0
    o_ref[3:4, :] = r1
    o_ref[4:5, :] = w0 / tot
    o_ref[5:6, :] = w1 / tot
    o_ref[6:8, :] = jnp.zeros((2, tm), F32)


def _route(lg, router_b, n_rows):
    tm = TM_TOK
    tri = jnp.asarray(np.triu(np.ones((tm, tm), np.float32), 1), BF16)
    return pl.pallas_call(
        _route_body,
        out_shape=[jax.ShapeDtypeStruct((8, n_rows), F32), jax.ShapeDtypeStruct((NE, LANES), F32)],
        grid=(n_rows // tm,),
        in_specs=[
            pl.BlockSpec((NE, tm), lambda i: (0, i)),
            pl.BlockSpec((NE, 1), lambda i: (0, 0)),
            pl.BlockSpec((tm, tm), lambda i: (0, 0)),
        ],
        out_specs=[pl.BlockSpec((8, tm), lambda i: (0, i)), pl.BlockSpec((NE, LANES), lambda i: (0, 0))],
        scratch_shapes=[pltpu.VMEM((NE, LANES), F32)],
        compiler_params=_cparams(("arbitrary",), 32),
        name="route",
    )(lg, router_b.reshape(NE, 1), tri)


def _row_copy(src_ref, src_row, dst_ref, dst_row, sem):
    return pltpu.make_async_copy(src_ref.at[pl.ds(src_row, 1)], dst_ref.at[pl.ds(dst_row, 1)], sem)


def _issue_rows(n, start_row):
    def trip(t, c):
        base = pl.multiple_of(t * ROW_UNROLL, ROW_UNROLL)
        for u in range(ROW_UNROLL):
            start_row(base + u)
        return c

    lax.fori_loop(0, n // ROW_UNROLL, trip, 0)


def _dispatch_body(tail_ref, need_ref, pos0_ref, pos1_ref, h_ref, xs_ref, zero_ref, sem, zsem):
    tm = TM_TOK

    @pl.when(pl.program_id(0) == 0)
    def _():
        zero_ref[...] = jnp.zeros_like(zero_ref)

        def tail_copy(e):
            start = pl.multiple_of(tail_ref[e], TM_E)
            return pltpu.make_async_copy(zero_ref, xs_ref.at[pl.ds(start, TM_E)], zsem)

        for e in range(2 * NE):
            @pl.when(need_ref[e] > 0)
            def _():
                tail_copy(e).start()
        for e in range(2 * NE):
            @pl.when(need_ref[e] > 0)
            def _():
                tail_copy(e).wait()

    def start_row(r):
        _row_copy(h_ref, r, xs_ref, pos0_ref[0, r], sem).start()
        _row_copy(h_ref, r, xs_ref, pos1_ref[0, r], sem).start()

    _issue_rows(tm, start_row)
    for _ in range(2):
        pltpu.make_async_copy(h_ref, xs_ref.at[pl.ds(0, tm)], sem).wait()


def _pos_specs(tm, index):
    return [pl.BlockSpec((None, None, 1, tm), lambda i, *_, k=k: (k, index(i), 0, 0), memory_space=pltpu.SMEM)
            for k in range(2)]


def _dispatch(tail, need, pos, h2, n_rows):
    tm = TM_TOK
    return pl.pallas_call(
        _dispatch_body,
        out_shape=jax.ShapeDtypeStruct((_n_expert_tiles(n_rows) * TM_E, D), F32),
        grid_spec=pltpu.PrefetchScalarGridSpec(
            num_scalar_prefetch=2,
            grid=(n_rows // tm,),
            in_specs=_pos_specs(tm, lambda i: i) + [pl.BlockSpec((tm, D), lambda i, t, n: (i, 0))],
            out_specs=pl.BlockSpec(memory_space=pl.ANY),
            scratch_shapes=[pltpu.VMEM((TM_E, D), F32), pltpu.SemaphoreType.DMA(()), pltpu.SemaphoreType.DMA(())],
        ),
        compiler_params=_cparams(("arbitrary",), 32),
        name="dispatch",
    )(tail, need, pos, pos, h2)


def _moe_body(te_ref, nu_ref, first_ref, nxt_ref, rem_ref, xs_ref, wu_hbm, wd_hbm, ys_ref,
              wub, wdb, stage, sem, st, *, layer):
    r = pl.program_id(0)

    def chunk_copy(e, k):
        if k < MOE_UP_CHUNKS:
            src = wu_hbm.at[layer, e, pl.ds(k * MOE_W_CHUNK, MOE_W_CHUNK), :]
        else:
            src = wd_hbm.at[layer, e, pl.ds((k - MOE_UP_CHUNKS) * MOE_W_CHUNK, MOE_W_CHUNK), :]
        return pltpu.make_async_copy(src, stage.at[k % 2], sem.at[k % 2])

    def convert(e, k, slot):
        chunk_copy(e, k).wait()
        v = stage[k % 2].astype(BF16)
        if k < MOE_UP_CHUNKS:
            wub[slot, k * MOE_W_CHUNK:(k + 1) * MOE_W_CHUNK, :] = v
        else:
            kd = k - MOE_UP_CHUNKS
            wdb[slot, kd * MOE_W_CHUNK:(kd + 1) * MOE_W_CHUNK, :] = v
        if k + 2 < MOE_CHUNKS:
            chunk_copy(e, k + 2).start()

    def begin(e):
        chunk_copy(e, 0).start()
        chunk_copy(e, 1).start()
        st[1] = 0

    @pl.when(r < nu_ref[0])
    def _():
        e_cur = te_ref[r]
        e_nxt = nxt_ref[r]

        @pl.when(r == 0)
        def _():
            st[0] = 1
            begin(e_cur)

        @pl.when(first_ref[r] == 1)
        def _():
            slot = 1 - st[0]
            done = st[1]
            for k in range(MOE_CHUNKS):
                @pl.when(k >= done)
                def _():
                    convert(e_cur, k, slot)
            st[0] = slot
            st[1] = MOE_CHUNKS

            @pl.when(e_nxt >= 0)
            def _():
                begin(e_nxt)

        cur = st[0]
        x = xs_ref[...].astype(BF16)
        hc = jnp.dot(x, wub[cur], preferred_element_type=F32)
        a = hc[:, :DFF]
        b = hc[:, DFF:]
        act = (a * _sigmoid(a) * b).astype(BF16)
        ys_ref[...] = jnp.dot(act, wdb[cur], preferred_element_type=F32)

        @pl.when(e_nxt >= 0)
        def _():
            done = st[1]
            share = (MOE_CHUNKS - done + rem_ref[r] - 1) // rem_ref[r]
            for k in range(MOE_CHUNKS):
                @pl.when(jnp.logical_and(k >= done, k < done + share))
                def _():
                    convert(e_nxt, k, 1 - cur)
            st[1] = done + share

    @pl.when(r >= nu_ref[0])
    def _():
        ys_ref[...] = jnp.zeros_like(ys_ref)


def _moe(plan, xs, wu, wd, layer):
    def row(r, te, nu, *_):
        return (jnp.minimum(r, nu[0] - 1), 0)

    return pl.pallas_call(
        functools.partial(_moe_body, layer=layer),
        out_shape=jax.ShapeDtypeStruct(xs.shape, F32),
        grid_spec=pltpu.PrefetchScalarGridSpec(
            num_scalar_prefetch=5,
            grid=(xs.shape[0] // TM_E,),
            in_specs=[
                pl.BlockSpec((TM_E, D), row),
                pl.BlockSpec(memory_space=pl.ANY),
                pl.BlockSpec(memory_space=pl.ANY),
            ],
            out_specs=pl.BlockSpec((TM_E, D), lambda r, *_: (r, 0)),
            scratch_shapes=[
                pltpu.VMEM((2, D, 2 * DFF), BF16),
                pltpu.VMEM((2, DFF, D), BF16),
                pltpu.VMEM((2, MOE_W_CHUNK, D), F32),
                pltpu.SemaphoreType.DMA((2,)),
                pltpu.SMEM((2,), I32),
            ],
        ),
        compiler_params=_cparams(("arbitrary",), 56),
        name="moe",
    )(*plan, xs, wu, wd)


def _combine_body(pos0_ref, pos1_ref, posn0_ref, posn1_ref, ys_ref, x_ref, w_ref, mod_ref, g_ref, nmod_ref,
                  *rest, final):
    tm = TM_TOK
    outs, (ybuf, sem) = rest[:-2], rest[-2:]
    i = pl.program_id(0)
    slot = i % 2

    def fetch(p0_ref, p1_ref, s):
        def start_row(r):
            _row_copy(ys_ref, p0_ref[0, r], ybuf.at[s, 0], r, sem.at[s]).start()
            _row_copy(ys_ref, p1_ref[0, r], ybuf.at[s, 1], r, sem.at[s]).start()

        _issue_rows(tm, start_row)

    @pl.when(i == 0)
    def _():
        fetch(pos0_ref, pos1_ref, 0)

    @pl.when(i + 1 < pl.num_programs(0))
    def _():
        fetch(posn0_ref, posn1_ref, 1 - slot)

    for k in range(2):
        pltpu.make_async_copy(ys_ref.at[pl.ds(0, tm)], ybuf.at[slot, k], sem.at[slot]).wait()

    reps = D // LANES
    w0 = jnp.tile(w_ref[:, 0:LANES], (1, reps))
    w1 = jnp.tile(w_ref[:, LANES:2 * LANES], (1, reps))
    xn = x_ref[...] + mod_ref[5:6, :] * (w0 * ybuf[slot, 0] + w1 * ybuf[slot, 1])
    if final:
        outs[0][...] = _rms(xn, g_ref[...])
    else:
        outs[0][...] = xn
        outs[1][...] = (_rms(xn, g_ref[...]) * (1.0 + nmod_ref[1:2, :]) + nmod_ref[0:1, :]).astype(BF16)


def _combine(pos, ys, x, wlanes, mods, layer, n_rows, g_next, final):
    tm = TM_TOK
    nt = n_rows // tm
    next_layer = min(layer + 1, DEPTH - 1)
    tile = pl.BlockSpec((tm, D), lambda i: (i, 0))
    out_shape = [jax.ShapeDtypeStruct((n_rows, D), F32)]
    if not final:
        out_shape.append(jax.ShapeDtypeStruct((n_rows, D), BF16))
    return pl.pallas_call(
        functools.partial(_combine_body, final=final),
        out_shape=out_shape,
        grid=(nt,),
        in_specs=_pos_specs(tm, lambda i: i) + _pos_specs(tm, lambda i: jnp.minimum(i + 1, nt - 1)) + [
            pl.BlockSpec(memory_space=pl.ANY),
            tile,
            pl.BlockSpec((tm, 2 * LANES), lambda i: (i, 0)),
            pl.BlockSpec((None, None, 6, D), lambda i: (layer, _mod_row(i, tm), 0, 0)),
            pl.BlockSpec((1, D), lambda i: (0, 0)),
            pl.BlockSpec((None, None, 6, D), lambda i: (next_layer, _mod_row(i, tm), 0, 0)),
        ],
        out_specs=[tile] * len(out_shape),
        scratch_shapes=[pltpu.VMEM((2, 2, tm, D), F32), pltpu.SemaphoreType.DMA((2,))],
        compiler_params=_cparams(("arbitrary",), 40),
        name="combine",
    )(pos, pos, pos, pos, ys, x, wlanes, mods, g_next.reshape(1, D), mods)


def _route_plan(route, cnt, n_rows):
    counts = cnt[:, 0].astype(I32)
    padded = ((counts + TM_E - 1) // TM_E) * TM_E
    ends = jnp.cumsum(padded)
    offs = ends - padded
    e01 = route[0:2].astype(I32)
    eids = jnp.arange(NE, dtype=I32)[:, None, None]
    off01 = jnp.sum(jnp.where(e01[None] == eids, offs[:, None, None], 0), axis=0)
    pos = off01 + route[2:4].astype(I32)
    pos = pos.reshape(2, n_rows // TM_TOK, 1, TM_TOK)
    nt = _n_expert_tiles(n_rows)
    tile_start = jnp.arange(nt, dtype=I32) * TM_E
    tile_expert = jnp.minimum(jnp.sum((tile_start[:, None] >= ends[None, :]).astype(I32), axis=1), NE - 1)
    n_used = (ends[-1] // TM_E).reshape(1).astype(I32)
    spare = n_used[0] + jnp.arange(NE, dtype=I32)
    zstart = jnp.concatenate([offs + (counts // TM_E) * TM_E, jnp.minimum(spare, nt - 1) * TM_E])
    zneed = jnp.concatenate([counts % TM_E != 0, spare < nt]).astype(I32)
    wl = jnp.concatenate([jnp.broadcast_to(route[4][:, None], (n_rows, LANES)),
                          jnp.broadcast_to(route[5][:, None], (n_rows, LANES))], axis=1)
    ntile = padded // TM_E
    ecol = jnp.arange(NE, dtype=I32)
    onehot = tile_expert[:, None] == ecol[None, :]
    pick = lambda v: jnp.sum(jnp.where(onehot, v[None, :], 0), axis=1)
    j_in = jnp.arange(nt, dtype=I32) - pick(offs // TM_E)
    first = (j_in == 0).astype(I32)
    rem = jnp.maximum(pick(ntile) - j_in, 1)
    later = (ecol[None, :] > ecol[:, None]) & (ntile[None, :] > 0)
    nxt_e = jnp.min(jnp.where(later, ecol[None, :], NE), axis=1)
    nxt = pick(jnp.where(nxt_e < NE, nxt_e, -1))
    moe_plan = (tile_expert.astype(I32), n_used, first, nxt.astype(I32), rem.astype(I32))
    return pos, moe_plan, zstart.astype(I32), zneed, wl


def kernel(x, c, ctx, c_ctx, ada_w, ada_b, norm1_g, norm2_g, w_in, attn_sink, sgu_ln_g, sgu_ln_b, sgu_w, sgu_b,
           sconv_w, conf_dw_w, conf_dw_b, conf_ln_g, conf_ln_b, w_branch, w_out, router_w, router_b,
           exp_w_up, exp_w_down, final_g):
    cvec = jnp.concatenate([c, c_ctx[None, :], jnp.zeros((8 - NB - 1, D), F32)], axis=0)
    mods = _ada(cvec, ada_w, ada_b).reshape(DEPTH, 8, 6, D)
    xa, cxa = x.reshape(TX, D), ctx.reshape(TC, D)
    cos, sin = _rope_tables()
    consts = _rope_constants()
    rw = router_w.T.astype(BF16)
    wb_all = w_branch.astype(BF16)
    wo_all = w_out.astype(BF16)

    for l in range(DEPTH):
        last = l == DEPTH - 1
        n_rows = TX if last else T
        n_x_tiles = TX // TM_TOK
        lp = (sgu_ln_g[l], sgu_ln_b[l], sgu_w[l], sgu_b[l], sconv_w[l], conf_dw_w[l], conf_dw_b[l],
              conf_ln_g[l], conf_ln_b[l])
        c_rows0 = 0 if l == 0 else TX

        if l == 0:
            h = _norm_mod(xa, cxa, norm1_g[l], mods, l)
        if not last:
            p = _inproj(h, w_in, l, T, N_IN // TN_IN)
            qr, kr, vr = _prep(p, T, n_x_tiles, cos, sin, consts)
            ya = _attn(qr, kr, vr, kr, vr, TX // LC, attn_sink[l])
            yac = _ctx_attn(qr, kr, vr, attn_sink[l])
        else:
            p = _inproj(h, w_in, l, TX, N_IN // TN_IN)
            pc = _inproj(h[TX:], w_in, l, TC, 1)
            qr, kr, vr = _prep(p, TX, n_x_tiles, cos, sin, consts)
            _, kc, vc = _prep(pc, TC, 0, cos, sin, consts)
            ya = _attn(qr, kr, vr, kc, vc, 0, attn_sink[l])
            yac = ya
        yr = _mixers(p, n_rows, n_x_tiles, lp)
        xn, h2, lg = _post(p, ya, yac, yr, wb_all, wo_all, xa, cxa, c_rows0 // TM_TOK, mods, l,
                           norm2_g[l], rw, n_rows)
        route, cnt = _route(lg, router_b, n_rows)
        pos, moe_plan, zstart, zneed, wl = _route_plan(route, cnt, n_rows)
        xs = _dispatch(zstart, zneed, pos, h2, n_rows)
        ys = _moe(moe_plan, xs, exp_w_up, exp_w_down, l)
        if last:
            (out,) = _combine(pos, ys, xn, wl, mods, l, n_rows, final_g, True)
        else:
            xa, h = _combine(pos, ys, xn, wl, mods, l, n_rows, norm1_g[l + 1], False)
            cxa = xa

    return out.reshape(NB, S, D)
```

```python
import functools

import numpy as np
import jax
import jax.numpy as jnp
from jax import lax
from jax.experimental import pallas as pl
from jax.experimental.pallas import tpu as pltpu

F32 = jnp.float32
BF16 = jnp.bfloat16
I32 = jnp.int32

D = 2048
NB = 4
S = 2048
LC = 256
DEPTH = 2
GRID_W = 64
BW = 512
HD = 64
NH = 8
NKV = 2
REP = NH // NKV
WINDOW = 128
QB = 128
ROPE_THETA = 10000.0
SGU_CHUNK = 128
SGU_GROUPS = 4
SCONV_K = 3
CONF_K = 31
NE = 16
NG = 4
EPG = NE // NG
DFF = D // 2
N_IN = BW + 2 * NKV * HD + 7 * BW + 4 * D
TX = NB * S
TC = NB * LC
T = TX + TC

OFF_Q, OFF_KV = 0, 512
OFF_SU, OFF_SV, OFF_CB, OFF_CC, OFF_CX, OFF_GA, OFF_GB, OFF_GATES = 768, 1280, 1792, 2304, 2816, 3328, 3840, 4352

LANES = 128
SUBLANES = 8
V7X_VMEM_BYTES = 64 * 1024 * 1024
MIB = 1024 * 1024

TM_NORM = 512
TM_IN = 1024
INPROJ_ROW_CHUNK = 512
IN_W_CHUNK = 256
IN_CHUNKS = D // IN_W_CHUNK
TN_IN = 1792
TM_TOK = 256
TM_POST = 512
POST_ROW_CHUNK = 256
TM_E = 256
HALO = 16
ROW_UNROLL = 8
COMBINE_ROWS_PER_TRIP = 32
MOE_W_CHUNK = 512
MOE_UP_CHUNKS = D // MOE_W_CHUNK
MOE_CHUNKS = MOE_UP_CHUNKS + DFF // MOE_W_CHUNK


def _n_expert_tiles(n_rows):
    return (2 * n_rows) // TM_E + NE


def _cparams(sem, vmem_mib):
    return pltpu.CompilerParams(dimension_semantics=sem, vmem_limit_bytes=vmem_mib * MIB)


def _sigmoid(x):
    return 0.5 * jnp.tanh(0.5 * x) + 0.5


def _gelu_tanh(x):
    c = np.float32(np.sqrt(2.0 / np.pi))
    return 0.5 * x * (1.0 + jnp.tanh(c * (x + np.float32(0.044715) * (x * x * x))))


def _layernorm(x, g, b, eps=1e-5):
    mu = jnp.mean(x, axis=-1, keepdims=True)
    xc = x - mu
    var = jnp.mean(xc * xc, axis=-1, keepdims=True)
    return xc * lax.rsqrt(var + eps) * g + b


def _mod_row(i, tm):
    return jnp.where(i < TX // tm, (i * tm) // S, NB)


def _ada_body(c_ref, w_ref, b_ref, o_ref):
    c = c_ref[...]
    s = (c * _sigmoid(c)).astype(BF16)
    o_ref[...] = jnp.dot(s, w_ref[...].astype(BF16), preferred_element_type=F32) + b_ref[...]


def _ada(cvec, ada_w, ada_b):
    tn = 1024
    return pl.pallas_call(
        _ada_body,
        out_shape=jax.ShapeDtypeStruct((DEPTH, 8, 6 * D), F32),
        grid=(DEPTH, 6 * D // tn),
        in_specs=[
            pl.BlockSpec((8, D), lambda l, j: (0, 0)),
            pl.BlockSpec((None, D, tn), lambda l, j: (l, 0, j)),
            pl.BlockSpec((None, 1, tn), lambda l, j: (l, 0, j)),
        ],
        out_specs=pl.BlockSpec((None, 8, tn), lambda l, j: (l, 0, j)),
        compiler_params=_cparams(("arbitrary", "arbitrary"), 40),
        name="ada",
    )(cvec, ada_w, ada_b.reshape(DEPTH, 1, 6 * D))


def _stream_specs(tm, c_blk0, width=D):
    nx = TX // tm
    return [pl.BlockSpec((tm, width), lambda i, *_: (jnp.minimum(i, nx - 1), 0)),
            pl.BlockSpec((tm, width), lambda i, *_: (c_blk0 + jnp.maximum(i - nx, 0), 0))]


def _stream_tile(x_ref, c_ref, tm):
    return jnp.where(pl.program_id(0) < TX // tm, x_ref[...], c_ref[...])


def _rms(x, g):
    return x * lax.rsqrt(jnp.mean(x * x, axis=-1, keepdims=True) + 1e-6) * g


def _norm_mod_body(x_ref, c_ref, g_ref, mod_ref, o_ref):
    y = _rms(_stream_tile(x_ref, c_ref, TM_NORM), g_ref[...])
    o_ref[...] = (y * (1.0 + mod_ref[1:2, :]) + mod_ref[0:1, :]).astype(o_ref.dtype)


def _norm_mod(x, cx, g, mods, layer):
    tm = TM_NORM
    return pl.pallas_call(
        _norm_mod_body,
        out_shape=jax.ShapeDtypeStruct((T, D), BF16),
        grid=(T // tm,),
        in_specs=_stream_specs(tm, 0) + [
            pl.BlockSpec((1, D), lambda i: (0, 0)),
            pl.BlockSpec((None, None, 6, D), lambda i: (layer, _mod_row(i, tm), 0, 0)),
        ],
        out_specs=pl.BlockSpec((tm, D), lambda i: (i, 0)),
        compiler_params=_cparams(("arbitrary",), 32),
        name="norm_mod",
    )(x, cx, g.reshape(1, D), mods)


def _inproj_body(h_ref, w_hbm, o_ref, wbf, stage, sem, st, *, layer):
    j = pl.program_id(0)
    i = pl.program_id(1)
    nj = pl.num_programs(0)
    ni = pl.num_programs(1)

    def chunk_copy(jj, k):
        cols = pl.ds(pl.multiple_of(jj * TN_IN, LANES), TN_IN)
        src = w_hbm.at[layer, pl.ds(k * IN_W_CHUNK, IN_W_CHUNK), cols]
        return pltpu.make_async_copy(src, stage.at[k % 2], sem.at[k % 2])

    def convert(jj, k, slot):
        chunk_copy(jj, k).wait()
        wbf[slot, k * IN_W_CHUNK:(k + 1) * IN_W_CHUNK, :] = stage[k % 2].astype(BF16)
        if k + 2 < IN_CHUNKS:
            chunk_copy(jj, k + 2).start()

    def begin(jj):
        chunk_copy(jj, 0).start()
        chunk_copy(jj, 1).start()
        st[1] = 0

    @pl.when(jnp.logical_and(i == 0, j == 0))
    def _():
        st[0] = 1
        begin(0)

    @pl.when(i == 0)
    def _():
        slot = 1 - st[0]
        done = st[1]
        for k in range(IN_CHUNKS):
            @pl.when(k >= done)
            def _():
                convert(j, k, slot)
        st[0] = slot
        st[1] = IN_CHUNKS

        @pl.when(j + 1 < nj)
        def _():
            begin(j + 1)

    cur = st[0]
    col0 = j * TN_IN
    rows = INPROJ_ROW_CHUNK

    def run(epilogue):
        for c in range(h_ref.shape[0] // rows):
            sl = slice(c * rows, (c + 1) * rows)
            acc = jnp.dot(h_ref[sl, :], wbf[cur], preferred_element_type=F32)
            o_ref[sl, :] = epilogue(acc).astype(BF16)

    @pl.when(col0 + TN_IN <= OFF_GB)
    def _():
        run(lambda acc: acc)

    @pl.when(col0 >= OFF_GB)
    def _():
        run(_sigmoid)

    @pl.when(jnp.logical_and(col0 < OFF_GB, col0 + TN_IN > OFF_GB))
    def _():
        col = col0 + lax.broadcasted_iota(I32, (rows, TN_IN), 1)
        run(lambda acc: jnp.where(col >= OFF_GB, _sigmoid(acc), acc))

    @pl.when(j + 1 < nj)
    def _():
        done = st[1]
        left = ni - i
        share = (IN_CHUNKS - done + left - 1) // left
        for k in range(IN_CHUNKS):
            @pl.when(jnp.logical_and(k >= done, k < done + share))
            def _():
                convert(j + 1, k, 1 - cur)
        st[1] = done + share


def _inproj(h, w_in, layer, n_rows, n_col_tiles):
    tm = TM_IN
    return pl.pallas_call(
        functools.partial(_inproj_body, layer=layer),
        out_shape=jax.ShapeDtypeStruct((n_rows, n_col_tiles * TN_IN), BF16),
        grid=(n_col_tiles, n_rows // tm),
        in_specs=[
            pl.BlockSpec((tm, D), lambda j, i: (i, 0)),
            pl.BlockSpec(memory_space=pl.ANY),
        ],
        out_specs=pl.BlockSpec((tm, TN_IN), lambda j, i: (i, j)),
        scratch_shapes=[
            pltpu.VMEM((2, D, TN_IN), BF16),
            pltpu.VMEM((2, IN_W_CHUNK, TN_IN), F32),
            pltpu.SemaphoreType.DMA((2,)),
            pltpu.SMEM((2,), I32),
        ],
        compiler_params=_cparams(("arbitrary", "arbitrary"), 56),
        name="inproj",
    )(h, w_in)


def _rope_constants():
    rh = np.zeros((HD, HD), np.float32)
    for base in (0, 32):
        for d in range(16):
            rh[base + 16 + d, base + d] = -1.0
            rh[base + d, base + 16 + d] = 1.0
    rq = np.kron(np.eye(NH, dtype=np.float32), rh)
    rk = np.kron(np.eye(NKV, dtype=np.float32), rh)
    rep = np.zeros((NKV * HD, NH * HD), np.float32)
    for g in range(NKV):
        for r in range(REP):
            for d in range(HD):
                rep[g * HD + d, g * REP * HD + r * HD + d] = 1.0
    return jnp.asarray(rq, BF16), jnp.asarray(rep, BF16), jnp.asarray(rk @ rep, BF16)


def _rope_tables():
    half = HD // 2
    inv = 1.0 / (ROPE_THETA ** (jnp.arange(0, half, 2, dtype=F32) / half))
    pos = jnp.arange(S)
    ar = (pos // GRID_W).astype(F32)[:, None] * inv
    ac = (pos % GRID_W).astype(F32)[:, None] * inv
    cos = jnp.tile(jnp.concatenate([jnp.cos(ar), jnp.cos(ar), jnp.cos(ac), jnp.cos(ac)], axis=1), (1, NH))
    sin = jnp.tile(jnp.concatenate([jnp.sin(ar), jnp.sin(ar), jnp.sin(ac), jnp.sin(ac)], axis=1), (1, NH))
    cos = jnp.concatenate([cos, jnp.ones((TM_TOK, NH * HD), F32)], axis=0)
    sin = jnp.concatenate([sin, jnp.zeros((TM_TOK, NH * HD), F32)], axis=0)
    return cos, sin


def _prep_body(q_ref, kv_ref, cos_ref, sin_ref, rq_ref, rep_ref, rrep_ref, qo_ref, ko_ref, vo_ref):
    cos = cos_ref[...]
    sin = sin_ref[...]
    q = q_ref[...]
    qs = jnp.dot(q, rq_ref[...], preferred_element_type=F32)
    qo_ref[...] = ((q.astype(F32) * cos + qs * sin) * (HD ** -0.5)).astype(BF16)
    k = kv_ref[:, 0:NKV * HD]
    v = kv_ref[:, NKV * HD:2 * NKV * HD]
    kr = jnp.dot(k, rep_ref[...], preferred_element_type=F32)
    ks = jnp.dot(k, rrep_ref[...], preferred_element_type=F32)
    ko_ref[...] = (kr * cos + ks * sin).astype(BF16)
    vo_ref[...] = jnp.dot(v, rep_ref[...], preferred_element_type=F32).astype(BF16)


def _prep(p, n_rows, n_x_tiles, cos, sin, consts):
    tm = TM_TOK
    rq, rep, rrep = consts
    tps = S // tm

    def tab(i):
        return (jnp.where(i < n_x_tiles, i % tps, tps), 0)

    w = NH * HD
    return pl.pallas_call(
        _prep_body,
        out_shape=[jax.ShapeDtypeStruct((n_rows, w), BF16)] * 3,
        grid=(n_rows // tm,),
        in_specs=[
            pl.BlockSpec((tm, w), lambda i: (i, 0)),
            pl.BlockSpec((tm, 2 * NKV * HD), lambda i: (i, OFF_KV // (2 * NKV * HD))),
            pl.BlockSpec((tm, w), tab),
            pl.BlockSpec((tm, w), tab),
            pl.BlockSpec((w, w), lambda i: (0, 0)),
            pl.BlockSpec((NKV * HD, w), lambda i: (0, 0)),
            pl.BlockSpec((NKV * HD, w), lambda i: (0, 0)),
        ],
        out_specs=[pl.BlockSpec((tm, w), lambda i: (i, 0))] * 3,
        compiler_params=_cparams(("arbitrary",), 32),
        name="prep",
    )(p, p, cos, sin, rq, rep, rrep)


def _attn_groups(q, kbs, vbs, sink_ref, valid, nq):
    gw = REP * HD
    lane_head = lax.broadcasted_iota(I32, (nq, gw), 1) // HD
    row_head = lax.broadcasted_iota(I32, (REP * nq, 1), 0) // nq
    valid_rep = None if valid is None else jnp.concatenate([valid] * REP, axis=0)
    scores = []
    for g in range(NKV):
        qg = q[:, g * gw:(g + 1) * gw]
        qs = jnp.concatenate([jnp.where(lane_head == r, qg, jnp.zeros_like(qg)) for r in range(REP)], axis=0)
        s = lax.dot_general(qs, kbs[g], (((1,), (1,)), ((), ())), preferred_element_type=F32)
        scores.append(s if valid_rep is None else jnp.where(valid_rep, s, -jnp.inf))
    probs = []
    for g in range(NKV):
        s = scores[g]
        sink = jnp.zeros((REP * nq, 1), F32)
        for r in range(REP):
            sink = jnp.where(row_head == r, sink_ref[g * REP + r], sink)
        m = jnp.maximum(jnp.max(s, axis=-1, keepdims=True), sink)
        e = jnp.exp(s - m)
        den = jnp.sum(e, axis=-1, keepdims=True) + jnp.exp(sink - m)
        probs.append((e * (1.0 / den)).astype(BF16))
    outs = []
    for g in range(NKV):
        o = jnp.dot(probs[g], vbs[g], preferred_element_type=F32)
        og = jnp.zeros((nq, gw), F32)
        for r in range(REP):
            og = og + jnp.where(lane_head == r, o[r * nq:(r + 1) * nq, :], 0.0)
        outs.append(og)
    return jnp.concatenate(outs, axis=1).astype(BF16)


def _attn_body(sink_ref, q_ref, kp_ref, kc_ref, kn_ref, vp_ref, vc_ref, vn_ref, kx_ref, vx_ref, o_ref):
    n = pl.program_id(1)
    nblk = S // QB
    nk = 3 * QB + LC
    row = lax.broadcasted_iota(I32, (QB, nk), 0)
    col = lax.broadcasted_iota(I32, (QB, nk), 1)
    lo = jnp.where(n == 0, QB, 0)
    hi = jnp.where(n == nblk - 1, 2 * QB, 3 * QB)
    band = (col >= row) & (col <= row + 2 * WINDOW) & (col >= lo) & (col < hi)
    valid = band | (col >= 3 * QB)
    gw = REP * HD
    kbs, vbs = [], []
    for g in range(NKV):
        sl = slice(g * gw, (g + 1) * gw)
        kbs.append(jnp.concatenate([kp_ref[:, sl], kc_ref[:, sl], kn_ref[:, sl], kx_ref[:, sl]], axis=0))
        vbs.append(jnp.concatenate([vp_ref[:, sl], vc_ref[:, sl], vn_ref[:, sl], vx_ref[:, sl]], axis=0))
    o_ref[...] = _attn_groups(q_ref[...], kbs, vbs, sink_ref, valid, QB)


def _attn(qr, kr, vr, kc_arr, vc_arr, ctx_blk0, sink):
    nblk = S // QB
    w = NH * HD

    def cur(b, n):
        return (b * nblk + n, 0)

    def prev(b, n):
        return (b * nblk + jnp.maximum(n - 1, 0), 0)

    def nxt(b, n):
        return (b * nblk + jnp.minimum(n + 1, nblk - 1), 0)

    def cx(b, n):
        return (ctx_blk0 + b, 0)

    blk = lambda f: pl.BlockSpec((QB, w), f)
    return pl.pallas_call(
        _attn_body,
        out_shape=jax.ShapeDtypeStruct((TX, w), BF16),
        grid=(NB, nblk),
        in_specs=[
            pl.BlockSpec(memory_space=pltpu.SMEM),
            blk(cur), blk(prev), blk(cur), blk(nxt), blk(prev), blk(cur), blk(nxt),
            pl.BlockSpec((LC, w), cx), pl.BlockSpec((LC, w), cx),
        ],
        out_specs=blk(cur),
        compiler_params=_cparams(("arbitrary", "arbitrary"), 32),
        name="window_attn",
    )(sink, qr, kr, kr, kr, vr, vr, vr, kc_arr, vc_arr)


def _ctx_attn_body(sink_ref, q_ref, kx_ref, vx_ref, o_ref):
    gw = REP * HD
    kbs = [kx_ref[:, g * gw:(g + 1) * gw] for g in range(NKV)]
    vbs = [vx_ref[:, g * gw:(g + 1) * gw] for g in range(NKV)]
    o_ref[...] = _attn_groups(q_ref[...], kbs, vbs, sink_ref, None, LC)


def _ctx_attn(qr, kr, vr, sink):
    w = NH * HD
    blk0 = TX // LC
    spec = pl.BlockSpec((LC, w), lambda b: (blk0 + b, 0))
    return pl.pallas_call(
        _ctx_attn_body,
        out_shape=jax.ShapeDtypeStruct((TC, w), BF16),
        grid=(NB,),
        in_specs=[pl.BlockSpec(memory_space=pltpu.SMEM), spec, spec, spec],
        out_specs=pl.BlockSpec((LC, w), lambda b: (b, 0)),
        compiler_params=_cparams(("arbitrary",), 32),
        name="ctx_attn",
    )(sink, qr, kr, vr)


def _mix_body(su_ref, sv_ref, cb_ref, cc_ref, cx_ref, ga_ref, gb_ref,
              ccp_ref, cxp_ref, gap_ref, gbp_ref, ccn_ref, cxn_ref, gan_ref, gbn_ref,
              lng_ref, lnb_ref, sw_ref, sbias_ref, scw_ref, dww_ref, dwb_ref, clg_ref, clb_ref,
              o_ref, m_scr, z_scr, *, n_x_tiles):
    tm = TM_TOK
    i = pl.program_id(0)
    tps = S // tm
    is_x = i < n_x_tiles
    first = jnp.logical_or(jnp.logical_not(is_x), (i % tps) == 0)
    last = jnp.logical_or(jnp.logical_not(is_x), (i % tps) == tps - 1)
    keep_p = jnp.where(first, 0.0, 1.0)
    keep_n = jnp.where(last, 0.0, 1.0)

    u = _gelu_tanh(su_ref[...].astype(F32))
    v = _layernorm(_gelu_tanh(sv_ref[...].astype(F32)), lng_ref[...], lnb_ref[...]).astype(BF16)
    gwid = BW // SGU_GROUPS
    chunks = []
    for c in range(tm // SGU_CHUNK):
        parts = []
        for g in range(SGU_GROUPS):
            vc = v[c * SGU_CHUNK:(c + 1) * SGU_CHUNK, g * gwid:(g + 1) * gwid]
            parts.append(jnp.dot(sw_ref[g], vc, preferred_element_type=F32))
        chunks.append(jnp.concatenate(parts, axis=1) + sbias_ref[...])
    yb = u * jnp.concatenate(chunks, axis=0)

    m_scr[HALO:HALO + tm, :] = cc_ref[...].astype(F32) * cx_ref[...].astype(F32)
    m_scr[0:HALO, :] = ccp_ref[...].astype(F32) * cxp_ref[...].astype(F32) * keep_p
    m_scr[HALO + tm:, :] = ccn_ref[...].astype(F32) * cxn_ref[...].astype(F32) * keep_n
    z = jnp.zeros((tm, BW), F32)
    for k in range(SCONV_K):
        z = z + scw_ref[k:k + 1, :] * m_scr[pl.ds(HALO - SCONV_K // 2 + k, tm), :]
    yc = cb_ref[...].astype(F32) * z

    z_scr[0, HALO:HALO + tm, :] = ga_ref[...].astype(F32) * gb_ref[...].astype(F32)
    z_scr[0, 0:HALO, :] = gap_ref[...].astype(F32) * gbp_ref[...].astype(F32) * keep_p
    z_scr[0, HALO + tm:, :] = gan_ref[...].astype(F32) * gbn_ref[...].astype(F32) * keep_n
    n_sh = tm + 2 * HALO - SUBLANES
    for s in range(1, SUBLANES):
        z_scr[s, 0:n_sh, :] = z_scr[0, pl.ds(s, n_sh), :]
    acc = jnp.zeros((tm, BW), F32) + dwb_ref[...]
    for k in range(CONF_K):
        off = HALO - CONF_K // 2 + k
        base = off - off % SUBLANES
        acc = acc + dww_ref[k:k + 1, :] * z_scr[off % SUBLANES, base:base + tm, :]
    zn = _layernorm(acc, clg_ref[...], clb_ref[...])
    yd = zn * _sigmoid(zn)

    o_ref[...] = jnp.concatenate([yb, yc, yd], axis=1).astype(BF16)


def _mixers(p, n_rows, n_x_tiles, lp):
    tm = TM_TOK
    (sgu_ln_g, sgu_ln_b, sgu_w, sgu_b, sconv_w, conf_dw_w, conf_dw_b, conf_ln_g, conf_ln_b) = lp

    def sec(off):
        return pl.BlockSpec((pl.Element(tm), pl.Element(BW)), lambda i: (i * tm, off))

    hpt = tm // HALO

    def halo_prev(off):
        return pl.BlockSpec((pl.Element(HALO), pl.Element(BW)),
                            lambda i: (jnp.maximum(i * hpt - 1, 0) * HALO, off))

    def halo_next(off):
        return pl.BlockSpec((pl.Element(HALO), pl.Element(BW)),
                            lambda i: (jnp.minimum(i * hpt + hpt, n_rows // HALO - 1) * HALO, off))

    def const(shape):
        return pl.BlockSpec(shape, lambda i: (0,) * len(shape))

    sbias = jnp.repeat(sgu_b.T, BW // SGU_GROUPS, axis=1)
    row = lambda a: a.reshape(1, BW)
    conv_offs = (OFF_CC, OFF_CX, OFF_GA, OFF_GB)
    return pl.pallas_call(
        functools.partial(_mix_body, n_x_tiles=n_x_tiles),
        out_shape=jax.ShapeDtypeStruct((n_rows, 3 * BW), BF16),
        grid=(n_rows // tm,),
        in_specs=[sec(o) for o in (OFF_SU, OFF_SV, OFF_CB, OFF_CC, OFF_CX, OFF_GA, OFF_GB)]
        + [halo_prev(o) for o in conv_offs] + [halo_next(o) for o in conv_offs]
        + [const((1, BW)), const((1, BW)), const((SGU_GROUPS, SGU_CHUNK, SGU_CHUNK)), const((SGU_CHUNK, BW)),
           const((SCONV_K, BW)), const((CONF_K, BW)), const((1, BW)), const((1, BW)), const((1, BW))],
        out_specs=pl.BlockSpec((tm, 3 * BW), lambda i: (i, 0)),
        scratch_shapes=[pltpu.VMEM((tm + 2 * HALO, BW), F32), pltpu.VMEM((SUBLANES, tm + 2 * HALO, BW), F32)],
        compiler_params=_cparams(("arbitrary",), 32),
        name="mixers",
    )(*([p] * 15), row(sgu_ln_g), row(sgu_ln_b), sgu_w.astype(BF16), sbias, sconv_w, conf_dw_w,
      row(conf_dw_b), row(conf_ln_g), row(conf_ln_b))


def _norm2(xn, n2g, mod):
    return _rms(xn, n2g) * (1.0 + mod[4:5, :]) + mod[3:4, :]


def _post_body(g_ref, ya_ref, yac_ref, yr_ref, wb_ref, wo_ref, x_ref, cx_ref, mod_ref, n2g_ref, rw_ref,
               xo_ref, h2_ref, lg_ref):
    tm = TM_TOK
    merged = None
    for br in range(4):
        y = _stream_tile(ya_ref, yac_ref, tm) if br == 0 else yr_ref[:, (br - 1) * BW:br * BW]
        pr = jnp.dot(y, wb_ref[br], preferred_element_type=F32)
        term = g_ref[:, br * D:(br + 1) * D].astype(F32) * pr
        merged = term if merged is None else merged + term
    out = jnp.dot(merged.astype(BF16), wo_ref[...], preferred_element_type=F32)
    xn = _stream_tile(x_ref, cx_ref, tm) + mod_ref[2:3, :] * out
    xo_ref[...] = xn
    h2 = _norm2(xn, n2g_ref[...], mod_ref)
    h2_ref[...] = h2
    nt = (((1,), (1,)), ((), ()))
    lg_ref[...] = lax.dot_general(rw_ref[...], h2.astype(BF16), nt, preferred_element_type=F32)


def _post(p, ya, yac, yr, wb, wo, x, cx, c_blk0, mods, layer, n2g, rw, n_rows):
    tm = TM_TOK
    one = pl.Buffered(1)
    return pl.pallas_call(
        _post_body,
        out_shape=[jax.ShapeDtypeStruct((n_rows, D), F32), jax.ShapeDtypeStruct((n_rows, D), F32),
                   jax.ShapeDtypeStruct((NE, n_rows), F32)],
        grid=(n_rows // tm,),
        in_specs=[pl.BlockSpec((pl.Element(tm), pl.Element(4 * D)), lambda i: (i * tm, OFF_GATES))]
        + _stream_specs(tm, 0, BW) + [
            pl.BlockSpec((tm, 3 * BW), lambda i: (i, 0)),
            pl.BlockSpec((None, 4, BW, D), lambda i: (layer, 0, 0, 0), pipeline_mode=one),
            pl.BlockSpec((None, D, D), lambda i: (layer, 0, 0), pipeline_mode=one)]
        + _stream_specs(tm, c_blk0) + [
            pl.BlockSpec((None, None, 6, D), lambda i: (layer, _mod_row(i, tm), 0, 0)),
            pl.BlockSpec((1, D), lambda i: (0, 0)),
            pl.BlockSpec((NE, D), lambda i: (0, 0)),
        ],
        out_specs=[pl.BlockSpec((tm, D), lambda i: (i, 0)), pl.BlockSpec((tm, D), lambda i: (i, 0)),
                   pl.BlockSpec((NE, tm), lambda i: (0, i))],
        compiler_params=_cparams(("arbitrary",), 52),
        name="post",
    )(p, ya, yac, yr, wb, wo, x, cx, mods, n2g.reshape(1, D), rw)


def _route_body(lg_ref, rb_ref, tri_ref, o_ref, cnt_ref, carry_ref):
    tm = TM_TOK
    i = pl.program_id(0)

    @pl.when(i == 0)
    def _():
        carry_ref[...] = jnp.zeros_like(carry_ref)

    lg = lg_ref[...]
    e = jnp.exp(lg - jnp.max(lg, axis=0, keepdims=True))
    sc = e / jnp.sum(e, axis=0, keepdims=True)
    bi = sc + rb_ref[...]
    b = [bi[k:k + 1, :] for k in range(NE)]
    s = [sc[k:k + 1, :] for k in range(NE)]

    gs = []
    for g in range(NG):
        v = b[g * EPG:(g + 1) * EPG]
        best = None
        for a in range(EPG):
            for c in range(a + 1, EPG):
                ps = v[a] + v[c]
                best = ps if best is None else jnp.maximum(best, ps)
        gs.append(best)
    gsel = jnp.zeros((1, tm), I32)
    gbest = gs[0]
    for g in range(1, NG):
        take = gs[g] > gbest
        gsel = jnp.where(take, g, gsel)
        gbest = jnp.where(take, gs[g], gbest)

    vb, vs = [], []
    for j in range(EPG):
        xb, xs = b[j], s[j]
        for g in range(1, NG):
            xb = jnp.where(gsel == g, b[g * EPG + j], xb)
            xs = jnp.where(gsel == g, s[g * EPG + j], xs)
        vb.append(xb)
        vs.append(xs)
    order = []
    for j in range(EPG):
        c = jnp.zeros((1, tm), I32)
        for m in range(EPG):
            if m == j:
                continue
            ahead = (vb[m] >= vb[j]) if m < j else (vb[m] > vb[j])
            c = c + jnp.where(ahead, 1, 0)
        order.append(c)
    zero = jnp.zeros((1, tm), F32)
    w0 = zero
    w1 = zero
    j0 = jnp.zeros((1, tm), I32)
    j1 = jnp.zeros((1, tm), I32)
    for j in range(EPG):
        w0 = jnp.where(order[j] == 0, vs[j], w0)
        w1 = jnp.where(order[j] == 1, vs[j], w1)
        j0 = jnp.where(order[j] == 0, j, j0)
        j1 = jnp.where(order[j] == 1, j, j1)
    tot = w0 + w1
    e0 = gsel * EPG + j0
    e1 = gsel * EPG + j1

    eid = lax.broadcasted_iota(I32, (NE, tm), 0)
    oh0 = eid == e0
    oh1 = eid == e1
    oh = jnp.where(oh0 | oh1, 1.0, 0.0)
    rank = jnp.dot(oh.astype(BF16), tri_ref[...], preferred_element_type=F32) + carry_ref[:, 0:1]
    r0 = jnp.sum(jnp.where(oh0, rank, 0.0), axis=0, keepdims=True)
    r1 = jnp.sum(jnp.where(oh1, rank, 0.0), axis=0, keepdims=True)
    new_carry = carry_ref[...] + jnp.sum(oh, axis=1, keepdims=True)
    carry_ref[...] = new_carry
    cnt_ref[...] = new_carry

    o_ref[0:1, :] = e0.astype(F32)
    o_ref[1:2, :] = e1.astype(F32)
    o_ref[2:3, :] = r0
    o_ref[3:4, :] = r1
    o_ref[4:5, :] = w0 / tot
    o_ref[5:6, :] = w1 / tot
    o_ref[6:8, :] = jnp.zeros((2, tm), F32)


def _route(lg, router_b, n_rows):
    tm = TM_TOK
    tri = jnp.asarray(np.triu(np.ones((tm, tm), np.float32), 1), BF16)
    return pl.pallas_call(
        _route_body,
        out_shape=[jax.ShapeDtypeStruct((8, n_rows), F32), jax.ShapeDtypeStruct((NE, LANES), F32)],
        grid=(n_rows // tm,),
        in_specs=[
            pl.BlockSpec((NE, tm), lambda i: (0, i)),
            pl.BlockSpec((NE, 1), lambda i: (0, 0)),
            pl.BlockSpec((tm, tm), lambda i: (0, 0)),
        ],
        out_specs=[pl.BlockSpec((8, tm), lambda i: (0, i)), pl.BlockSpec((NE, LANES), lambda i: (0, 0))],
        scratch_shapes=[pltpu.VMEM((NE, LANES), F32)],
        compiler_params=_cparams(("arbitrary",), 32),
        name="route",
    )(lg, router_b.reshape(NE, 1), tri)


def _row_copy(src_ref, src_row, dst_ref, dst_row, sem):
    return pltpu.make_async_copy(src_ref.at[pl.ds(src_row, 1)], dst_ref.at[pl.ds(dst_row, 1)], sem)


def _issue_rows(n, start_row):
    def trip(t, c):
        base = pl.multiple_of(t * ROW_UNROLL, ROW_UNROLL)
        for u in range(ROW_UNROLL):
            start_row(base + u)
        return c

    lax.fori_loop(0, n // ROW_UNROLL, trip, 0)


def _dispatch_body(tail_ref, need_ref, pos0_ref, pos1_ref, h_ref, xs_ref, zero_ref, sem, zsem):
    tm = TM_TOK

    @pl.when(pl.program_id(0) == 0)
    def _():
        zero_ref[...] = jnp.zeros_like(zero_ref)

        def tail_copy(e):
            start = pl.multiple_of(tail_ref[e], TM_E)
            return pltpu.make_async_copy(zero_ref, xs_ref.at[pl.ds(start, TM_E)], zsem)

        for e in range(2 * NE):
            @pl.when(need_ref[e] > 0)
            def _():
                tail_copy(e).start()
        for e in range(2 * NE):
            @pl.when(need_ref[e] > 0)
            def _():
                tail_copy(e).wait()

    def start_row(r):
        _row_copy(h_ref, r, xs_ref, pos0_ref[0, r], sem).start()
        _row_copy(h_ref, r, xs_ref, pos1_ref[0, r], sem).start()

    _issue_rows(tm, start_row)
    for _ in range(2):
        pltpu.make_async_copy(h_ref, xs_ref.at[pl.ds(0, tm)], sem).wait()


def _pos_specs(tm, index):
    return [pl.BlockSpec((None, None, 1, tm), lambda i, *_, k=k: (k, index(i), 0, 0), memory_space=pltpu.SMEM)
            for k in range(2)]


def _dispatch(tail, need, pos, h2, n_rows):
    tm = TM_TOK
    return pl.pallas_call(
        _dispatch_body,
        out_shape=jax.ShapeDtypeStruct((_n_expert_tiles(n_rows) * TM_E, D), F32),
        grid_spec=pltpu.PrefetchScalarGridSpec(
            num_scalar_prefetch=2,
            grid=(n_rows // tm,),
            in_specs=_pos_specs(tm, lambda i: i) + [pl.BlockSpec((tm, D), lambda i, t, n: (i, 0))],
            out_specs=pl.BlockSpec(memory_space=pl.ANY),
            scratch_shapes=[pltpu.VMEM((TM_E, D), F32), pltpu.SemaphoreType.DMA(()), pltpu.SemaphoreType.DMA(())],
        ),
        compiler_params=_cparams(("arbitrary",), 32),
        name="dispatch",
    )(tail, need, pos, pos, h2)


def _moe_body(te_ref, nu_ref, first_ref, nxt_ref, rem_ref, xs_ref, wu_hbm, wd_hbm, ys_ref,
              wub, wdb, stage, sem, st, *, layer):
    r = pl.program_id(0)

    def chunk_copy(e, k):
        if k < MOE_UP_CHUNKS:
            src = wu_hbm.at[layer, e, pl.ds(k * MOE_W_CHUNK, MOE_W_CHUNK), :]
        else:
            src = wd_hbm.at[layer, e, pl.ds((k - MOE_UP_CHUNKS) * MOE_W_CHUNK, MOE_W_CHUNK), :]
        return pltpu.make_async_copy(src, stage.at[k % 2], sem.at[k % 2])

    def convert(e, k, slot):
        chunk_copy(e, k).wait()
        v = stage[k % 2].astype(BF16)
        if k < MOE_UP_CHUNKS:
            wub[slot, k * MOE_W_CHUNK:(k + 1) * MOE_W_CHUNK, :] = v
        else:
            kd = k - MOE_UP_CHUNKS
            wdb[slot, kd * MOE_W_CHUNK:(kd + 1) * MOE_W_CHUNK, :] = v
        if k + 2 < MOE_CHUNKS:
            chunk_copy(e, k + 2).start()

    def begin(e):
        chunk_copy(e, 0).start()
        chunk_copy(e, 1).start()
        st[1] = 0

    @pl.when(r < nu_ref[0])
    def _():
        e_cur = te_ref[r]
        e_nxt = nxt_ref[r]

        @pl.when(r == 0)
        def _():
            st[0] = 1
            begin(e_cur)

        @pl.when(first_ref[r] == 1)
        def _():
            slot = 1 - st[0]
            done = st[1]
            for k in range(MOE_CHUNKS):
                @pl.when(k >= done)
                def _():
                    convert(e_cur, k, slot)
            st[0] = slot
            st[1] = MOE_CHUNKS

            @pl.when(e_nxt >= 0)
            def _():
                begin(e_nxt)

        cur = st[0]
        x = xs_ref[...].astype(BF16)
        hc = jnp.dot(x, wub[cur], preferred_element_type=F32)
        a = hc[:, :DFF]
        b = hc[:, DFF:]
        act = (a * _sigmoid(a) * b).astype(BF16)
        ys_ref[...] = jnp.dot(act, wdb[cur], preferred_element_type=F32)

        @pl.when(e_nxt >= 0)
        def _():
            done = st[1]
            share = (MOE_CHUNKS - done + rem_ref[r] - 1) // rem_ref[r]
            for k in range(MOE_CHUNKS):
                @pl.when(jnp.logical_and(k >= done, k < done + share))
                def _():
                    convert(e_nxt, k, 1 - cur)
            st[1] = done + share

    @pl.when(r >= nu_ref[0])
    def _():
        ys_ref[...] = jnp.zeros_like(ys_ref)


def _moe(plan, xs, wu, wd, layer):
    def row(r, te, nu, *_):
        return (jnp.minimum(r, nu[0] - 1), 0)

    return pl.pallas_call(
        functools.partial(_moe_body, layer=layer),
        out_shape=jax.ShapeDtypeStruct(xs.shape, F32),
        grid_spec=pltpu.PrefetchScalarGridSpec(
            num_scalar_prefetch=5,
            grid=(xs.shape[0] // TM_E,),
            in_specs=[
                pl.BlockSpec((TM_E, D), row),
                pl.BlockSpec(memory_space=pl.ANY),
                pl.BlockSpec(memory_space=pl.ANY),
            ],
            out_specs=pl.BlockSpec((TM_E, D), lambda r, *_: (r, 0)),
            scratch_shapes=[
                pltpu.VMEM((2, D, 2 * DFF), BF16),
                pltpu.VMEM((2, DFF, D), BF16),
                pltpu.VMEM((2, MOE_W_CHUNK, D), F32),
                pltpu.SemaphoreType.DMA((2,)),
                pltpu.SMEM((2,), I32),
            ],
        ),
        compiler_params=_cparams(("arbitrary",), 56),
        name="moe",
    )(*plan, xs, wu, wd)


def _combine_body(pos0_ref, pos1_ref, posn0_ref, posn1_ref, ys_ref, x_ref, w_ref, mod_ref, g_ref, nmod_ref,
                  *rest, final):
    tm = TM_TOK
    outs, (ybuf, sem) = rest[:-2], rest[-2:]
    i = pl.program_id(0)
    slot = i % 2

    def start_row(p0_ref, p1_ref, s, r):
        _row_copy(ys_ref, p0_ref[0, r], ybuf.at[s, 0], r, sem.at[s]).start()
        _row_copy(ys_ref, p1_ref[0, r], ybuf.at[s, 1], r, sem.at[s]).start()

    def drain(s):
        for k in range(2):
            pltpu.make_async_copy(ys_ref.at[pl.ds(0, tm)], ybuf.at[s, k], sem.at[s]).wait()

    @pl.when(i == 0)
    def _():
        _issue_rows(tm, functools.partial(start_row, pos0_ref, pos1_ref, 0))

    drain(slot)

    reps = D // LANES
    rows_per_trip = COMBINE_ROWS_PER_TRIP

    def trip(t, c):
        base = pl.multiple_of(t * rows_per_trip, rows_per_trip)
        rows = pl.ds(base, rows_per_trip)
        w0 = jnp.tile(w_ref[rows, 0:LANES], (1, reps))
        w1 = jnp.tile(w_ref[rows, LANES:2 * LANES], (1, reps))
        xn = x_ref[rows, :] + mod_ref[5:6, :] * (w0 * ybuf[slot, 0, rows, :] + w1 * ybuf[slot, 1, rows, :])
        if final:
            outs[0][rows, :] = _rms(xn, g_ref[...])
        else:
            outs[0][rows, :] = xn
            outs[1][rows, :] = (_rms(xn, g_ref[...]) * (1.0 + nmod_ref[1:2, :]) + nmod_ref[0:1, :]).astype(BF16)
        for u in range(rows_per_trip):
            start_row(posn0_ref, posn1_ref, 1 - slot, base + u)
        return c

    lax.fori_loop(0, tm // rows_per_trip, trip, 0)

    @pl.when(i == pl.num_programs(0) - 1)
    def _():
        drain(1 - slot)


def _combine(pos, ys, x, wlanes, mods, layer, n_rows, g_next, final):
    tm = TM_TOK
    nt = n_rows // tm
    next_layer = min(layer + 1, DEPTH - 1)
    tile = pl.BlockSpec((tm, D), lambda i: (i, 0))
    out_shape = [jax.ShapeDtypeStruct((n_rows, D), F32)]
    if not final:
        out_shape.append(jax.ShapeDtypeStruct((n_rows, D), BF16))
    return pl.pallas_call(
        functools.partial(_combine_body, final=final),
        out_shape=out_shape,
        grid=(nt,),
        in_specs=_pos_specs(tm, lambda i: i) + _pos_specs(tm, lambda i: jnp.minimum(i + 1, nt - 1)) + [
            pl.BlockSpec(memory_space=pl.ANY),
            tile,
            pl.BlockSpec((tm, 2 * LANES), lambda i: (i, 0)),
            pl.BlockSpec((None, None, 6, D), lambda i: (layer, _mod_row(i, tm), 0, 0)),
            pl.BlockSpec((1, D), lambda i: (0, 0)),
            pl.BlockSpec((None, None, 6, D), lambda i: (next_layer, _mod_row(i, tm), 0, 0)),
        ],
        out_specs=[tile] * len(out_shape),
        scratch_shapes=[pltpu.VMEM((2, 2, tm, D), F32), pltpu.SemaphoreType.DMA((2,))],
        compiler_params=_cparams(("arbitrary",), 40),
        name="combine",
    )(pos, pos, pos, pos, ys, x, wlanes, mods, g_next.reshape(1, D), mods)


def _route_plan(route, cnt, n_rows):
    counts = cnt[:, 0].astype(I32)
    padded = ((counts + TM_E - 1) // TM_E) * TM_E
    ends = jnp.cumsum(padded)
    offs = ends - padded
    e01 = route[0:2].astype(I32)
    eids = jnp.arange(NE, dtype=I32)[:, None, None]
    off01 = jnp.sum(jnp.where(e01[None] == eids, offs[:, None, None], 0), axis=0)
    pos = off01 + route[2:4].astype(I32)
    pos = pos.reshape(2, n_rows // TM_TOK, 1, TM_TOK)
    nt = _n_expert_tiles(n_rows)
    tile_start = jnp.arange(nt, dtype=I32) * TM_E
    tile_expert = jnp.minimum(jnp.sum((tile_start[:, None] >= ends[None, :]).astype(I32), axis=1), NE - 1)
    n_used = (ends[-1] // TM_E).reshape(1).astype(I32)
    spare = n_used[0] + jnp.arange(NE, dtype=I32)
    zstart = jnp.concatenate([offs + (counts // TM_E) * TM_E, jnp.minimum(spare, nt - 1) * TM_E])
    zneed = jnp.concatenate([counts % TM_E != 0, spare < nt]).astype(I32)
    wl = jnp.concatenate([jnp.broadcast_to(route[4][:, None], (n_rows, LANES)),
                          jnp.broadcast_to(route[5][:, None], (n_rows, LANES))], axis=1)
    ntile = padded // TM_E
    ecol = jnp.arange(NE, dtype=I32)
    onehot = tile_expert[:, None] == ecol[None, :]
    pick = lambda v: jnp.sum(jnp.where(onehot, v[None, :], 0), axis=1)
    j_in = jnp.arange(nt, dtype=I32) - pick(offs // TM_E)
    first = (j_in == 0).astype(I32)
    rem = jnp.maximum(pick(ntile) - j_in, 1)
    later = (ecol[None, :] > ecol[:, None]) & (ntile[None, :] > 0)
    nxt_e = jnp.min(jnp.where(later, ecol[None, :], NE), axis=1)
    nxt = pick(jnp.where(nxt_e < NE, nxt_e, -1))
    moe_plan = (tile_expert.astype(I32), n_used, first, nxt.astype(I32), rem.astype(I32))
    return pos, moe_plan, zstart.astype(I32), zneed, wl


def kernel(x, c, ctx, c_ctx, ada_w, ada_b, norm1_g, norm2_g, w_in, attn_sink, sgu_ln_g, sgu_ln_b, sgu_w, sgu_b,
           sconv_w, conf_dw_w, conf_dw_b, conf_ln_g, conf_ln_b, w_branch, w_out, router_w, router_b,
           exp_w_up, exp_w_down, final_g):
    cvec = jnp.concatenate([c, c_ctx[None, :], jnp.zeros((8 - NB - 1, D), F32)], axis=0)
    mods = _ada(cvec, ada_w, ada_b).reshape(DEPTH, 8, 6, D)
    xa, cxa = x.reshape(TX, D), ctx.reshape(TC, D)
    cos, sin = _rope_tables()
    consts = _rope_constants()
    rw = router_w.T.astype(BF16)
    wb_all = w_branch.astype(BF16)
    wo_all = w_out.astype(BF16)

    for l in range(DEPTH):
        last = l == DEPTH - 1
        n_rows = TX if last else T
        n_x_tiles = TX // TM_TOK
        lp = (sgu_ln_g[l], sgu_ln_b[l], sgu_w[l], sgu_b[l], sconv_w[l], conf_dw_w[l], conf_dw_b[l],
              conf_ln_g[l], conf_ln_b[l])
        c_rows0 = 0 if l == 0 else TX

        if l == 0:
            h = _norm_mod(xa, cxa, norm1_g[l], mods, l)
        if not last:
            p = _inproj(h, w_in, l, T, N_IN // TN_IN)
            qr, kr, vr = _prep(p, T, n_x_tiles, cos, sin, consts)
            ya = _attn(qr, kr, vr, kr, vr, TX // LC, attn_sink[l])
            yac = _ctx_attn(qr, kr, vr, attn_sink[l])
        else:
            p = _inproj(h, w_in, l, TX, N_IN // TN_IN)
            pc = _inproj(h[TX:], w_in, l, TC, 1)
            qr, kr, vr = _prep(p, TX, n_x_tiles, cos, sin, consts)
            _, kc, vc = _prep(pc, TC, 0, cos, sin, consts)
            ya = _attn(qr, kr, vr, kc, vc, 0, attn_sink[l])
            yac = ya
        yr = _mixers(p, n_rows, n_x_tiles, lp)
        xn, h2, lg = _post(p, ya, yac, yr, wb_all, wo_all, xa, cxa, c_rows0 // TM_TOK, mods, l,
                           norm2_g[l], rw, n_rows)
        route, cnt = _route(lg, router_b, n_rows)
        pos, moe_plan, zstart, zneed, wl = _route_plan(route, cnt, n_rows)
        xs = _dispatch(zstart, zneed, pos, h2, n_rows)
        ys = _moe(moe_plan, xs, exp_w_up, exp_w_down, l)
        if last:
            (out,) = _combine(pos, ys, xn, wl, mods, l, n_rows, final_g, True)
        else:
            xa, h = _combine(pos, ys, xn, wl, mods, l, n_rows, norm1_g[l + 1], False)
            cxa = xa

    return out.reshape(NB, S, D)
```

```python
import functools

import numpy as np
import jax
import jax.numpy as jnp
from jax import lax
from jax.experimental import pallas as pl
from jax.experimental.pallas import tpu as pltpu

F32 = jnp.float32
BF16 = jnp.bfloat16
I32 = jnp.int32

D = 2048
NB = 4
S = 2048
LC = 256
DEPTH = 2
GRID_W = 64
BW = 512
HD = 64
NH = 8
NKV = 2
REP = NH // NKV
WINDOW = 128
QB = 128
ROPE_THETA = 10000.0
SGU_CHUNK = 128
SGU_GROUPS = 4
SCONV_K = 3
CONF_K = 31
NE = 16
NG = 4
EPG = NE // NG
DFF = D // 2
N_IN = BW + 2 * NKV * HD + 7 * BW + 4 * D
TX = NB * S
TC = NB * LC
T = TX + TC

OFF_Q, OFF_KV = 0, 512
OFF_SU, OFF_SV, OFF_CB, OFF_CC, OFF_CX, OFF_GA, OFF_GB, OFF_GATES = 768, 1280, 1792, 2304, 2816, 3328, 3840, 4352

LANES = 128
SUBLANES = 8
V7X_VMEM_BYTES = 64 * 1024 * 1024
MIB = 1024 * 1024

TM_NORM = 512
TM_IN = 1024
INPROJ_ROW_CHUNK = 512
IN_W_CHUNK = 256
IN_CHUNKS = D // IN_W_CHUNK
TN_IN = 1792
TM_TOK = 256
TM_POST = 512
POST_ROW_CHUNK = 256
TM_E = 256
HALO = 16
ROW_UNROLL = 8
COMBINE_ROWS_PER_TRIP = 32
MOE_W_CHUNK = 512
MOE_UP_CHUNKS = D // MOE_W_CHUNK
MOE_CHUNKS = MOE_UP_CHUNKS + DFF // MOE_W_CHUNK


def _n_expert_tiles(n_rows):
    return (2 * n_rows) // TM_E + NE


def _cparams(sem, vmem_mib):
    return pltpu.CompilerParams(dimension_semantics=sem, vmem_limit_bytes=vmem_mib * MIB)


def _sigmoid(x):
    return 0.5 * jnp.tanh(0.5 * x) + 0.5


def _gelu_tanh(x):
    c = np.float32(np.sqrt(2.0 / np.pi))
    return 0.5 * x * (1.0 + jnp.tanh(c * (x + np.float32(0.044715) * (x * x * x))))


def _layernorm(x, g, b, eps=1e-5):
    mu = jnp.mean(x, axis=-1, keepdims=True)
    xc = x - mu
    var = jnp.mean(xc * xc, axis=-1, keepdims=True)
    return xc * lax.rsqrt(var + eps) * g + b


def _mod_row(i, tm):
    return jnp.where(i < TX // tm, (i * tm) // S, NB)


def _ada_body(c_ref, w_ref, b_ref, o_ref):
    c = c_ref[...]
    s = (c * _sigmoid(c)).astype(BF16)
    o_ref[...] = jnp.dot(s, w_ref[...].astype(BF16), preferred_element_type=F32) + b_ref[...]


def _ada(cvec, ada_w, ada_b):
    tn = 1024
    return pl.pallas_call(
        _ada_body,
        out_shape=jax.ShapeDtypeStruct((DEPTH, 8, 6 * D), F32),
        grid=(DEPTH, 6 * D // tn),
        in_specs=[
            pl.BlockSpec((8, D), lambda l, j: (0, 0)),
            pl.BlockSpec((None, D, tn), lambda l, j: (l, 0, j)),
            pl.BlockSpec((None, 1, tn), lambda l, j: (l, 0, j)),
        ],
        out_specs=pl.BlockSpec((None, 8, tn), lambda l, j: (l, 0, j)),
        compiler_params=_cparams(("arbitrary", "arbitrary"), 40),
        name="ada",
    )(cvec, ada_w, ada_b.reshape(DEPTH, 1, 6 * D))


def _stream_specs(tm, c_blk0, width=D):
    nx = TX // tm
    return [pl.BlockSpec((tm, width), lambda i, *_: (jnp.minimum(i, nx - 1), 0)),
            pl.BlockSpec((tm, width), lambda i, *_: (c_blk0 + jnp.maximum(i - nx, 0), 0))]


def _stream_tile(x_ref, c_ref, tm):
    return jnp.where(pl.program_id(0) < TX // tm, x_ref[...], c_ref[...])


def _rms(x, g):
    return x * lax.rsqrt(jnp.mean(x * x, axis=-1, keepdims=True) + 1e-6) * g


def _norm_mod_body(x_ref, c_ref, g_ref, mod_ref, o_ref):
    y = _rms(_stream_tile(x_ref, c_ref, TM_NORM), g_ref[...])
    o_ref[...] = (y * (1.0 + mod_ref[1:2, :]) + mod_ref[0:1, :]).astype(o_ref.dtype)


def _norm_mod(x, cx, g, mods, layer):
    tm = TM_NORM
    return pl.pallas_call(
        _norm_mod_body,
        out_shape=jax.ShapeDtypeStruct((T, D), BF16),
        grid=(T // tm,),
        in_specs=_stream_specs(tm, 0) + [
            pl.BlockSpec((1, D), lambda i: (0, 0)),
            pl.BlockSpec((None, None, 6, D), lambda i: (layer, _mod_row(i, tm), 0, 0)),
        ],
        out_specs=pl.BlockSpec((tm, D), lambda i: (i, 0)),
        compiler_params=_cparams(("arbitrary",), 32),
        name="norm_mod",
    )(x, cx, g.reshape(1, D), mods)


def _inproj_body(h_ref, w_hbm, o_ref, wbf, stage, sem, st, *, layer):
    j = pl.program_id(0)
    i = pl.program_id(1)
    nj = pl.num_programs(0)
    ni = pl.num_programs(1)

    def chunk_copy(jj, k):
        cols = pl.ds(pl.multiple_of(jj * TN_IN, LANES), TN_IN)
        src = w_hbm.at[layer, pl.ds(k * IN_W_CHUNK, IN_W_CHUNK), cols]
        return pltpu.make_async_copy(src, stage.at[k % 2], sem.at[k % 2])

    def convert(jj, k, slot):
        chunk_copy(jj, k).wait()
        wbf[slot, k * IN_W_CHUNK:(k + 1) * IN_W_CHUNK, :] = stage[k % 2].astype(BF16)
        if k + 2 < IN_CHUNKS:
            chunk_copy(jj, k + 2).start()

    def begin(jj):
        chunk_copy(jj, 0).start()
        chunk_copy(jj, 1).start()
        st[1] = 0

    @pl.when(jnp.logical_and(i == 0, j == 0))
    def _():
        st[0] = 1
        begin(0)

    @pl.when(i == 0)
    def _():
        slot = 1 - st[0]
        done = st[1]
        for k in range(IN_CHUNKS):
            @pl.when(k >= done)
            def _():
                convert(j, k, slot)
        st[0] = slot
        st[1] = IN_CHUNKS

        @pl.when(j + 1 < nj)
        def _():
            begin(j + 1)

    cur = st[0]
    col0 = j * TN_IN
    rows = INPROJ_ROW_CHUNK

    def run(epilogue):
        for c in range(h_ref.shape[0] // rows):
            sl = slice(c * rows, (c + 1) * rows)
            acc = jnp.dot(h_ref[sl, :], wbf[cur], preferred_element_type=F32)
            o_ref[sl, :] = epilogue(acc).astype(BF16)

    @pl.when(col0 + TN_IN <= OFF_GB)
    def _():
        run(lambda acc: acc)

    @pl.when(col0 >= OFF_GB)
    def _():
        run(_sigmoid)

    @pl.when(jnp.logical_and(col0 < OFF_GB, col0 + TN_IN > OFF_GB))
    def _():
        col = col0 + lax.broadcasted_iota(I32, (rows, TN_IN), 1)
        run(lambda acc: jnp.where(col >= OFF_GB, _sigmoid(acc), acc))

    @pl.when(j + 1 < nj)
    def _():
        done = st[1]
        left = ni - i
        share = (IN_CHUNKS - done + left - 1) // left
        for k in range(IN_CHUNKS):
            @pl.when(jnp.logical_and(k >= done, k < done + share))
            def _():
                convert(j + 1, k, 1 - cur)
        st[1] = done + share


def _inproj(h, w_in, layer, n_rows, n_col_tiles):
    tm = TM_IN
    return pl.pallas_call(
        functools.partial(_inproj_body, layer=layer),
        out_shape=jax.ShapeDtypeStruct((n_rows, n_col_tiles * TN_IN), BF16),
        grid=(n_col_tiles, n_rows // tm),
        in_specs=[
            pl.BlockSpec((tm, D), lambda j, i: (i, 0)),
            pl.BlockSpec(memory_space=pl.ANY),
        ],
        out_specs=pl.BlockSpec((tm, TN_IN), lambda j, i: (i, j)),
        scratch_shapes=[
            pltpu.VMEM((2, D, TN_IN), BF16),
            pltpu.VMEM((2, IN_W_CHUNK, TN_IN), F32),
            pltpu.SemaphoreType.DMA((2,)),
            pltpu.SMEM((2,), I32),
        ],
        compiler_params=_cparams(("arbitrary", "arbitrary"), 56),
        name="inproj",
    )(h, w_in)


def _rope_constants():
    rh = np.zeros((HD, HD), np.float32)
    for base in (0, 32):
        for d in range(16):
            rh[base + 16 + d, base + d] = -1.0
            rh[base + d, base + 16 + d] = 1.0
    rq = np.kron(np.eye(NH, dtype=np.float32), rh)
    rk = np.kron(np.eye(NKV, dtype=np.float32), rh)
    rep = np.zeros((NKV * HD, NH * HD), np.float32)
    for g in range(NKV):
        for r in range(REP):
            for d in range(HD):
                rep[g * HD + d, g * REP * HD + r * HD + d] = 1.0
    return jnp.asarray(rq, BF16), jnp.asarray(rep, BF16), jnp.asarray(rk @ rep, BF16)


def _rope_tables():
    half = HD // 2
    inv = 1.0 / (ROPE_THETA ** (jnp.arange(0, half, 2, dtype=F32) / half))
    pos = jnp.arange(S)
    ar = (pos // GRID_W).astype(F32)[:, None] * inv
    ac = (pos % GRID_W).astype(F32)[:, None] * inv
    cos = jnp.tile(jnp.concatenate([jnp.cos(ar), jnp.cos(ar), jnp.cos(ac), jnp.cos(ac)], axis=1), (1, NH))
    sin = jnp.tile(jnp.concatenate([jnp.sin(ar), jnp.sin(ar), jnp.sin(ac), jnp.sin(ac)], axis=1), (1, NH))
    cos = jnp.concatenate([cos, jnp.ones((TM_TOK, NH * HD), F32)], axis=0)
    sin = jnp.concatenate([sin, jnp.zeros((TM_TOK, NH * HD), F32)], axis=0)
    return cos, sin


def _prep_body(q_ref, kv_ref, cos_ref, sin_ref, rq_ref, rep_ref, rrep_ref, qo_ref, ko_ref, vo_ref):
    cos = cos_ref[...]
    sin = sin_ref[...]
    q = q_ref[...]
    qs = jnp.dot(q, rq_ref[...], preferred_element_type=F32)
    qo_ref[...] = ((q.astype(F32) * cos + qs * sin) * (HD ** -0.5)).astype(BF16)
    k = kv_ref[:, 0:NKV * HD]
    v = kv_ref[:, NKV * HD:2 * NKV * HD]
    kr = jnp.dot(k, rep_ref[...], preferred_element_type=F32)
    ks = jnp.dot(k, rrep_ref[...], preferred_element_type=F32)
    ko_ref[...] = (kr * cos + ks * sin).astype(BF16)
    vo_ref[...] = jnp.dot(v, rep_ref[...], preferred_element_type=F32).astype(BF16)


def _prep(p, n_rows, n_x_tiles, cos, sin, consts):
    tm = TM_TOK
    rq, rep, rrep = consts
    tps = S // tm

    def tab(i):
        return (jnp.where(i < n_x_tiles, i % tps, tps), 0)

    w = NH * HD
    return pl.pallas_call(
        _prep_body,
        out_shape=[jax.ShapeDtypeStruct((n_rows, w), BF16)] * 3,
        grid=(n_rows // tm,),
        in_specs=[
            pl.BlockSpec((tm, w), lambda i: (i, 0)),
            pl.BlockSpec((tm, 2 * NKV * HD), lambda i: (i, OFF_KV // (2 * NKV * HD))),
            pl.BlockSpec((tm, w), tab),
            pl.BlockSpec((tm, w), tab),
            pl.BlockSpec((w, w), lambda i: (0, 0)),
            pl.BlockSpec((NKV * HD, w), lambda i: (0, 0)),
            pl.BlockSpec((NKV * HD, w), lambda i: (0, 0)),
        ],
        out_specs=[pl.BlockSpec((tm, w), lambda i: (i, 0))] * 3,
        compiler_params=_cparams(("arbitrary",), 32),
        name="prep",
    )(p, p, cos, sin, rq, rep, rrep)


def _attn_groups(q, kbs, vbs, sink_ref, valid, nq):
    gw = REP * HD
    lane_head = lax.broadcasted_iota(I32, (nq, gw), 1) // HD
    row_head = lax.broadcasted_iota(I32, (REP * nq, 1), 0) // nq
    valid_rep = None if valid is None else jnp.concatenate([valid] * REP, axis=0)
    scores = []
    for g in range(NKV):
        qg = q[:, g * gw:(g + 1) * gw]
        qs = jnp.concatenate([jnp.where(lane_head == r, qg, jnp.zeros_like(qg)) for r in range(REP)], axis=0)
        s = lax.dot_general(qs, kbs[g], (((1,), (1,)), ((), ())), preferred_element_type=F32)
        scores.append(s if valid_rep is None else jnp.where(valid_rep, s, -jnp.inf))
    probs = []
    for g in range(NKV):
        s = scores[g]
        sink = jnp.zeros((REP * nq, 1), F32)
        for r in range(REP):
            sink = jnp.where(row_head == r, sink_ref[g * REP + r], sink)
        m = jnp.maximum(jnp.max(s, axis=-1, keepdims=True), sink)
        e = jnp.exp(s - m)
        den = jnp.sum(e, axis=-1, keepdims=True) + jnp.exp(sink - m)
        probs.append((e * (1.0 / den)).astype(BF16))
    outs = []
    for g in range(NKV):
        o = jnp.dot(probs[g], vbs[g], preferred_element_type=F32)
        og = jnp.zeros((nq, gw), F32)
        for r in range(REP):
            og = og + jnp.where(lane_head == r, o[r * nq:(r + 1) * nq, :], 0.0)
        outs.append(og)
    return jnp.concatenate(outs, axis=1).astype(BF16)


def _attn_body(sink_ref, q_ref, kp_ref, kc_ref, kn_ref, vp_ref, vc_ref, vn_ref, kx_ref, vx_ref, o_ref):
    n = pl.program_id(1)
    nblk = S // QB
    nk = 3 * QB + LC
    row = lax.broadcasted_iota(I32, (QB, nk), 0)
    col = lax.broadcasted_iota(I32, (QB, nk), 1)
    lo = jnp.where(n == 0, QB, 0)
    hi = jnp.where(n == nblk - 1, 2 * QB, 3 * QB)
    band = (col >= row) & (col <= row + 2 * WINDOW) & (col >= lo) & (col < hi)
    valid = band | (col >= 3 * QB)
    gw = REP * HD
    kbs, vbs = [], []
    for g in range(NKV):
        sl = slice(g * gw, (g + 1) * gw)
        kbs.append(jnp.concatenate([kp_ref[:, sl], kc_ref[:, sl], kn_ref[:, sl], kx_ref[:, sl]], axis=0))
        vbs.append(jnp.concatenate([vp_ref[:, sl], vc_ref[:, sl], vn_ref[:, sl], vx_ref[:, sl]], axis=0))
    o_ref[...] = _attn_groups(q_ref[...], kbs, vbs, sink_ref, valid, QB)


def _attn(qr, kr, vr, kc_arr, vc_arr, ctx_blk0, sink):
    nblk = S // QB
    w = NH * HD

    def cur(b, n):
        return (b * nblk + n, 0)

    def prev(b, n):
        return (b * nblk + jnp.maximum(n - 1, 0), 0)

    def nxt(b, n):
        return (b * nblk + jnp.minimum(n + 1, nblk - 1), 0)

    def cx(b, n):
        return (ctx_blk0 + b, 0)

    blk = lambda f: pl.BlockSpec((QB, w), f)
    return pl.pallas_call(
        _attn_body,
        out_shape=jax.ShapeDtypeStruct((TX, w), BF16),
        grid=(NB, nblk),
        in_specs=[
            pl.BlockSpec(memory_space=pltpu.SMEM),
            blk(cur), blk(prev), blk(cur), blk(nxt), blk(prev), blk(cur), blk(nxt),
            pl.BlockSpec((LC, w), cx), pl.BlockSpec((LC, w), cx),
        ],
        out_specs=blk(cur),
        compiler_params=_cparams(("arbitrary", "arbitrary"), 32),
        name="window_attn",
    )(sink, qr, kr, kr, kr, vr, vr, vr, kc_arr, vc_arr)


def _ctx_attn_body(sink_ref, q_ref, kx_ref, vx_ref, o_ref):
    gw = REP * HD
    kbs = [kx_ref[:, g * gw:(g + 1) * gw] for g in range(NKV)]
    vbs = [vx_ref[:, g * gw:(g + 1) * gw] for g in range(NKV)]
    o_ref[...] = _attn_groups(q_ref[...], kbs, vbs, sink_ref, None, LC)


def _ctx_attn(qr, kr, vr, sink):
    w = NH * HD
    blk0 = TX // LC
    spec = pl.BlockSpec((LC, w), lambda b: (blk0 + b, 0))
    return pl.pallas_call(
        _ctx_attn_body,
        out_shape=jax.ShapeDtypeStruct((TC, w), BF16),
        grid=(NB,),
        in_specs=[pl.BlockSpec(memory_space=pltpu.SMEM), spec, spec, spec],
        out_specs=pl.BlockSpec((LC, w), lambda b: (b, 0)),
        compiler_params=_cparams(("arbitrary",), 32),
        name="ctx_attn",
    )(sink, qr, kr, vr)


def _mix_body(su_ref, sv_ref, cb_ref, cc_ref, cx_ref, ga_ref, gb_ref,
              ccp_ref, cxp_ref, gap_ref, gbp_ref, ccn_ref, cxn_ref, gan_ref, gbn_ref,
              lng_ref, lnb_ref, sw_ref, sbias_ref, scw_ref, dww_ref, dwb_ref, clg_ref, clb_ref,
              o_ref, m_scr, z_scr, *, n_x_tiles):
    tm = TM_TOK
    i = pl.program_id(0)
    tps = S // tm
    is_x = i < n_x_tiles
    first = jnp.logical_or(jnp.logical_not(is_x), (i % tps) == 0)
    last = jnp.logical_or(jnp.logical_not(is_x), (i % tps) == tps - 1)
    keep_p = jnp.where(first, 0.0, 1.0)
    keep_n = jnp.where(last, 0.0, 1.0)

    u = _gelu_tanh(su_ref[...].astype(F32))
    v = _layernorm(_gelu_tanh(sv_ref[...].astype(F32)), lng_ref[...], lnb_ref[...]).astype(BF16)
    gwid = BW // SGU_GROUPS
    chunks = []
    for c in range(tm // SGU_CHUNK):
        parts = []
        for g in range(SGU_GROUPS):
            vc = v[c * SGU_CHUNK:(c + 1) * SGU_CHUNK, g * gwid:(g + 1) * gwid]
            parts.append(jnp.dot(sw_ref[g], vc, preferred_element_type=F32))
        chunks.append(jnp.concatenate(parts, axis=1) + sbias_ref[...])
    yb = u * jnp.concatenate(chunks, axis=0)

    m_scr[HALO:HALO + tm, :] = cc_ref[...].astype(F32) * cx_ref[...].astype(F32)
    m_scr[0:HALO, :] = ccp_ref[...].astype(F32) * cxp_ref[...].astype(F32) * keep_p
    m_scr[HALO + tm:, :] = ccn_ref[...].astype(F32) * cxn_ref[...].astype(F32) * keep_n
    z = jnp.zeros((tm, BW), F32)
    for k in range(SCONV_K):
        z = z + scw_ref[k:k + 1, :] * m_scr[pl.ds(HALO - SCONV_K // 2 + k, tm), :]
    yc = cb_ref[...].astype(F32) * z

    z_scr[0, HALO:HALO + tm, :] = ga_ref[...].astype(F32) * gb_ref[...].astype(F32)
    z_scr[0, 0:HALO, :] = gap_ref[...].astype(F32) * gbp_ref[...].astype(F32) * keep_p
    z_scr[0, HALO + tm:, :] = gan_ref[...].astype(F32) * gbn_ref[...].astype(F32) * keep_n
    n_sh = tm + 2 * HALO - SUBLANES
    for s in range(1, SUBLANES):
        z_scr[s, 0:n_sh, :] = z_scr[0, pl.ds(s, n_sh), :]
    acc = jnp.zeros((tm, BW), F32) + dwb_ref[...]
    for k in range(CONF_K):
        off = HALO - CONF_K // 2 + k
        base = off - off % SUBLANES
        acc = acc + dww_ref[k:k + 1, :] * z_scr[off % SUBLANES, base:base + tm, :]
    zn = _layernorm(acc, clg_ref[...], clb_ref[...])
    yd = zn * _sigmoid(zn)

    o_ref[...] = jnp.concatenate([yb, yc, yd], axis=1).astype(BF16)


def _mixers(p, n_rows, n_x_tiles, lp):
    tm = TM_TOK
    (sgu_ln_g, sgu_ln_b, sgu_w, sgu_b, sconv_w, conf_dw_w, conf_dw_b, conf_ln_g, conf_ln_b) = lp

    def sec(off):
        return pl.BlockSpec((pl.Element(tm), pl.Element(BW)), lambda i: (i * tm, off))

    hpt = tm // HALO

    def halo_prev(off):
        return pl.BlockSpec((pl.Element(HALO), pl.Element(BW)),
                            lambda i: (jnp.maximum(i * hpt - 1, 0) * HALO, off))

    def halo_next(off):
        return pl.BlockSpec((pl.Element(HALO), pl.Element(BW)),
                            lambda i: (jnp.minimum(i * hpt + hpt, n_rows // HALO - 1) * HALO, off))

    def const(shape):
        return pl.BlockSpec(shape, lambda i: (0,) * len(shape))

    sbias = jnp.repeat(sgu_b.T, BW // SGU_GROUPS, axis=1)
    row = lambda a: a.reshape(1, BW)
    conv_offs = (OFF_CC, OFF_CX, OFF_GA, OFF_GB)
    return pl.pallas_call(
        functools.partial(_mix_body, n_x_tiles=n_x_tiles),
        out_shape=jax.ShapeDtypeStruct((n_rows, 3 * BW), BF16),
        grid=(n_rows // tm,),
        in_specs=[sec(o) for o in (OFF_SU, OFF_SV, OFF_CB, OFF_CC, OFF_CX, OFF_GA, OFF_GB)]
        + [halo_prev(o) for o in conv_offs] + [halo_next(o) for o in conv_offs]
        + [const((1, BW)), const((1, BW)), const((SGU_GROUPS, SGU_CHUNK, SGU_CHUNK)), const((SGU_CHUNK, BW)),
           const((SCONV_K, BW)), const((CONF_K, BW)), const((1, BW)), const((1, BW)), const((1, BW))],
        out_specs=pl.BlockSpec((tm, 3 * BW), lambda i: (i, 0)),
        scratch_shapes=[pltpu.VMEM((tm + 2 * HALO, BW), F32), pltpu.VMEM((SUBLANES, tm + 2 * HALO, BW), F32)],
        compiler_params=_cparams(("arbitrary",), 32),
        name="mixers",
    )(*([p] * 15), row(sgu_ln_g), row(sgu_ln_b), sgu_w.astype(BF16), sbias, sconv_w, conf_dw_w,
      row(conf_dw_b), row(conf_ln_g), row(conf_ln_b))


def _norm2(xn, n2g, mod):
    return _rms(xn, n2g) * (1.0 + mod[4:5, :]) + mod[3:4, :]


def _post_body(g_ref, ya_ref, yac_ref, yr_ref, wb_ref, wo_ref, x_ref, cx_ref, mod_ref, n2g_ref, rw_ref,
               xo_ref, h2_ref, lg_ref):
    tm = TM_TOK
    merged = None
    for br in range(4):
        y = _stream_tile(ya_ref, yac_ref, tm) if br == 0 else yr_ref[:, (br - 1) * BW:br * BW]
        pr = jnp.dot(y, wb_ref[br], preferred_element_type=F32)
        term = g_ref[:, br * D:(br + 1) * D].astype(F32) * pr
        merged = term if merged is None else merged + term
    out = jnp.dot(merged.astype(BF16), wo_ref[...], preferred_element_type=F32)
    xn = _stream_tile(x_ref, cx_ref, tm) + mod_ref[2:3, :] * out
    xo_ref[...] = xn
    h2 = _norm2(xn, n2g_ref[...], mod_ref)
    h2_ref[...] = h2
    nt = (((1,), (1,)), ((), ()))
    lg_ref[...] = lax.dot_general(rw_ref[...], h2.astype(BF16), nt, preferred_element_type=F32)


def _post(p, ya, yac, yr, wb, wo, x, cx, c_blk0, mods, layer, n2g, rw, n_rows):
    tm = TM_TOK
    one = pl.Buffered(1)
    return pl.pallas_call(
        _post_body,
        out_shape=[jax.ShapeDtypeStruct((n_rows, D), F32), jax.ShapeDtypeStruct((n_rows, D), F32),
                   jax.ShapeDtypeStruct((NE, n_rows), F32)],
        grid=(n_rows // tm,),
        in_specs=[pl.BlockSpec((pl.Element(tm), pl.Element(4 * D)), lambda i: (i * tm, OFF_GATES))]
        + _stream_specs(tm, 0, BW) + [
            pl.BlockSpec((tm, 3 * BW), lambda i: (i, 0)),
            pl.BlockSpec((None, 4, BW, D), lambda i: (layer, 0, 0, 0), pipeline_mode=one),
            pl.BlockSpec((None, D, D), lambda i: (layer, 0, 0), pipeline_mode=one)]
        + _stream_specs(tm, c_blk0) + [
            pl.BlockSpec((None, None, 6, D), lambda i: (layer, _mod_row(i, tm), 0, 0)),
            pl.BlockSpec((1, D), lambda i: (0, 0)),
            pl.BlockSpec((NE, D), lambda i: (0, 0)),
        ],
        out_specs=[pl.BlockSpec((tm, D), lambda i: (i, 0)), pl.BlockSpec((tm, D), lambda i: (i, 0)),
                   pl.BlockSpec((NE, tm), lambda i: (0, i))],
        compiler_params=_cparams(("arbitrary",), 52),
        name="post",
    )(p, ya, yac, yr, wb, wo, x, cx, mods, n2g.reshape(1, D), rw)


def _route_body(lg_ref, rb_ref, tri_ref, o_ref, cnt_ref, carry_ref):
    tm = TM_TOK
    i = pl.program_id(0)

    @pl.when(i == 0)
    def _():
        carry_ref[...] = jnp.zeros_like(carry_ref)

    lg = lg_ref[...]
    e = jnp.exp(lg - jnp.max(lg, axis=0, keepdims=True))
    sc = e / jnp.sum(e, axis=0, keepdims=True)
    bi = sc + rb_ref[...]
    b = [bi[k:k + 1, :] for k in range(NE)]
    s = [sc[k:k + 1, :] for k in range(NE)]

    gs = []
    for g in range(NG):
        v = b[g * EPG:(g + 1) * EPG]
        best = None
        for a in range(EPG):
            for c in range(a + 1, EPG):
                ps = v[a] + v[c]
                best = ps if best is None else jnp.maximum(best, ps)
        gs.append(best)
    gsel = jnp.zeros((1, tm), I32)
    gbest = gs[0]
    for g in range(1, NG):
        take = gs[g] > gbest
        gsel = jnp.where(take, g, gsel)
        gbest = jnp.where(take, gs[g], gbest)

    vb, vs = [], []
    for j in range(EPG):
        xb, xs = b[j], s[j]
        for g in range(1, NG):
            xb = jnp.where(gsel == g, b[g * EPG + j], xb)
            xs = jnp.where(gsel == g, s[g * EPG + j], xs)
        vb.append(xb)
        vs.append(xs)
    order = []
    for j in range(EPG):
        c = jnp.zeros((1, tm), I32)
        for m in range(EPG):
            if m == j:
                continue
            ahead = (vb[m] >= vb[j]) if m < j else (vb[m] > vb[j])
            c = c + jnp.where(ahead, 1, 0)
        order.append(c)
    zero = jnp.zeros((1, tm), F32)
    w0 = zero
    w1 = zero
    j0 = jnp.zeros((1, tm), I32)
    j1 = jnp.zeros((1, tm), I32)
    for j in range(EPG):
        w0 = jnp.where(order[j] == 0, vs[j], w0)
        w1 = jnp.where(order[j] == 1, vs[j], w1)
        j0 = jnp.where(order[j] == 0, j, j0)
        j1 = jnp.where(order[j] == 1, j, j1)
    tot = w0 + w1
    e0 = gsel * EPG + j0
    e1 = gsel * EPG + j1

    eid = lax.broadcasted_iota(I32, (NE, tm), 0)
    oh0 = eid == e0
    oh1 = eid == e1
    oh = jnp.where(oh0 | oh1, 1.0, 0.0)
    rank = jnp.dot(oh.astype(BF16), tri_ref[...], preferred_element_type=F32) + carry_ref[:, 0:1]
    r0 = jnp.sum(jnp.where(oh0, rank, 0.0), axis=0, keepdims=True)
    r1 = jnp.sum(jnp.where(oh1, rank, 0.0), axis=0, keepdims=True)
    new_carry = carry_ref[...] + jnp.sum(oh, axis=1, keepdims=True)
    carry_ref[...] = new_carry
    cnt_ref[...] = new_carry

    o_ref[0:1, :] = e0.astype(F32)
    o_ref[1:2, :] = e1.astype(F32)
    o_ref[2:3, :] = r0
    o_ref[3:4, :] = r1
    o_ref[4:5, :] = w0 / tot
    o_ref[5:6, :] = w1 / tot
    o_ref[6:8, :] = jnp.zeros((2, tm), F32)


def _route(lg, router_b, n_rows):
    tm = TM_TOK
    tri = jnp.asarray(np.triu(np.ones((tm, tm), np.float32), 1), BF16)
    return pl.pallas_call(
        _route_body,
        out_shape=[jax.ShapeDtypeStruct((8, n_rows), F32), jax.ShapeDtypeStruct((NE, LANES), F32)],
        grid=(n_rows // tm,),
        in_specs=[
            pl.BlockSpec((NE, tm), lambda i: (0, i)),
            pl.BlockSpec((NE, 1), lambda i: (0, 0)),
            pl.BlockSpec((tm, tm), lambda i: (0, 0)),
        ],
        out_specs=[pl.BlockSpec((8, tm), lambda i: (0, i)), pl.BlockSpec((NE, LANES), lambda i: (0, 0))],
        scratch_shapes=[pltpu.VMEM((NE, LANES), F32)],
        compiler_params=_cparams(("arbitrary",), 32),
        name="route",
    )(lg, router_b.reshape(NE, 1), tri)


def _row_copy(src_ref, src_row, dst_ref, dst_row, sem):
    return pltpu.make_async_copy(src_ref.at[pl.ds(src_row, 1)], dst_ref.at[pl.ds(dst_row, 1)], sem)


def _issue_rows(n, start_row):
    def trip(t, c):
        base = pl.multiple_of(t * ROW_UNROLL, ROW_UNROLL)
        for u in range(ROW_UNROLL):
            start_row(base + u)
        return c

    lax.fori_loop(0, n // ROW_UNROLL, trip, 0)


def _dispatch_body(tail_ref, need_ref, pos0_ref, pos1_ref, h_ref, xs_ref, zero_ref, sem, zsem):
    tm = TM_TOK

    @pl.when(pl.program_id(0) == 0)
    def _():
        zero_ref[...] = jnp.zeros_like(zero_ref)

        def tail_copy(e):
            start = pl.multiple_of(tail_ref[e], TM_E)
            return pltpu.make_async_copy(zero_ref, xs_ref.at[pl.ds(start, TM_E)], zsem)

        for e in range(2 * NE):
            @pl.when(need_ref[e] > 0)
            def _():
                tail_copy(e).start()
        for e in range(2 * NE):
            @pl.when(need_ref[e] > 0)
            def _():
                tail_copy(e).wait()

    def start_row(r):
        _row_copy(h_ref, r, xs_ref, pos0_ref[0, r], sem).start(priority=0)
        _row_copy(h_ref, r, xs_ref, pos1_ref[0, r], sem).start(priority=1)

    _issue_rows(tm, start_row)
    for _ in range(2):
        pltpu.make_async_copy(h_ref, xs_ref.at[pl.ds(0, tm)], sem).wait()


def _pos_specs(tm, index):
    return [pl.BlockSpec((None, None, 1, tm), lambda i, *_, k=k: (k, index(i), 0, 0), memory_space=pltpu.SMEM)
            for k in range(2)]


def _dispatch(tail, need, pos, h2, n_rows):
    tm = TM_TOK
    return pl.pallas_call(
        _dispatch_body,
        out_shape=jax.ShapeDtypeStruct((_n_expert_tiles(n_rows) * TM_E, D), F32),
        grid_spec=pltpu.PrefetchScalarGridSpec(
            num_scalar_prefetch=2,
            grid=(n_rows // tm,),
            in_specs=_pos_specs(tm, lambda i: i) + [pl.BlockSpec((tm, D), lambda i, t, n: (i, 0))],
            out_specs=pl.BlockSpec(memory_space=pl.ANY),
            scratch_shapes=[pltpu.VMEM((TM_E, D), F32), pltpu.SemaphoreType.DMA(()), pltpu.SemaphoreType.DMA(())],
        ),
        compiler_params=_cparams(("arbitrary",), 32),
        name="dispatch",
    )(tail, need, pos, pos, h2)


def _moe_body(te_ref, nu_ref, first_ref, nxt_ref, rem_ref, xs_ref, wu_hbm, wd_hbm, ys_ref,
              wub, wdb, stage, sem, st, *, layer):
    r = pl.program_id(0)

    def chunk_copy(e, k):
        if k < MOE_UP_CHUNKS:
            src = wu_hbm.at[layer, e, pl.ds(k * MOE_W_CHUNK, MOE_W_CHUNK), :]
        else:
            src = wd_hbm.at[layer, e, pl.ds((k - MOE_UP_CHUNKS) * MOE_W_CHUNK, MOE_W_CHUNK), :]
        return pltpu.make_async_copy(src, stage.at[k % 2], sem.at[k % 2])

    def convert(e, k, slot):
        chunk_copy(e, k).wait()
        v = stage[k % 2].astype(BF16)
        if k < MOE_UP_CHUNKS:
            wub[slot, k * MOE_W_CHUNK:(k + 1) * MOE_W_CHUNK, :] = v
        else:
            kd = k - MOE_UP_CHUNKS
            wdb[slot, kd * MOE_W_CHUNK:(kd + 1) * MOE_W_CHUNK, :] = v
        if k + 2 < MOE_CHUNKS:
            chunk_copy(e, k + 2).start()

    def begin(e):
        chunk_copy(e, 0).start()
        chunk_copy(e, 1).start()
        st[1] = 0

    @pl.when(r < nu_ref[0])
    def _():
        e_cur = te_ref[r]
        e_nxt = nxt_ref[r]

        @pl.when(r == 0)
        def _():
            st[0] = 1
            begin(e_cur)

        @pl.when(first_ref[r] == 1)
        def _():
            slot = 1 - st[0]
            done = st[1]
            for k in range(MOE_CHUNKS):
                @pl.when(k >= done)
                def _():
                    convert(e_cur, k, slot)
            st[0] = slot
            st[1] = MOE_CHUNKS

            @pl.when(e_nxt >= 0)
            def _():
                begin(e_nxt)

        cur = st[0]
        x = xs_ref[...].astype(BF16)
        hc = jnp.dot(x, wub[cur], preferred_element_type=F32)
        a = hc[:, :DFF]
        b = hc[:, DFF:]
        act = (a * _sigmoid(a) * b).astype(BF16)
        ys_ref[...] = jnp.dot(act, wdb[cur], preferred_element_type=F32)

        @pl.when(e_nxt >= 0)
        def _():
            done = st[1]
            share = (MOE_CHUNKS - done + rem_ref[r] - 1) // rem_ref[r]
            for k in range(MOE_CHUNKS):
                @pl.when(jnp.logical_and(k >= done, k < done + share))
                def _():
                    convert(e_nxt, k, 1 - cur)
            st[1] = done + share

    @pl.when(r >= nu_ref[0])
    def _():
        ys_ref[...] = jnp.zeros_like(ys_ref)


def _moe(plan, xs, wu, wd, layer):
    def row(r, te, nu, *_):
        return (jnp.minimum(r, nu[0] - 1), 0)

    return pl.pallas_call(
        functools.partial(_moe_body, layer=layer),
        out_shape=jax.ShapeDtypeStruct(xs.shape, F32),
        grid_spec=pltpu.PrefetchScalarGridSpec(
            num_scalar_prefetch=5,
            grid=(xs.shape[0] // TM_E,),
            in_specs=[
                pl.BlockSpec((TM_E, D), row),
                pl.BlockSpec(memory_space=pl.ANY),
                pl.BlockSpec(memory_space=pl.ANY),
            ],
            out_specs=pl.BlockSpec((TM_E, D), lambda r, *_: (r, 0)),
            scratch_shapes=[
                pltpu.VMEM((2, D, 2 * DFF), BF16),
                pltpu.VMEM((2, DFF, D), BF16),
                pltpu.VMEM((2, MOE_W_CHUNK, D), F32),
                pltpu.SemaphoreType.DMA((2,)),
                pltpu.SMEM((2,), I32),
            ],
        ),
        compiler_params=_cparams(("arbitrary",), 56),
        name="moe",
    )(*plan, xs, wu, wd)


def _combine_body(pos0_ref, pos1_ref, posn0_ref, posn1_ref, ys_ref, x_ref, w_ref, mod_ref, g_ref, nmod_ref,
                  *rest, final):
    tm = TM_TOK
    outs, (ybuf, sem) = rest[:-2], rest[-2:]
    i = pl.program_id(0)
    slot = i % 2

    def start_row(p0_ref, p1_ref, s, r):
        _row_copy(ys_ref, p0_ref[0, r], ybuf.at[s, 0], r, sem.at[s]).start(priority=0)
        _row_copy(ys_ref, p1_ref[0, r], ybuf.at[s, 1], r, sem.at[s]).start(priority=1)

    def drain(s):
        for k in range(2):
            pltpu.make_async_copy(ys_ref.at[pl.ds(0, tm)], ybuf.at[s, k], sem.at[s]).wait()

    @pl.when(i == 0)
    def _():
        _issue_rows(tm, functools.partial(start_row, pos0_ref, pos1_ref, 0))

    drain(slot)

    reps = D // LANES
    rows_per_trip = COMBINE_ROWS_PER_TRIP

    def trip(t, c):
        base = pl.multiple_of(t * rows_per_trip, rows_per_trip)
        rows = pl.ds(base, rows_per_trip)
        w0 = jnp.tile(w_ref[rows, 0:LANES], (1, reps))
        w1 = jnp.tile(w_ref[rows, LANES:2 * LANES], (1, reps))
        xn = x_ref[rows, :] + mod_ref[5:6, :] * (w0 * ybuf[slot, 0, rows, :] + w1 * ybuf[slot, 1, rows, :])
        if final:
            outs[0][rows, :] = _rms(xn, g_ref[...])
        else:
            outs[0][rows, :] = xn
            outs[1][rows, :] = (_rms(xn, g_ref[...]) * (1.0 + nmod_ref[1:2, :]) + nmod_ref[0:1, :]).astype(BF16)
        for u in range(rows_per_trip):
            start_row(posn0_ref, posn1_ref, 1 - slot, base + u)
        return c

    lax.fori_loop(0, tm // rows_per_trip, trip, 0)

    @pl.when(i == pl.num_programs(0) - 1)
    def _():
        drain(1 - slot)


def _combine(pos, ys, x, wlanes, mods, layer, n_rows, g_next, final):
    tm = TM_TOK
    nt = n_rows // tm
    next_layer = min(layer + 1, DEPTH - 1)
    tile = pl.BlockSpec((tm, D), lambda i: (i, 0))
    out_shape = [jax.ShapeDtypeStruct((n_rows, D), F32)]
    if not final:
        out_shape.append(jax.ShapeDtypeStruct((n_rows, D), BF16))
    return pl.pallas_call(
        functools.partial(_combine_body, final=final),
        out_shape=out_shape,
        grid=(nt,),
        in_specs=_pos_specs(tm, lambda i: i) + _pos_specs(tm, lambda i: jnp.minimum(i + 1, nt - 1)) + [
            pl.BlockSpec(memory_space=pl.ANY),
            tile,
            pl.BlockSpec((tm, 2 * LANES), lambda i: (i, 0)),
            pl.BlockSpec((None, None, 6, D), lambda i: (layer, _mod_row(i, tm), 0, 0)),
            pl.BlockSpec((1, D), lambda i: (0, 0)),
            pl.BlockSpec((None, None, 6, D), lambda i: (next_layer, _mod_row(i, tm), 0, 0)),
        ],
        out_specs=[tile] * len(out_shape),
        scratch_shapes=[pltpu.VMEM((2, 2, tm, D), F32), pltpu.SemaphoreType.DMA((2,))],
        compiler_params=_cparams(("arbitrary",), 40),
        name="combine",
    )(pos, pos, pos, pos, ys, x, wlanes, mods, g_next.reshape(1, D), mods)


def _route_plan(route, cnt, n_rows):
    counts = cnt[:, 0].astype(I32)
    padded = ((counts + TM_E - 1) // TM_E) * TM_E
    ends = jnp.cumsum(padded)
    offs = ends - padded
    e01 = route[0:2].astype(I32)
    eids = jnp.arange(NE, dtype=I32)[:, None, None]
    off01 = jnp.sum(jnp.where(e01[None] == eids, offs[:, None, None], 0), axis=0)
    pos = off01 + route[2:4].astype(I32)
    pos = pos.reshape(2, n_rows // TM_TOK, 1, TM_TOK)
    nt = _n_expert_tiles(n_rows)
    tile_start = jnp.arange(nt, dtype=I32) * TM_E
    tile_expert = jnp.minimum(jnp.sum((tile_start[:, None] >= ends[None, :]).astype(I32), axis=1), NE - 1)
    n_used = (ends[-1] // TM_E).reshape(1).astype(I32)
    spare = n_used[0] + jnp.arange(NE, dtype=I32)
    zstart = jnp.concatenate([offs + (counts // TM_E) * TM_E, jnp.minimum(spare, nt - 1) * TM_E])
    zneed = jnp.concatenate([counts % TM_E != 0, spare < nt]).astype(I32)
    wl = jnp.concatenate([jnp.broadcast_to(route[4][:, None], (n_rows, LANES)),
                          jnp.broadcast_to(route[5][:, None], (n_rows, LANES))], axis=1)
    ntile = padded // TM_E
    ecol = jnp.arange(NE, dtype=I32)
    onehot = tile_expert[:, None] == ecol[None, :]
    pick = lambda v: jnp.sum(jnp.where(onehot, v[None, :], 0), axis=1)
    j_in = jnp.arange(nt, dtype=I32) - pick(offs // TM_E)
    first = (j_in == 0).astype(I32)
    rem = jnp.maximum(pick(ntile) - j_in, 1)
    later = (ecol[None, :] > ecol[:, None]) & (ntile[None, :] > 0)
    nxt_e = jnp.min(jnp.where(later, ecol[None, :], NE), axis=1)
    nxt = pick(jnp.where(nxt_e < NE, nxt_e, -1))
    moe_plan = (tile_expert.astype(I32), n_used, first, nxt.astype(I32), rem.astype(I32))
    return pos, moe_plan, zstart.astype(I32), zneed, wl


def kernel(x, c, ctx, c_ctx, ada_w, ada_b, norm1_g, norm2_g, w_in, attn_sink, sgu_ln_g, sgu_ln_b, sgu_w, sgu_b,
           sconv_w, conf_dw_w, conf_dw_b, conf_ln_g, conf_ln_b, w_branch, w_out, router_w, router_b,
           exp_w_up, exp_w_down, final_g):
    cvec = jnp.concatenate([c, c_ctx[None, :], jnp.zeros((8 - NB - 1, D), F32)], axis=0)
    mods = _ada(cvec, ada_w, ada_b).reshape(DEPTH, 8, 6, D)
    xa, cxa = x.reshape(TX, D), ctx.reshape(TC, D)
    cos, sin = _rope_tables()
    consts = _rope_constants()
    rw = router_w.T.astype(BF16)
    wb_all = w_branch.astype(BF16)
    wo_all = w_out.astype(BF16)

    for l in range(DEPTH):
        last = l == DEPTH - 1
        n_rows = TX if last else T
        n_x_tiles = TX // TM_TOK
        lp = (sgu_ln_g[l], sgu_ln_b[l], sgu_w[l], sgu_b[l], sconv_w[l], conf_dw_w[l], conf_dw_b[l],
              conf_ln_g[l], conf_ln_b[l])
        c_rows0 = 0 if l == 0 else TX

        if l == 0:
            h = _norm_mod(xa, cxa, norm1_g[l], mods, l)
        if not last:
            p = _inproj(h, w_in, l, T, N_IN // TN_IN)
            qr, kr, vr = _prep(p, T, n_x_tiles, cos, sin, consts)
            ya = _attn(qr, kr, vr, kr, vr, TX // LC, attn_sink[l])
            yac = _ctx_attn(qr, kr, vr, attn_sink[l])
        else:
            p = _inproj(h, w_in, l, TX, N_IN // TN_IN)
            pc = _inproj(h[TX:], w_in, l, TC, 1)
            qr, kr, vr = _prep(p, TX, n_x_tiles, cos, sin, consts)
            _, kc, vc = _prep(pc, TC, 0, cos, sin, consts)
            ya = _attn(qr, kr, vr, kc, vc, 0, attn_sink[l])
            yac = ya
        yr = _mixers(p, n_rows, n_x_tiles, lp)
        xn, h2, lg = _post(p, ya, yac, yr, wb_all, wo_all, xa, cxa, c_rows0 // TM_TOK, mods, l,
                           norm2_g[l], rw, n_rows)
        route, cnt = _route(lg, router_b, n_rows)
        pos, moe_plan, zstart, zneed, wl = _route_plan(route, cnt, n_rows)
        xs = _dispatch(zstart, zneed, pos, h2, n_rows)
        ys = _moe(moe_plan, xs, exp_w_up, exp_w_down, l)
        if last:
            (out,) = _combine(pos, ys, xn, wl, mods, l, n_rows, final_g, True)
        else:
            xa, h = _combine(pos, ys, xn, wl, mods, l, n_rows, norm1_g[l + 1], False)
            cxa = xa

    return out.reshape(NB, S, D)
```

```python
import functools

import numpy as np
import jax
import jax.numpy as jnp
from jax import lax
from jax.experimental import pallas as pl
from jax.experimental.pallas import tpu as pltpu

F32 = jnp.float32
BF16 = jnp.bfloat16
I32 = jnp.int32

D = 2048
NB = 4
S = 2048
LC = 256
DEPTH = 2
GRID_W = 64
BW = 512
HD = 64
NH = 8
NKV = 2
REP = NH // NKV
WINDOW = 128
QB = 128
ROPE_THETA = 10000.0
SGU_CHUNK = 128
SGU_GROUPS = 4
SCONV_K = 3
CONF_K = 31
NE = 16
NG = 4
EPG = NE // NG
DFF = D // 2
N_IN = BW + 2 * NKV * HD + 7 * BW + 4 * D
TX = NB * S
TC = NB * LC
T = TX + TC

OFF_Q, OFF_KV = 0, 512
OFF_SU, OFF_SV, OFF_CB, OFF_CC, OFF_CX, OFF_GA, OFF_GB, OFF_GATES = 768, 1280, 1792, 2304, 2816, 3328, 3840, 4352

LANES = 128
SUBLANES = 8
V7X_VMEM_BYTES = 64 * 1024 * 1024
MIB = 1024 * 1024

TM_NORM = 512
TM_IN = 1024
INPROJ_ROW_CHUNK = 512
IN_W_CHUNK = 256
IN_CHUNKS = D // IN_W_CHUNK
TN_IN = 1792
TM_TOK = 256
TM_POST = 512
POST_ROW_CHUNK = 256
TM_E = 256
HALO = 16
ROW_UNROLL = 8
MOE_W_CHUNK = 512
MOE_UP_CHUNKS = D // MOE_W_CHUNK
MOE_CHUNKS = MOE_UP_CHUNKS + DFF // MOE_W_CHUNK


def _n_expert_tiles(n_rows):
    return (2 * n_rows) // TM_E + NE


def _cparams(sem, vmem_mib):
    return pltpu.CompilerParams(dimension_semantics=sem, vmem_limit_bytes=vmem_mib * MIB)


def _sigmoid(x):
    return 0.5 * jnp.tanh(0.5 * x) + 0.5


def _gelu_tanh(x):
    c = np.float32(np.sqrt(2.0 / np.pi))
    return 0.5 * x * (1.0 + jnp.tanh(c * (x + np.float32(0.044715) * (x * x * x))))


def _layernorm(x, g, b, eps=1e-5):
    mu = jnp.mean(x, axis=-1, keepdims=True)
    xc = x - mu
    var = jnp.mean(xc * xc, axis=-1, keepdims=True)
    return xc * lax.rsqrt(var + eps) * g + b


def _mod_row(i, tm):
    return jnp.where(i < TX // tm, (i * tm) // S, NB)


def _ada_body(c_ref, w_ref, b_ref, o_ref):
    c = c_ref[...]
    s = (c * _sigmoid(c)).astype(BF16)
    o_ref[...] = jnp.dot(s, w_ref[...].astype(BF16), preferred_element_type=F32) + b_ref[...]


def _ada(cvec, ada_w, ada_b):
    tn = 1024
    return pl.pallas_call(
        _ada_body,
        out_shape=jax.ShapeDtypeStruct((DEPTH, 8, 6 * D), F32),
        grid=(DEPTH, 6 * D // tn),
        in_specs=[
            pl.BlockSpec((8, D), lambda l, j: (0, 0)),
            pl.BlockSpec((None, D, tn), lambda l, j: (l, 0, j)),
            pl.BlockSpec((None, 1, tn), lambda l, j: (l, 0, j)),
        ],
        out_specs=pl.BlockSpec((None, 8, tn), lambda l, j: (l, 0, j)),
        compiler_params=_cparams(("arbitrary", "arbitrary"), 40),
        name="ada",
    )(cvec, ada_w, ada_b.reshape(DEPTH, 1, 6 * D))


def _stream_specs(tm, c_blk0, width=D):
    nx = TX // tm
    return [pl.BlockSpec((tm, width), lambda i, *_: (jnp.minimum(i, nx - 1), 0)),
            pl.BlockSpec((tm, width), lambda i, *_: (c_blk0 + jnp.maximum(i - nx, 0), 0))]


def _stream_tile(x_ref, c_ref, tm):
    return jnp.where(pl.program_id(0) < TX // tm, x_ref[...], c_ref[...])


def _rms(x, g):
    return x * lax.rsqrt(jnp.mean(x * x, axis=-1, keepdims=True) + 1e-6) * g


def _norm_mod_body(x_ref, c_ref, g_ref, mod_ref, o_ref):
    y = _rms(_stream_tile(x_ref, c_ref, TM_NORM), g_ref[...])
    o_ref[...] = (y * (1.0 + mod_ref[1:2, :]) + mod_ref[0:1, :]).astype(o_ref.dtype)


def _norm_mod(x, cx, g, mods, layer):
    tm = TM_NORM
    return pl.pallas_call(
        _norm_mod_body,
        out_shape=jax.ShapeDtypeStruct((T, D), BF16),
        grid=(T // tm,),
        in_specs=_stream_specs(tm, 0) + [
            pl.BlockSpec((1, D), lambda i: (0, 0)),
            pl.BlockSpec((None, None, 6, D), lambda i: (layer, _mod_row(i, tm), 0, 0)),
        ],
        out_specs=pl.BlockSpec((tm, D), lambda i: (i, 0)),
        compiler_params=_cparams(("arbitrary",), 32),
        name="norm_mod",
    )(x, cx, g.reshape(1, D), mods)


def _inproj_body(h_ref, w_hbm, o_ref, wbf, stage, sem, st, *, layer):
    j = pl.program_id(0)
    i = pl.program_id(1)
    nj = pl.num_programs(0)
    ni = pl.num_programs(1)

    def chunk_copy(jj, k):
        cols = pl.ds(pl.multiple_of(jj * TN_IN, LANES), TN_IN)
        src = w_hbm.at[layer, pl.ds(k * IN_W_CHUNK, IN_W_CHUNK), cols]
        return pltpu.make_async_copy(src, stage.at[k % 2], sem.at[k % 2])

    def convert(jj, k, slot):
        chunk_copy(jj, k).wait()
        wbf[slot, k * IN_W_CHUNK:(k + 1) * IN_W_CHUNK, :] = stage[k % 2].astype(BF16)
        if k + 2 < IN_CHUNKS:
            chunk_copy(jj, k + 2).start()

    def begin(jj):
        chunk_copy(jj, 0).start()
        chunk_copy(jj, 1).start()
        st[1] = 0

    @pl.when(jnp.logical_and(i == 0, j == 0))
    def _():
        st[0] = 1
        begin(0)

    @pl.when(i == 0)
    def _():
        slot = 1 - st[0]
        done = st[1]
        for k in range(IN_CHUNKS):
            @pl.when(k >= done)
            def _():
                convert(j, k, slot)
        st[0] = slot
        st[1] = IN_CHUNKS

        @pl.when(j + 1 < nj)
        def _():
            begin(j + 1)

    cur = st[0]
    col0 = j * TN_IN
    rows = INPROJ_ROW_CHUNK

    def run(epilogue):
        for c in range(h_ref.shape[0] // rows):
            sl = slice(c * rows, (c + 1) * rows)
            acc = jnp.dot(h_ref[sl, :], wbf[cur], preferred_element_type=F32)
            o_ref[sl, :] = epilogue(acc).astype(BF16)

    @pl.when(col0 + TN_IN <= OFF_GB)
    def _():
        run(lambda acc: acc)

    @pl.when(col0 >= OFF_GB)
    def _():
        run(_sigmoid)

    @pl.when(jnp.logical_and(col0 < OFF_GB, col0 + TN_IN > OFF_GB))
    def _():
        col = col0 + lax.broadcasted_iota(I32, (rows, TN_IN), 1)
        run(lambda acc: jnp.where(col >= OFF_GB, _sigmoid(acc), acc))

    @pl.when(j + 1 < nj)
    def _():
        done = st[1]
        left = ni - i
        share = (IN_CHUNKS - done + left - 1) // left
        for k in range(IN_CHUNKS):
            @pl.when(jnp.logical_and(k >= done, k < done + share))
            def _():
                convert(j + 1, k, 1 - cur)
        st[1] = done + share


def _inproj(h, w_in, layer, n_rows, n_col_tiles):
    tm = TM_IN
    return pl.pallas_call(
        functools.partial(_inproj_body, layer=layer),
        out_shape=jax.ShapeDtypeStruct((n_rows, n_col_tiles * TN_IN), BF16),
        grid=(n_col_tiles, n_rows // tm),
        in_specs=[
            pl.BlockSpec((tm, D), lambda j, i: (i, 0)),
            pl.BlockSpec(memory_space=pl.ANY),
        ],
        out_specs=pl.BlockSpec((tm, TN_IN), lambda j, i: (i, j)),
        scratch_shapes=[
            pltpu.VMEM((2, D, TN_IN), BF16),
            pltpu.VMEM((2, IN_W_CHUNK, TN_IN), F32),
            pltpu.SemaphoreType.DMA((2,)),
            pltpu.SMEM((2,), I32),
        ],
        compiler_params=_cparams(("arbitrary", "arbitrary"), 56),
        name="inproj",
    )(h, w_in)


def _rope_constants():
    rh = np.zeros((HD, HD), np.float32)
    for base in (0, 32):
        for d in range(16):
            rh[base + 16 + d, base + d] = -1.0
            rh[base + d, base + 16 + d] = 1.0
    rq = np.kron(np.eye(NH, dtype=np.float32), rh)
    rk = np.kron(np.eye(NKV, dtype=np.float32), rh)
    rep = np.zeros((NKV * HD, NH * HD), np.float32)
    for g in range(NKV):
        for r in range(REP):
            for d in range(HD):
                rep[g * HD + d, g * REP * HD + r * HD + d] = 1.0
    return jnp.asarray(rq, BF16), jnp.asarray(rep, BF16), jnp.asarray(rk @ rep, BF16)


def _rope_tables():
    half = HD // 2
    inv = 1.0 / (ROPE_THETA ** (jnp.arange(0, half, 2, dtype=F32) / half))
    pos = jnp.arange(S)
    ar = (pos // GRID_W).astype(F32)[:, None] * inv
    ac = (pos % GRID_W).astype(F32)[:, None] * inv
    reps = LANES // HD
    cos = jnp.tile(jnp.concatenate([jnp.cos(ar), jnp.cos(ar), jnp.cos(ac), jnp.cos(ac)], axis=1), (1, reps))
    sin = jnp.tile(jnp.concatenate([jnp.sin(ar), jnp.sin(ar), jnp.sin(ac), jnp.sin(ac)], axis=1), (1, reps))
    cos = jnp.concatenate([cos, jnp.ones((TM_TOK, LANES), F32)], axis=0)
    sin = jnp.concatenate([sin, jnp.zeros((TM_TOK, LANES), F32)], axis=0)
    return cos, sin


def _prep_body(q_ref, kv_ref, cos_ref, sin_ref, rq_ref, rep_ref, rrep_ref, qo_ref, ko_ref, vo_ref):
    cos = jnp.tile(cos_ref[...], (1, NH * HD // LANES))
    sin = jnp.tile(sin_ref[...], (1, NH * HD // LANES))
    q = q_ref[...]
    qs = jnp.dot(q, rq_ref[...], preferred_element_type=F32)
    qo_ref[...] = ((q.astype(F32) * cos + qs * sin) * (HD ** -0.5)).astype(BF16)
    k = kv_ref[:, 0:NKV * HD]
    v = kv_ref[:, NKV * HD:2 * NKV * HD]
    kr = jnp.dot(k, rep_ref[...], preferred_element_type=F32)
    ks = jnp.dot(k, rrep_ref[...], preferred_element_type=F32)
    ko_ref[...] = (kr * cos + ks * sin).astype(BF16)
    vo_ref[...] = jnp.dot(v, rep_ref[...], preferred_element_type=F32).astype(BF16)


def _prep(p, n_rows, n_x_tiles, cos, sin, consts):
    tm = TM_TOK
    rq, rep, rrep = consts
    tps = S // tm

    def tab(i):
        return (jnp.where(i < n_x_tiles, i % tps, tps), 0)

    w = NH * HD
    return pl.pallas_call(
        _prep_body,
        out_shape=[jax.ShapeDtypeStruct((n_rows, w), BF16)] * 3,
        grid=(n_rows // tm,),
        in_specs=[
            pl.BlockSpec((tm, w), lambda i: (i, 0)),
            pl.BlockSpec((tm, 2 * NKV * HD), lambda i: (i, OFF_KV // (2 * NKV * HD))),
            pl.BlockSpec((tm, LANES), tab),
            pl.BlockSpec((tm, LANES), tab),
            pl.BlockSpec((w, w), lambda i: (0, 0)),
            pl.BlockSpec((NKV * HD, w), lambda i: (0, 0)),
            pl.BlockSpec((NKV * HD, w), lambda i: (0, 0)),
        ],
        out_specs=[pl.BlockSpec((tm, w), lambda i: (i, 0))] * 3,
        compiler_params=_cparams(("arbitrary",), 32),
        name="prep",
    )(p, p, cos, sin, rq, rep, rrep)


def _ctx_kv_body(h_ref, w_ref, rep_ref, ko_ref, vo_ref):
    kv = jnp.dot(h_ref[...], w_ref[...].astype(BF16), preferred_element_type=F32).astype(BF16)
    ko_ref[...] = jnp.dot(kv[:, 0:NKV * HD], rep_ref[...], preferred_element_type=F32).astype(BF16)
    vo_ref[...] = jnp.dot(kv[:, NKV * HD:2 * NKV * HD], rep_ref[...], preferred_element_type=F32).astype(BF16)


def _ctx_kv(h, w_in, layer, rep):
    tm = TM_TOK
    w = NH * HD
    kvw = 2 * NKV * HD
    return pl.pallas_call(
        _ctx_kv_body,
        out_shape=[jax.ShapeDtypeStruct((TC, w), BF16)] * 2,
        grid=(TC // tm,),
        in_specs=[
            pl.BlockSpec((tm, D), lambda i: (TX // tm + i, 0)),
            pl.BlockSpec((None, pl.Element(D), pl.Element(kvw)), lambda i: (layer, 0, OFF_KV)),
            pl.BlockSpec((NKV * HD, w), lambda i: (0, 0)),
        ],
        out_specs=[pl.BlockSpec((tm, w), lambda i: (i, 0))] * 2,
        compiler_params=_cparams(("arbitrary",), 32),
        name="ctx_kv",
    )(h, w_in, rep)


def _attn_groups(q, kbs, vbs, sink_ref, valid, nq):
    gw = REP * HD
    lane_head = lax.broadcasted_iota(I32, (nq, gw), 1) // HD
    row_head = lax.broadcasted_iota(I32, (REP * nq, 1), 0) // nq
    valid_rep = None if valid is None else jnp.concatenate([valid] * REP, axis=0)
    scores = []
    for g in range(NKV):
        qg = q[:, g * gw:(g + 1) * gw]
        qs = jnp.concatenate([jnp.where(lane_head == r, qg, jnp.zeros_like(qg)) for r in range(REP)], axis=0)
        s = lax.dot_general(qs, kbs[g], (((1,), (1,)), ((), ())), preferred_element_type=F32)
        scores.append(s if valid_rep is None else jnp.where(valid_rep, s, -jnp.inf))
    probs = []
    for g in range(NKV):
        s = scores[g]
        sink = jnp.zeros((REP * nq, 1), F32)
        for r in range(REP):
            sink = jnp.where(row_head == r, sink_ref[g * REP + r], sink)
        m = jnp.maximum(jnp.max(s, axis=-1, keepdims=True), sink)
        e = jnp.exp(s - m)
        den = jnp.sum(e, axis=-1, keepdims=True) + jnp.exp(sink - m)
        probs.append((e * (1.0 / den)).astype(BF16))
    outs = []
    for g in range(NKV):
        o = jnp.dot(probs[g], vbs[g], preferred_element_type=F32)
        og = jnp.zeros((nq, gw), F32)
        for r in range(REP):
            og = og + jnp.where(lane_head == r, o[r * nq:(r + 1) * nq, :], 0.0)
        outs.append(og)
    return jnp.concatenate(outs, axis=1).astype(BF16)


def _attn_body(sink_ref, q_ref, kp_ref, kc_ref, kn_ref, vp_ref, vc_ref, vn_ref, kx_ref, vx_ref, o_ref):
    n = pl.program_id(1)
    nblk = S // QB
    nk = 3 * QB + LC
    row = lax.broadcasted_iota(I32, (QB, nk), 0)
    col = lax.broadcasted_iota(I32, (QB, nk), 1)
    lo = jnp.where(n == 0, QB, 0)
    hi = jnp.where(n == nblk - 1, 2 * QB, 3 * QB)
    band = (col >= row) & (col <= row + 2 * WINDOW) & (col >= lo) & (col < hi)
    valid = band | (col >= 3 * QB)
    gw = REP * HD
    kbs, vbs = [], []
    for g in range(NKV):
        sl = slice(g * gw, (g + 1) * gw)
        kbs.append(jnp.concatenate([kp_ref[:, sl], kc_ref[:, sl], kn_ref[:, sl], kx_ref[:, sl]], axis=0))
        vbs.append(jnp.concatenate([vp_ref[:, sl], vc_ref[:, sl], vn_ref[:, sl], vx_ref[:, sl]], axis=0))
    o_ref[...] = _attn_groups(q_ref[...], kbs, vbs, sink_ref, valid, QB)


def _attn(qr, kr, vr, kc_arr, vc_arr, ctx_blk0, sink):
    nblk = S // QB
    w = NH * HD

    def cur(b, n):
        return (b * nblk + n, 0)

    def prev(b, n):
        return (b * nblk + jnp.maximum(n - 1, 0), 0)

    def nxt(b, n):
        return (b * nblk + jnp.minimum(n + 1, nblk - 1), 0)

    def cx(b, n):
        return (ctx_blk0 + b, 0)

    blk = lambda f: pl.BlockSpec((QB, w), f)
    return pl.pallas_call(
        _attn_body,
        out_shape=jax.ShapeDtypeStruct((TX, w), BF16),
        grid=(NB, nblk),
        in_specs=[
            pl.BlockSpec(memory_space=pltpu.SMEM),
            blk(cur), blk(prev), blk(cur), blk(nxt), blk(prev), blk(cur), blk(nxt),
            pl.BlockSpec((LC, w), cx), pl.BlockSpec((LC, w), cx),
        ],
        out_specs=blk(cur),
        compiler_params=_cparams(("arbitrary", "arbitrary"), 32),
        name="window_attn",
    )(sink, qr, kr, kr, kr, vr, vr, vr, kc_arr, vc_arr)


def _ctx_attn_body(sink_ref, q_ref, kx_ref, vx_ref, o_ref):
    gw = REP * HD
    kbs = [kx_ref[:, g * gw:(g + 1) * gw] for g in range(NKV)]
    vbs = [vx_ref[:, g * gw:(g + 1) * gw] for g in range(NKV)]
    o_ref[...] = _attn_groups(q_ref[...], kbs, vbs, sink_ref, None, LC)


def _ctx_attn(qr, kr, vr, sink):
    w = NH * HD
    blk0 = TX // LC
    spec = pl.BlockSpec((LC, w), lambda b: (blk0 + b, 0))
    return pl.pallas_call(
        _ctx_attn_body,
        out_shape=jax.ShapeDtypeStruct((TC, w), BF16),
        grid=(NB,),
        in_specs=[pl.BlockSpec(memory_space=pltpu.SMEM), spec, spec, spec],
        out_specs=pl.BlockSpec((LC, w), lambda b: (b, 0)),
        compiler_params=_cparams(("arbitrary",), 32),
        name="ctx_attn",
    )(sink, qr, kr, vr)


def _mix_body(su_ref, sv_ref, cb_ref, cc_ref, cx_ref, ga_ref, gb_ref,
              ccp_ref, cxp_ref, gap_ref, gbp_ref, ccn_ref, cxn_ref, gan_ref, gbn_ref,
              lng_ref, lnb_ref, sw_ref, sbias_ref, scw_ref, dww_ref, dwb_ref, clg_ref, clb_ref,
              o_ref, m_scr, z_scr, *, n_x_tiles):
    tm = TM_TOK
    i = pl.program_id(0)
    tps = S // tm
    is_x = i < n_x_tiles
    first = jnp.logical_or(jnp.logical_not(is_x), (i % tps) == 0)
    last = jnp.logical_or(jnp.logical_not(is_x), (i % tps) == tps - 1)
    keep_p = jnp.where(first, 0.0, 1.0)
    keep_n = jnp.where(last, 0.0, 1.0)

    u = _gelu_tanh(su_ref[...].astype(F32))
    v = _layernorm(_gelu_tanh(sv_ref[...].astype(F32)), lng_ref[...], lnb_ref[...]).astype(BF16)
    gwid = BW // SGU_GROUPS
    chunks = []
    for c in range(tm // SGU_CHUNK):
        parts = []
        for g in range(SGU_GROUPS):
            vc = v[c * SGU_CHUNK:(c + 1) * SGU_CHUNK, g * gwid:(g + 1) * gwid]
            parts.append(jnp.dot(sw_ref[g], vc, preferred_element_type=F32))
        chunks.append(jnp.concatenate(parts, axis=1) + sbias_ref[...])
    yb = u * jnp.concatenate(chunks, axis=0)

    m_scr[HALO:HALO + tm, :] = cc_ref[...].astype(F32) * cx_ref[...].astype(F32)
    m_scr[0:HALO, :] = ccp_ref[...].astype(F32) * cxp_ref[...].astype(F32) * keep_p
    m_scr[HALO + tm:, :] = ccn_ref[...].astype(F32) * cxn_ref[...].astype(F32) * keep_n
    z = jnp.zeros((tm, BW), F32)
    for k in range(SCONV_K):
        z = z + scw_ref[k:k + 1, :] * m_scr[pl.ds(HALO - SCONV_K // 2 + k, tm), :]
    yc = cb_ref[...].astype(F32) * z

    z_scr[0, HALO:HALO + tm, :] = ga_ref[...].astype(F32) * gb_ref[...].astype(F32)
    z_scr[0, 0:HALO, :] = gap_ref[...].astype(F32) * gbp_ref[...].astype(F32) * keep_p
    z_scr[0, HALO + tm:, :] = gan_ref[...].astype(F32) * gbn_ref[...].astype(F32) * keep_n
    n_sh = tm + 2 * HALO - SUBLANES
    for s in range(1, SUBLANES):
        z_scr[s, 0:n_sh, :] = z_scr[0, pl.ds(s, n_sh), :]
    acc = jnp.zeros((tm, BW), F32) + dwb_ref[...]
    for k in range(CONF_K):
        off = HALO - CONF_K // 2 + k
        base = off - off % SUBLANES
        acc = acc + dww_ref[k:k + 1, :] * z_scr[off % SUBLANES, base:base + tm, :]
    zn = _layernorm(acc, clg_ref[...], clb_ref[...])
    yd = zn * _sigmoid(zn)

    o_ref[...] = jnp.concatenate([yb, yc, yd], axis=1).astype(BF16)


def _mixers(p, n_rows, n_x_tiles, lp):
    tm = TM_TOK
    (sgu_ln_g, sgu_ln_b, sgu_w, sgu_b, sconv_w, conf_dw_w, conf_dw_b, conf_ln_g, conf_ln_b) = lp

    def sec(off):
        return pl.BlockSpec((pl.Element(tm), pl.Element(BW)), lambda i: (i * tm, off))

    hpt = tm // HALO

    def halo_prev(off):
        return pl.BlockSpec((pl.Element(HALO), pl.Element(BW)),
                            lambda i: (jnp.maximum(i * hpt - 1, 0) * HALO, off))

    def halo_next(off):
        return pl.BlockSpec((pl.Element(HALO), pl.Element(BW)),
                            lambda i: (jnp.minimum(i * hpt + hpt, n_rows // HALO - 1) * HALO, off))

    def const(shape):
        return pl.BlockSpec(shape, lambda i: (0,) * len(shape))

    sbias = jnp.repeat(sgu_b.T, BW // SGU_GROUPS, axis=1)
    row = lambda a: a.reshape(1, BW)
    conv_offs = (OFF_CC, OFF_CX, OFF_GA, OFF_GB)
    return pl.pallas_call(
        functools.partial(_mix_body, n_x_tiles=n_x_tiles),
        out_shape=jax.ShapeDtypeStruct((n_rows, 3 * BW), BF16),
        grid=(n_rows // tm,),
        in_specs=[sec(o) for o in (OFF_SU, OFF_SV, OFF_CB, OFF_CC, OFF_CX, OFF_GA, OFF_GB)]
        + [halo_prev(o) for o in conv_offs] + [halo_next(o) for o in conv_offs]
        + [const((1, BW)), const((1, BW)), const((SGU_GROUPS, SGU_CHUNK, SGU_CHUNK)), const((SGU_CHUNK, BW)),
           const((SCONV_K, BW)), const((CONF_K, BW)), const((1, BW)), const((1, BW)), const((1, BW))],
        out_specs=pl.BlockSpec((tm, 3 * BW), lambda i: (i, 0)),
        scratch_shapes=[pltpu.VMEM((tm + 2 * HALO, BW), F32), pltpu.VMEM((SUBLANES, tm + 2 * HALO, BW), F32)],
        compiler_params=_cparams(("arbitrary",), 32),
        name="mixers",
    )(*([p] * 15), row(sgu_ln_g), row(sgu_ln_b), sgu_w.astype(BF16), sbias, sconv_w, conf_dw_w,
      row(conf_dw_b), row(conf_ln_g), row(conf_ln_b))


def _norm2(xn, n2g, mod):
    return _rms(xn, n2g) * (1.0 + mod[4:5, :]) + mod[3:4, :]


def _post_body(g_ref, ya_ref, yac_ref, yr_ref, wb_ref, wo_ref, x_ref, cx_ref, mod_ref, n2g_ref, rw_ref,
               xo_ref, h2_ref, lg_ref):
    tm = TM_TOK
    merged = None
    for br in range(4):
        y = _stream_tile(ya_ref, yac_ref, tm) if br == 0 else yr_ref[:, (br - 1) * BW:br * BW]
        pr = jnp.dot(y, wb_ref[br], preferred_element_type=F32)
        term = g_ref[:, br * D:(br + 1) * D].astype(F32) * pr
        merged = term if merged is None else merged + term
    out = jnp.dot(merged.astype(BF16), wo_ref[...], preferred_element_type=F32)
    xn = _stream_tile(x_ref, cx_ref, tm) + mod_ref[2:3, :] * out
    xo_ref[...] = xn
    h2 = _norm2(xn, n2g_ref[...], mod_ref)
    h2_ref[...] = h2
    nt = (((1,), (1,)), ((), ()))
    lg_ref[...] = lax.dot_general(rw_ref[...], h2.astype(BF16), nt, preferred_element_type=F32)


def _post(p, ya, yac, yr, wb, wo, x, cx, c_blk0, mods, layer, n2g, rw, n_rows):
    tm = TM_TOK
    one = pl.Buffered(1)
    return pl.pallas_call(
        _post_body,
        out_shape=[jax.ShapeDtypeStruct((n_rows, D), F32), jax.ShapeDtypeStruct((n_rows, D), F32),
                   jax.ShapeDtypeStruct((NE, n_rows), F32)],
        grid=(n_rows // tm,),
        in_specs=[pl.BlockSpec((pl.Element(tm), pl.Element(4 * D)), lambda i: (i * tm, OFF_GATES))]
        + _stream_specs(tm, 0, BW) + [
            pl.BlockSpec((tm, 3 * BW), lambda i: (i, 0)),
            pl.BlockSpec((None, 4, BW, D), lambda i: (layer, 0, 0, 0), pipeline_mode=one),
            pl.BlockSpec((None, D, D), lambda i: (layer, 0, 0), pipeline_mode=one)]
        + _stream_specs(tm, c_blk0) + [
            pl.BlockSpec((None, None, 6, D), lambda i: (layer, _mod_row(i, tm), 0, 0)),
            pl.BlockSpec((1, D), lambda i: (0, 0)),
            pl.BlockSpec((NE, D), lambda i: (0, 0)),
        ],
        out_specs=[pl.BlockSpec((tm, D), lambda i: (i, 0)), pl.BlockSpec((tm, D), lambda i: (i, 0)),
                   pl.BlockSpec((NE, tm), lambda i: (0, i))],
        compiler_params=_cparams(("arbitrary",), 52),
        name="post",
    )(p, ya, yac, yr, wb, wo, x, cx, mods, n2g.reshape(1, D), rw)


def _route_body(lg_ref, rb_ref, tri_ref, o_ref, cnt_ref, carry_ref):
    tm = TM_TOK
    i = pl.program_id(0)

    @pl.when(i == 0)
    def _():
        carry_ref[...] = jnp.zeros_like(carry_ref)

    lg = lg_ref[...]
    e = jnp.exp(lg - jnp.max(lg, axis=0, keepdims=True))
    sc = e / jnp.sum(e, axis=0, keepdims=True)
    bi = sc + rb_ref[...]
    b = [bi[k:k + 1, :] for k in range(NE)]
    s = [sc[k:k + 1, :] for k in range(NE)]

    gs = []
    for g in range(NG):
        v = b[g * EPG:(g + 1) * EPG]
        best = None
        for a in range(EPG):
            for c in range(a + 1, EPG):
                ps = v[a] + v[c]
                best = ps if best is None else jnp.maximum(best, ps)
        gs.append(best)
    gsel = jnp.zeros((1, tm), I32)
    gbest = gs[0]
    for g in range(1, NG):
        take = gs[g] > gbest
        gsel = jnp.where(take, g, gsel)
        gbest = jnp.where(take, gs[g], gbest)

    vb, vs = [], []
    for j in range(EPG):
        xb, xs = b[j], s[j]
        for g in range(1, NG):
            xb = jnp.where(gsel == g, b[g * EPG + j], xb)
            xs = jnp.where(gsel == g, s[g * EPG + j], xs)
        vb.append(xb)
        vs.append(xs)
    order = []
    for j in range(EPG):
        c = jnp.zeros((1, tm), I32)
        for m in range(EPG):
            if m == j:
                continue
            ahead = (vb[m] >= vb[j]) if m < j else (vb[m] > vb[j])
            c = c + jnp.where(ahead, 1, 0)
        order.append(c)
    zero = jnp.zeros((1, tm), F32)
    w0 = zero
    w1 = zero
    j0 = jnp.zeros((1, tm), I32)
    j1 = jnp.zeros((1, tm), I32)
    for j in range(EPG):
        w0 = jnp.where(order[j] == 0, vs[j], w0)
        w1 = jnp.where(order[j] == 1, vs[j], w1)
        j0 = jnp.where(order[j] == 0, j, j0)
        j1 = jnp.where(order[j] == 1, j, j1)
    tot = w0 + w1
    e0 = gsel * EPG + j0
    e1 = gsel * EPG + j1

    eid = lax.broadcasted_iota(I32, (NE, tm), 0)
    oh0 = eid == e0
    oh1 = eid == e1
    oh = jnp.where(oh0 | oh1, 1.0, 0.0)
    rank = jnp.dot(oh.astype(BF16), tri_ref[...], preferred_element_type=F32) + carry_ref[:, 0:1]
    r0 = jnp.sum(jnp.where(oh0, rank, 0.0), axis=0, keepdims=True)
    r1 = jnp.sum(jnp.where(oh1, rank, 0.0), axis=0, keepdims=True)
    new_carry = carry_ref[...] + jnp.sum(oh, axis=1, keepdims=True)
    carry_ref[...] = new_carry
    cnt_ref[...] = new_carry

    o_ref[0:1, :] = e0.astype(F32)
    o_ref[1:2, :] = e1.astype(F32)
    o_ref[2:3, :] = r0
    o_ref[3:4, :] = r1
    o_ref[4:5, :] = w0 / tot
    o_ref[5:6, :] = w1 / tot
    o_ref[6:8, :] = jnp.zeros((2, tm), F32)


def _route(lg, router_b, n_rows):
    tm = TM_TOK
    tri = jnp.asarray(np.triu(np.ones((tm, tm), np.float32), 1), BF16)
    return pl.pallas_call(
        _route_body,
        out_shape=[jax.ShapeDtypeStruct((8, n_rows), F32), jax.ShapeDtypeStruct((NE, LANES), F32)],
        grid=(n_rows // tm,),
        in_specs=[
            pl.BlockSpec((NE, tm), lambda i: (0, i)),
            pl.BlockSpec((NE, 1), lambda i: (0, 0)),
            pl.BlockSpec((tm, tm), lambda i: (0, 0)),
        ],
        out_specs=[pl.BlockSpec((8, tm), lambda i: (0, i)), pl.BlockSpec((NE, LANES), lambda i: (0, 0))],
        scratch_shapes=[pltpu.VMEM((NE, LANES), F32)],
        compiler_params=_cparams(("arbitrary",), 32),
        name="route",
    )(lg, router_b.reshape(NE, 1), tri)


def _row_copy(src_ref, src_row, dst_ref, dst_row, sem):
    return pltpu.make_async_copy(src_ref.at[pl.ds(src_row, 1)], dst_ref.at[pl.ds(dst_row, 1)], sem)


def _issue_rows(n, start_row):
    def trip(t, c):
        base = pl.multiple_of(t * ROW_UNROLL, ROW_UNROLL)
        for u in range(ROW_UNROLL):
            start_row(base + u)
        return c

    lax.fori_loop(0, n // ROW_UNROLL, trip, 0)


def _dispatch_body(tail_ref, need_ref, pos0_ref, pos1_ref, h_ref, xs_ref, zero_ref, sem, zsem):
    tm = TM_TOK

    @pl.when(pl.program_id(0) == 0)
    def _():
        zero_ref[...] = jnp.zeros_like(zero_ref)

        def tail_copy(e):
            start = pl.multiple_of(tail_ref[e], TM_E)
            return pltpu.make_async_copy(zero_ref, xs_ref.at[pl.ds(start, TM_E)], zsem)

        for e in range(2 * NE):
            @pl.when(need_ref[e] > 0)
            def _():
                tail_copy(e).start()
        for e in range(2 * NE):
            @pl.when(need_ref[e] > 0)
            def _():
                tail_copy(e).wait()

    def start_row(r):
        _row_copy(h_ref, r, xs_ref, pos0_ref[0, r], sem).start()
        _row_copy(h_ref, r, xs_ref, pos1_ref[0, r], sem).start()

    _issue_rows(tm, start_row)
    for _ in range(2):
        pltpu.make_async_copy(h_ref, xs_ref.at[pl.ds(0, tm)], sem).wait()


def _pos_specs(tm, index):
    return [pl.BlockSpec((None, None, 1, tm), lambda i, *_, k=k: (k, index(i), 0, 0), memory_space=pltpu.SMEM)
            for k in range(2)]


def _dispatch(tail, need, pos, h2, n_rows):
    tm = TM_TOK
    return pl.pallas_call(
        _dispatch_body,
        out_shape=jax.ShapeDtypeStruct((_n_expert_tiles(n_rows) * TM_E, D), F32),
        grid_spec=pltpu.PrefetchScalarGridSpec(
            num_scalar_prefetch=2,
            grid=(n_rows // tm,),
            in_specs=_pos_specs(tm, lambda i: i) + [pl.BlockSpec((tm, D), lambda i, t, n: (i, 0))],
            out_specs=pl.BlockSpec(memory_space=pl.ANY),
            scratch_shapes=[pltpu.VMEM((TM_E, D), F32), pltpu.SemaphoreType.DMA(()), pltpu.SemaphoreType.DMA(())],
        ),
        compiler_params=_cparams(("arbitrary",), 32),
        name="dispatch",
    )(tail, need, pos, pos, h2)


def _moe_body(te_ref, nu_ref, first_ref, nxt_ref, rem_ref, xs_ref, wu_hbm, wd_hbm, ys_ref,
              wub, wdb, stage, sem, st, *, layer):
    r = pl.program_id(0)

    def chunk_copy(e, k):
        if k < MOE_UP_CHUNKS:
            src = wu_hbm.at[layer, e, pl.ds(k * MOE_W_CHUNK, MOE_W_CHUNK), :]
        else:
            src = wd_hbm.at[layer, e, pl.ds((k - MOE_UP_CHUNKS) * MOE_W_CHUNK, MOE_W_CHUNK), :]
        return pltpu.make_async_copy(src, stage.at[k % 2], sem.at[k % 2])

    def convert(e, k, slot):
        chunk_copy(e, k).wait()
        v = stage[k % 2].astype(BF16)
        if k < MOE_UP_CHUNKS:
            wub[slot, k * MOE_W_CHUNK:(k + 1) * MOE_W_CHUNK, :] = v
        else:
            kd = k - MOE_UP_CHUNKS
            wdb[slot, kd * MOE_W_CHUNK:(kd + 1) * MOE_W_CHUNK, :] = v
        if k + 2 < MOE_CHUNKS:
            chunk_copy(e, k + 2).start()

    def begin(e):
        chunk_copy(e, 0).start()
        chunk_copy(e, 1).start()
        st[1] = 0

    @pl.when(r < nu_ref[0])
    def _():
        e_cur = te_ref[r]
        e_nxt = nxt_ref[r]

        @pl.when(r == 0)
        def _():
            st[0] = 1
            begin(e_cur)

        @pl.when(first_ref[r] == 1)
        def _():
            slot = 1 - st[0]
            done = st[1]
            for k in range(MOE_CHUNKS):
                @pl.when(k >= done)
                def _():
                    convert(e_cur, k, slot)
            st[0] = slot
            st[1] = MOE_CHUNKS

            @pl.when(e_nxt >= 0)
            def _():
                begin(e_nxt)

        cur = st[0]
        x = xs_ref[...].astype(BF16)
        hc = jnp.dot(x, wub[cur], preferred_element_type=F32)
        a = hc[:, :DFF]
        b = hc[:, DFF:]
        act = (a * _sigmoid(a) * b).astype(BF16)
        ys_ref[...] = jnp.dot(act, wdb[cur], preferred_element_type=F32)

        @pl.when(e_nxt >= 0)
        def _():
            done = st[1]
            share = (MOE_CHUNKS - done + rem_ref[r] - 1) // rem_ref[r]
            for k in range(MOE_CHUNKS):
                @pl.when(jnp.logical_and(k >= done, k < done + share))
                def _():
                    convert(e_nxt, k, 1 - cur)
            st[1] = done + share

    @pl.when(r >= nu_ref[0])
    def _():
        ys_ref[...] = jnp.zeros_like(ys_ref)


def _moe(plan, xs, wu, wd, layer):
    def row(r, te, nu, *_):
        return (jnp.minimum(r, nu[0] - 1), 0)

    return pl.pallas_call(
        functools.partial(_moe_body, layer=layer),
        out_shape=jax.ShapeDtypeStruct(xs.shape, F32),
        grid_spec=pltpu.PrefetchScalarGridSpec(
            num_scalar_prefetch=5,
            grid=(xs.shape[0] // TM_E,),
            in_specs=[
                pl.BlockSpec((TM_E, D), row),
                pl.BlockSpec(memory_space=pl.ANY),
                pl.BlockSpec(memory_space=pl.ANY),
            ],
            out_specs=pl.BlockSpec((TM_E, D), lambda r, *_: (r, 0)),
            scratch_shapes=[
                pltpu.VMEM((2, D, 2 * DFF), BF16),
                pltpu.VMEM((2, DFF, D), BF16),
                pltpu.VMEM((2, MOE_W_CHUNK, D), F32),
                pltpu.SemaphoreType.DMA((2,)),
                pltpu.SMEM((2,), I32),
            ],
        ),
        compiler_params=_cparams(("arbitrary",), 56),
        name="moe",
    )(*plan, xs, wu, wd)


def _combine_body(pos0_ref, pos1_ref, posn0_ref, posn1_ref, ys_ref, x_ref, w_ref, mod_ref, g_ref, nmod_ref,
                  *rest, final):
    tm = TM_TOK
    outs, (ybuf, sem) = rest[:-2], rest[-2:]
    i = pl.program_id(0)
    slot = i % 2

    def fetch(p0_ref, p1_ref, s):
        def start_row(r):
            _row_copy(ys_ref, p0_ref[0, r], ybuf.at[s, 0], r, sem.at[s]).start()
            _row_copy(ys_ref, p1_ref[0, r], ybuf.at[s, 1], r, sem.at[s]).start()

        _issue_rows(tm, start_row)

    @pl.when(i == 0)
    def _():
        fetch(pos0_ref, pos1_ref, 0)

    @pl.when(i + 1 < pl.num_programs(0))
    def _():
        fetch(posn0_ref, posn1_ref, 1 - slot)

    for k in range(2):
        pltpu.make_async_copy(ys_ref.at[pl.ds(0, tm)], ybuf.at[slot, k], sem.at[slot]).wait()

    reps = D // LANES
    w0 = jnp.tile(w_ref[:, 0:LANES], (1, reps))
    w1 = jnp.tile(w_ref[:, LANES:2 * LANES], (1, reps))
    xn = x_ref[...] + mod_ref[5:6, :] * (w0 * ybuf[slot, 0] + w1 * ybuf[slot, 1])
    if final:
        outs[0][...] = _rms(xn, g_ref[...])
    else:
        outs[0][...] = xn
        outs[1][...] = (_rms(xn, g_ref[...]) * (1.0 + nmod_ref[1:2, :]) + nmod_ref[0:1, :]).astype(BF16)


def _combine(pos, ys, x, wlanes, mods, layer, n_rows, g_next, final):
    tm = TM_TOK
    nt = n_rows // tm
    next_layer = min(layer + 1, DEPTH - 1)
    tile = pl.BlockSpec((tm, D), lambda i: (i, 0))
    out_shape = [jax.ShapeDtypeStruct((n_rows, D), F32)]
    if not final:
        out_shape.append(jax.ShapeDtypeStruct((n_rows, D), BF16))
    return pl.pallas_call(
        functools.partial(_combine_body, final=final),
        out_shape=out_shape,
        grid=(nt,),
        in_specs=_pos_specs(tm, lambda i: i) + _pos_specs(tm, lambda i: jnp.minimum(i + 1, nt - 1)) + [
            pl.BlockSpec(memory_space=pl.ANY),
            tile,
            pl.BlockSpec((tm, 2 * LANES), lambda i: (i, 0)),
            pl.BlockSpec((None, None, 6, D), lambda i: (layer, _mod_row(i, tm), 0, 0)),
            pl.BlockSpec((1, D), lambda i: (0, 0)),
            pl.BlockSpec((None, None, 6, D), lambda i: (next_layer, _mod_row(i, tm), 0, 0)),
        ],
        out_specs=[tile] * len(out_shape),
        scratch_shapes=[pltpu.VMEM((2, 2, tm, D), F32), pltpu.SemaphoreType.DMA((2,))],
        compiler_params=_cparams(("arbitrary",), 40),
        name="combine",
    )(pos, pos, pos, pos, ys, x, wlanes, mods, g_next.reshape(1, D), mods)


def _route_plan(route, cnt, n_rows):
    counts = cnt[:, 0].astype(I32)
    padded = ((counts + TM_E - 1) // TM_E) * TM_E
    ends = jnp.cumsum(padded)
    offs = ends - padded
    e01 = route[0:2].astype(I32)
    eids = jnp.arange(NE, dtype=I32)[:, None, None]
    off01 = jnp.sum(jnp.where(e01[None] == eids, offs[:, None, None], 0), axis=0)
    pos = off01 + route[2:4].astype(I32)
    pos = pos.reshape(2, n_rows // TM_TOK, 1, TM_TOK)
    nt = _n_expert_tiles(n_rows)
    tile_start = jnp.arange(nt, dtype=I32) * TM_E
    tile_expert = jnp.minimum(jnp.sum((tile_start[:, None] >= ends[None, :]).astype(I32), axis=1), NE - 1)
    n_used = (ends[-1] // TM_E).reshape(1).astype(I32)
    spare = n_used[0] + jnp.arange(NE, dtype=I32)
    zstart = jnp.concatenate([offs + (counts // TM_E) * TM_E, jnp.minimum(spare, nt - 1) * TM_E])
    zneed = jnp.concatenate([counts % TM_E != 0, spare < nt]).astype(I32)
    wl = jnp.concatenate([jnp.broadcast_to(route[4][:, None], (n_rows, LANES)),
                          jnp.broadcast_to(route[5][:, None], (n_rows, LANES))], axis=1)
    ntile = padded // TM_E
    ecol = jnp.arange(NE, dtype=I32)
    onehot = tile_expert[:, None] == ecol[None, :]
    pick = lambda v: jnp.sum(jnp.where(onehot, v[None, :], 0), axis=1)
    j_in = jnp.arange(nt, dtype=I32) - pick(offs // TM_E)
    first = (j_in == 0).astype(I32)
    rem = jnp.maximum(pick(ntile) - j_in, 1)
    later = (ecol[None, :] > ecol[:, None]) & (ntile[None, :] > 0)
    nxt_e = jnp.min(jnp.where(later, ecol[None, :], NE), axis=1)
    nxt = pick(jnp.where(nxt_e < NE, nxt_e, -1))
    moe_plan = (tile_expert.astype(I32), n_used, first, nxt.astype(I32), rem.astype(I32))
    return pos, moe_plan, zstart.astype(I32), zneed, wl


def kernel(x, c, ctx, c_ctx, ada_w, ada_b, norm1_g, norm2_g, w_in, attn_sink, sgu_ln_g, sgu_ln_b, sgu_w, sgu_b,
           sconv_w, conf_dw_w, conf_dw_b, conf_ln_g, conf_ln_b, w_branch, w_out, router_w, router_b,
           exp_w_up, exp_w_down, final_g):
    cvec = jnp.concatenate([c, c_ctx[None, :], jnp.zeros((8 - NB - 1, D), F32)], axis=0)
    mods = _ada(cvec, ada_w, ada_b).reshape(DEPTH, 8, 6, D)
    xa, cxa = x.reshape(TX, D), ctx.reshape(TC, D)
    cos, sin = _rope_tables()
    consts = _rope_constants()
    rw = router_w.T.astype(BF16)
    wb_all = w_branch.astype(BF16)
    wo_all = w_out.astype(BF16)

    for l in range(DEPTH):
        last = l == DEPTH - 1
        n_rows = TX if last else T
        n_x_tiles = TX // TM_TOK
        lp = (sgu_ln_g[l], sgu_ln_b[l], sgu_w[l], sgu_b[l], sconv_w[l], conf_dw_w[l], conf_dw_b[l],
              conf_ln_g[l], conf_ln_b[l])
        c_rows0 = 0 if l == 0 else TX

        if l == 0:
            h = _norm_mod(xa, cxa, norm1_g[l], mods, l)
        if not last:
            p = _inproj(h, w_in, l, T, N_IN // TN_IN)
            qr, kr, vr = _prep(p, T, n_x_tiles, cos, sin, consts)
            ya = _attn(qr, kr, vr, kr, vr, TX // LC, attn_sink[l])
            yac = _ctx_attn(qr, kr, vr, attn_sink[l])
        else:
            p = _inproj(h, w_in, l, TX, N_IN // TN_IN)
            qr, kr, vr = _prep(p, TX, n_x_tiles, cos, sin, consts)
            kc, vc = _ctx_kv(h, w_in, l, consts[1])
            ya = _attn(qr, kr, vr, kc, vc, 0, attn_sink[l])
            yac = ya
        yr = _mixers(p, n_rows, n_x_tiles, lp)
        xn, h2, lg = _post(p, ya, yac, yr, wb_all, wo_all, xa, cxa, c_rows0 // TM_TOK, mods, l,
                           norm2_g[l], rw, n_rows)
        route, cnt = _route(lg, router_b, n_rows)
        pos, moe_plan, zstart, zneed, wl = _route_plan(route, cnt, n_rows)
        xs = _dispatch(zstart, zneed, pos, h2, n_rows)
        ys = _moe(moe_plan, xs, exp_w_up, exp_w_down, l)
        if last:
            (out,) = _combine(pos, ys, xn, wl, mods, l, n_rows, final_g, True)
        else:
            xa, h = _combine(pos, ys, xn, wl, mods, l, n_rows, norm1_g[l + 1], False)
            cxa = xa

    return out.reshape(NB, S, D)
```

```python
import functools

import numpy as np
import jax
import jax.numpy as jnp
from jax import lax
from jax.experimental import pallas as pl
from jax.experimental.pallas import tpu as pltpu

F32 = jnp.float32
BF16 = jnp.bfloat16
I32 = jnp.int32

D = 2048
NB = 4
S = 2048
LC = 256
DEPTH = 2
GRID_W = 64
BW = 512
HD = 64
NH = 8
NKV = 2
REP = NH // NKV
WINDOW = 128
QB = 128
ROPE_THETA = 10000.0
SGU_CHUNK = 128
SGU_GROUPS = 4
SCONV_K = 3
CONF_K = 31
NE = 16
NG = 4
EPG = NE // NG
DFF = D // 2
N_IN = BW + 2 * NKV * HD + 7 * BW + 4 * D
TX = NB * S
TC = NB * LC
T = TX + TC

OFF_KV = BW
OFF_SU, OFF_SV, OFF_CB, OFF_CC, OFF_CX, OFF_GA, OFF_GB, OFF_GATES = 768, 1280, 1792, 2304, 2816, 3328, 3840, 4352

LANES = 128
SUBLANES = 8
MIB = 1024 * 1024

TM_NORM = 512
TM_IN = 1024
INPROJ_ROW_CHUNK = 512
IN_W_CHUNK = 256
IN_CHUNKS = D // IN_W_CHUNK
TN_IN = 1792
TM_TOK = 256
TM_PREP = 512
TM_E = 256
HALO = 16
ROW_UNROLL = 8
MOE_W_CHUNK = 512
MOE_UP_CHUNKS = D // MOE_W_CHUNK
MOE_CHUNKS = MOE_UP_CHUNKS + DFF // MOE_W_CHUNK


def _n_expert_tiles(n_rows):
    return (2 * n_rows) // TM_E + NE


def _cparams(sem, vmem_mib):
    return pltpu.CompilerParams(dimension_semantics=sem, vmem_limit_bytes=vmem_mib * MIB)


def _sigmoid(x):
    return 0.5 * jnp.tanh(0.5 * x) + 0.5


def _gelu_tanh(x):
    c = np.float32(np.sqrt(2.0 / np.pi))
    return 0.5 * x * (1.0 + jnp.tanh(c * (x + np.float32(0.044715) * (x * x * x))))


def _layernorm(x, g, b, eps=1e-5):
    mu = jnp.mean(x, axis=-1, keepdims=True)
    xc = x - mu
    var = jnp.mean(xc * xc, axis=-1, keepdims=True)
    return xc * lax.rsqrt(var + eps) * g + b


def _mod_row(i, tm):
    return jnp.where(i < TX // tm, (i * tm) // S, NB)


def _ada_body(c_ref, w_ref, b_ref, o_ref):
    c = c_ref[...]
    s = (c * _sigmoid(c)).astype(BF16)
    o_ref[...] = jnp.dot(s, w_ref[...].astype(BF16), preferred_element_type=F32) + b_ref[...]


def _ada(cvec, ada_w, ada_b):
    tn = 1024
    return pl.pallas_call(
        _ada_body,
        out_shape=jax.ShapeDtypeStruct((DEPTH, 8, 6 * D), F32),
        grid=(DEPTH, 6 * D // tn),
        in_specs=[
            pl.BlockSpec((8, D), lambda l, j: (0, 0)),
            pl.BlockSpec((None, D, tn), lambda l, j: (l, 0, j)),
            pl.BlockSpec((None, 1, tn), lambda l, j: (l, 0, j)),
        ],
        out_specs=pl.BlockSpec((None, 8, tn), lambda l, j: (l, 0, j)),
        compiler_params=_cparams(("arbitrary", "arbitrary"), 40),
        name="ada",
    )(cvec, ada_w, ada_b.reshape(DEPTH, 1, 6 * D))


def _stream_specs(tm, c_blk0, width=D):
    nx = TX // tm
    return [pl.BlockSpec((tm, width), lambda i, *_: (jnp.minimum(i, nx - 1), 0)),
            pl.BlockSpec((tm, width), lambda i, *_: (c_blk0 + jnp.maximum(i - nx, 0), 0))]


def _stream_tile(x_ref, c_ref, tm):
    return jnp.where(pl.program_id(0) < TX // tm, x_ref[...], c_ref[...])


def _rms(x, g):
    return x * lax.rsqrt(jnp.mean(x * x, axis=-1, keepdims=True) + 1e-6) * g


def _norm_mod_body(x_ref, c_ref, g_ref, mod_ref, o_ref):
    y = _rms(_stream_tile(x_ref, c_ref, TM_NORM), g_ref[...])
    o_ref[...] = (y * (1.0 + mod_ref[1:2, :]) + mod_ref[0:1, :]).astype(o_ref.dtype)


def _norm_mod(x, cx, g, mods, layer):
    tm = TM_NORM
    return pl.pallas_call(
        _norm_mod_body,
        out_shape=jax.ShapeDtypeStruct((T, D), BF16),
        grid=(T // tm,),
        in_specs=_stream_specs(tm, 0) + [
            pl.BlockSpec((1, D), lambda i: (0, 0)),
            pl.BlockSpec((None, None, 6, D), lambda i: (layer, _mod_row(i, tm), 0, 0)),
        ],
        out_specs=pl.BlockSpec((tm, D), lambda i: (i, 0)),
        compiler_params=_cparams(("arbitrary",), 32),
        name="norm_mod",
    )(x, cx, g.reshape(1, D), mods)


def _inproj_body(h_ref, w_hbm, o_ref, wbf, stage, sem, st, *, layer):
    j = pl.program_id(0)
    i = pl.program_id(1)
    nj = pl.num_programs(0)
    ni = pl.num_programs(1)

    def chunk_copy(jj, k):
        cols = pl.ds(pl.multiple_of(jj * TN_IN, LANES), TN_IN)
        src = w_hbm.at[layer, pl.ds(k * IN_W_CHUNK, IN_W_CHUNK), cols]
        return pltpu.make_async_copy(src, stage.at[k % 2], sem.at[k % 2])

    def convert(jj, k, slot):
        chunk_copy(jj, k).wait()
        wbf[slot, k * IN_W_CHUNK:(k + 1) * IN_W_CHUNK, :] = stage[k % 2].astype(BF16)
        if k + 2 < IN_CHUNKS:
            chunk_copy(jj, k + 2).start()

    def begin(jj):
        chunk_copy(jj, 0).start()
        chunk_copy(jj, 1).start()
        st[1] = 0

    @pl.when(jnp.logical_and(i == 0, j == 0))
    def _():
        st[0] = 1
        begin(0)

    @pl.when(i == 0)
    def _():
        slot = 1 - st[0]
        done = st[1]
        for k in range(IN_CHUNKS):
            @pl.when(k >= done)
            def _():
                convert(j, k, slot)
        st[0] = slot
        st[1] = IN_CHUNKS

        @pl.when(j + 1 < nj)
        def _():
            begin(j + 1)

    cur = st[0]
    col0 = j * TN_IN
    rows = INPROJ_ROW_CHUNK

    def run(epilogue):
        for c in range(h_ref.shape[0] // rows):
            sl = slice(c * rows, (c + 1) * rows)
            acc = jnp.dot(h_ref[sl, :], wbf[cur], preferred_element_type=F32)
            o_ref[sl, :] = epilogue(acc).astype(BF16)

    @pl.when(col0 + TN_IN <= OFF_GB)
    def _():
        run(lambda acc: acc)

    @pl.when(col0 >= OFF_GB)
    def _():
        run(_sigmoid)

    @pl.when(jnp.logical_and(col0 < OFF_GB, col0 + TN_IN > OFF_GB))
    def _():
        col = col0 + lax.broadcasted_iota(I32, (rows, TN_IN), 1)
        run(lambda acc: jnp.where(col >= OFF_GB, _sigmoid(acc), acc))

    @pl.when(j + 1 < nj)
    def _():
        done = st[1]
        left = ni - i
        share = (IN_CHUNKS - done + left - 1) // left
        for k in range(IN_CHUNKS):
            @pl.when(jnp.logical_and(k >= done, k < done + share))
            def _():
                convert(j + 1, k, 1 - cur)
        st[1] = done + share


def _inproj(h, w_in, layer, n_rows, n_col_tiles):
    tm = TM_IN
    return pl.pallas_call(
        functools.partial(_inproj_body, layer=layer),
        out_shape=jax.ShapeDtypeStruct((n_rows, n_col_tiles * TN_IN), BF16),
        grid=(n_col_tiles, n_rows // tm),
        in_specs=[
            pl.BlockSpec((tm, D), lambda j, i: (i, 0)),
            pl.BlockSpec(memory_space=pl.ANY),
        ],
        out_specs=pl.BlockSpec((tm, TN_IN), lambda j, i: (i, j)),
        scratch_shapes=[
            pltpu.VMEM((2, D, TN_IN), BF16),
            pltpu.VMEM((2, IN_W_CHUNK, TN_IN), F32),
            pltpu.SemaphoreType.DMA((2,)),
            pltpu.SMEM((2,), I32),
        ],
        compiler_params=_cparams(("arbitrary", "arbitrary"), 56),
        name="inproj",
    )(h, w_in)


def _rope_constants():
    rh = np.zeros((HD, HD), np.float32)
    for base in (0, 32):
        for d in range(16):
            rh[base + 16 + d, base + d] = -1.0
            rh[base + d, base + 16 + d] = 1.0
    rq = np.kron(np.eye(NH, dtype=np.float32), rh)
    rk = np.kron(np.eye(NKV, dtype=np.float32), rh)
    rep = np.zeros((NKV * HD, NH * HD), np.float32)
    for g in range(NKV):
        for r in range(REP):
            for d in range(HD):
                rep[g * HD + d, g * REP * HD + r * HD + d] = 1.0
    return jnp.asarray(rq, BF16), jnp.asarray(rep, BF16), jnp.asarray(rk @ rep, BF16)


def _rope_tables():
    half = HD // 2
    inv = 1.0 / (ROPE_THETA ** (jnp.arange(0, half, 2, dtype=F32) / half))
    pos = jnp.arange(S)
    ar = (pos // GRID_W).astype(F32)[:, None] * inv
    ac = (pos % GRID_W).astype(F32)[:, None] * inv
    reps = LANES // HD
    cos = jnp.tile(jnp.concatenate([jnp.cos(ar), jnp.cos(ar), jnp.cos(ac), jnp.cos(ac)], axis=1), (1, reps))
    sin = jnp.tile(jnp.concatenate([jnp.sin(ar), jnp.sin(ar), jnp.sin(ac), jnp.sin(ac)], axis=1), (1, reps))
    cos = jnp.concatenate([cos, jnp.ones((TM_PREP, LANES), F32)], axis=0)
    sin = jnp.concatenate([sin, jnp.zeros((TM_PREP, LANES), F32)], axis=0)
    return cos, sin


def _prep_body(q_ref, kv_ref, cos_ref, sin_ref, rq_ref, rep_ref, rrep_ref, qo_ref, ko_ref, vo_ref):
    cos = jnp.tile(cos_ref[...], (1, NH * HD // LANES))
    sin = jnp.tile(sin_ref[...], (1, NH * HD // LANES))
    q = q_ref[...]
    qs = jnp.dot(q, rq_ref[...], preferred_element_type=F32)
    qo_ref[...] = ((q.astype(F32) * cos + qs * sin) * (HD ** -0.5)).astype(BF16)
    k = kv_ref[:, 0:NKV * HD]
    v = kv_ref[:, NKV * HD:2 * NKV * HD]
    kr = jnp.dot(k, rep_ref[...], preferred_element_type=F32)
    ks = jnp.dot(k, rrep_ref[...], preferred_element_type=F32)
    ko_ref[...] = (kr * cos + ks * sin).astype(BF16)
    vo_ref[...] = jnp.dot(v, rep_ref[...], preferred_element_type=F32).astype(BF16)


def _prep(p, n_rows, cos, sin, consts):
    tm = TM_PREP
    rq, rep, rrep = consts
    tps = S // tm

    def tab(i):
        return (jnp.where(i < TX // tm, i % tps, tps), 0)

    w = NH * HD
    return pl.pallas_call(
        _prep_body,
        out_shape=[jax.ShapeDtypeStruct((n_rows, w), BF16)] * 3,
        grid=(n_rows // tm,),
        in_specs=[
            pl.BlockSpec((tm, w), lambda i: (i, 0)),
            pl.BlockSpec((tm, 2 * NKV * HD), lambda i: (i, OFF_KV // (2 * NKV * HD))),
            pl.BlockSpec((tm, LANES), tab),
            pl.BlockSpec((tm, LANES), tab),
            pl.BlockSpec((w, w), lambda i: (0, 0)),
            pl.BlockSpec((NKV * HD, w), lambda i: (0, 0)),
            pl.BlockSpec((NKV * HD, w), lambda i: (0, 0)),
        ],
        out_specs=[pl.BlockSpec((tm, w), lambda i: (i, 0))] * 3,
        compiler_params=_cparams(("arbitrary",), 32),
        name="prep",
    )(p, p, cos, sin, rq, rep, rrep)


def _ctx_kv_body(h_ref, w_ref, rep_ref, ko_ref, vo_ref):
    kv = jnp.dot(h_ref[...], w_ref[...].astype(BF16), preferred_element_type=F32).astype(BF16)
    ko_ref[...] = jnp.dot(kv[:, 0:NKV * HD], rep_ref[...], preferred_element_type=F32).astype(BF16)
    vo_ref[...] = jnp.dot(kv[:, NKV * HD:2 * NKV * HD], rep_ref[...], preferred_element_type=F32).astype(BF16)


def _ctx_kv(h, w_in, layer, rep):
    tm = TM_TOK
    w = NH * HD
    kvw = 2 * NKV * HD
    return pl.pallas_call(
        _ctx_kv_body,
        out_shape=[jax.ShapeDtypeStruct((TC, w), BF16)] * 2,
        grid=(TC // tm,),
        in_specs=[
            pl.BlockSpec((tm, D), lambda i: (TX // tm + i, 0)),
            pl.BlockSpec((None, pl.Element(D), pl.Element(kvw)), lambda i: (layer, 0, OFF_KV)),
            pl.BlockSpec((NKV * HD, w), lambda i: (0, 0)),
        ],
        out_specs=[pl.BlockSpec((tm, w), lambda i: (i, 0))] * 2,
        compiler_params=_cparams(("arbitrary",), 32),
        name="ctx_kv",
    )(h, w_in, rep)


def _attn_groups(q, kbs, vbs, sink_ref, valid, nq):
    gw = REP * HD
    lane_head = lax.broadcasted_iota(I32, (nq, gw), 1) // HD
    row_head = lax.broadcasted_iota(I32, (REP * nq, 1), 0) // nq
    valid_rep = None if valid is None else jnp.concatenate([valid] * REP, axis=0)
    scores = []
    for g in range(NKV):
        qg = q[:, g * gw:(g + 1) * gw]
        qs = jnp.concatenate([jnp.where(lane_head == r, qg, jnp.zeros_like(qg)) for r in range(REP)], axis=0)
        s = lax.dot_general(qs, kbs[g], (((1,), (1,)), ((), ())), preferred_element_type=F32)
        scores.append(s if valid_rep is None else jnp.where(valid_rep, s, -jnp.inf))
    probs = []
    for g in range(NKV):
        s = scores[g]
        sink = jnp.zeros((REP * nq, 1), F32)
        for r in range(REP):
            sink = jnp.where(row_head == r, sink_ref[g * REP + r], sink)
        m = jnp.maximum(jnp.max(s, axis=-1, keepdims=True), sink)
        e = jnp.exp(s - m)
        den = jnp.sum(e, axis=-1, keepdims=True) + jnp.exp(sink - m)
        probs.append((e * (1.0 / den)).astype(BF16))
    outs = []
    for g in range(NKV):
        o = jnp.dot(probs[g], vbs[g], preferred_element_type=F32)
        og = jnp.zeros((nq, gw), F32)
        for r in range(REP):
            og = og + jnp.where(lane_head == r, o[r * nq:(r + 1) * nq, :], 0.0)
        outs.append(og)
    return jnp.concatenate(outs, axis=1).astype(BF16)


def _attn_body(sink_ref, q_ref, kp_ref, kc_ref, kn_ref, vp_ref, vc_ref, vn_ref, kx_ref, vx_ref, o_ref):
    n = pl.program_id(1)
    nblk = S // QB
    nk = 3 * QB + LC
    row = lax.broadcasted_iota(I32, (QB, nk), 0)
    col = lax.broadcasted_iota(I32, (QB, nk), 1)
    lo = jnp.where(n == 0, QB, 0)
    hi = jnp.where(n == nblk - 1, 2 * QB, 3 * QB)
    band = (col >= row) & (col <= row + 2 * WINDOW) & (col >= lo) & (col < hi)
    valid = band | (col >= 3 * QB)
    gw = REP * HD
    kbs, vbs = [], []
    for g in range(NKV):
        sl = slice(g * gw, (g + 1) * gw)
        kbs.append(jnp.concatenate([kp_ref[:, sl], kc_ref[:, sl], kn_ref[:, sl], kx_ref[:, sl]], axis=0))
        vbs.append(jnp.concatenate([vp_ref[:, sl], vc_ref[:, sl], vn_ref[:, sl], vx_ref[:, sl]], axis=0))
    o_ref[...] = _attn_groups(q_ref[...], kbs, vbs, sink_ref, valid, QB)


def _attn(qr, kr, vr, kc_arr, vc_arr, ctx_blk0, sink):
    nblk = S // QB
    w = NH * HD

    def cur(b, n):
        return (b * nblk + n, 0)

    def prev(b, n):
        return (b * nblk + jnp.maximum(n - 1, 0), 0)

    def nxt(b, n):
        return (b * nblk + jnp.minimum(n + 1, nblk - 1), 0)

    def cx(b, n):
        return (ctx_blk0 + b, 0)

    blk = lambda f: pl.BlockSpec((QB, w), f)
    return pl.pallas_call(
        _attn_body,
        out_shape=jax.ShapeDtypeStruct((TX, w), BF16),
        grid=(NB, nblk),
        in_specs=[
            pl.BlockSpec(memory_space=pltpu.SMEM),
            blk(cur), blk(prev), blk(cur), blk(nxt), blk(prev), blk(cur), blk(nxt),
            pl.BlockSpec((LC, w), cx), pl.BlockSpec((LC, w), cx),
        ],
        out_specs=blk(cur),
        compiler_params=_cparams(("arbitrary", "arbitrary"), 32),
        name="window_attn",
    )(sink, qr, kr, kr, kr, vr, vr, vr, kc_arr, vc_arr)


def _ctx_attn_body(sink_ref, q_ref, kx_ref, vx_ref, o_ref):
    gw = REP * HD
    kbs = [kx_ref[:, g * gw:(g + 1) * gw] for g in range(NKV)]
    vbs = [vx_ref[:, g * gw:(g + 1) * gw] for g in range(NKV)]
    o_ref[...] = _attn_groups(q_ref[...], kbs, vbs, sink_ref, None, LC)


def _ctx_attn(qr, kr, vr, sink):
    w = NH * HD
    blk0 = TX // LC
    spec = pl.BlockSpec((LC, w), lambda b: (blk0 + b, 0))
    return pl.pallas_call(
        _ctx_attn_body,
        out_shape=jax.ShapeDtypeStruct((TC, w), BF16),
        grid=(NB,),
        in_specs=[pl.BlockSpec(memory_space=pltpu.SMEM), spec, spec, spec],
        out_specs=pl.BlockSpec((LC, w), lambda b: (b, 0)),
        compiler_params=_cparams(("arbitrary",), 32),
        name="ctx_attn",
    )(sink, qr, kr, vr)


def _mix_body(su_ref, sv_ref, cb_ref, cc_ref, cx_ref, ga_ref, gb_ref,
              ccp_ref, cxp_ref, gap_ref, gbp_ref, ccn_ref, cxn_ref, gan_ref, gbn_ref,
              lng_ref, lnb_ref, sw_ref, sbias_ref, scw_ref, dww_ref, dwb_ref, clg_ref, clb_ref,
              o_ref, m_scr, z_scr, *, n_x_tiles):
    tm = TM_TOK
    i = pl.program_id(0)
    tps = S // tm
    is_x = i < n_x_tiles
    first = jnp.logical_or(jnp.logical_not(is_x), (i % tps) == 0)
    last = jnp.logical_or(jnp.logical_not(is_x), (i % tps) == tps - 1)
    keep_p = jnp.where(first, 0.0, 1.0)
    keep_n = jnp.where(last, 0.0, 1.0)

    u = _gelu_tanh(su_ref[...].astype(F32))
    v = _layernorm(_gelu_tanh(sv_ref[...].astype(F32)), lng_ref[...], lnb_ref[...]).astype(BF16)
    gwid = BW // SGU_GROUPS
    chunks = []
    for c in range(tm // SGU_CHUNK):
        parts = []
        for g in range(SGU_GROUPS):
            vc = v[c * SGU_CHUNK:(c + 1) * SGU_CHUNK, g * gwid:(g + 1) * gwid]
            parts.append(jnp.dot(sw_ref[g], vc, preferred_element_type=F32))
        chunks.append(jnp.concatenate(parts, axis=1) + sbias_ref[...])
    yb = u * jnp.concatenate(chunks, axis=0)

    m_scr[HALO:HALO + tm, :] = cc_ref[...].astype(F32) * cx_ref[...].astype(F32)
    m_scr[0:HALO, :] = ccp_ref[...].astype(F32) * cxp_ref[...].astype(F32) * keep_p
    m_scr[HALO + tm:, :] = ccn_ref[...].astype(F32) * cxn_ref[...].astype(F32) * keep_n
    z = jnp.zeros((tm, BW), F32)
    for k in range(SCONV_K):
        z = z + scw_ref[k:k + 1, :] * m_scr[pl.ds(HALO - SCONV_K // 2 + k, tm), :]
    yc = cb_ref[...].astype(F32) * z

    z_scr[0, HALO:HALO + tm, :] = ga_ref[...].astype(F32) * gb_ref[...].astype(F32)
    z_scr[0, 0:HALO, :] = gap_ref[...].astype(F32) * gbp_ref[...].astype(F32) * keep_p
    z_scr[0, HALO + tm:, :] = gan_ref[...].astype(F32) * gbn_ref[...].astype(F32) * keep_n
    n_sh = tm + 2 * HALO - SUBLANES
    for s in range(1, SUBLANES):
        z_scr[s, 0:n_sh, :] = z_scr[0, pl.ds(s, n_sh), :]
    acc = jnp.zeros((tm, BW), F32) + dwb_ref[...]
    for k in range(CONF_K):
        off = HALO - CONF_K // 2 + k
        base = off - off % SUBLANES
        acc = acc + dww_ref[k:k + 1, :] * z_scr[off % SUBLANES, base:base + tm, :]
    zn = _layernorm(acc, clg_ref[...], clb_ref[...])
    yd = zn * _sigmoid(zn)

    o_ref[...] = jnp.concatenate([yb, yc, yd], axis=1).astype(BF16)


def _mixers(p, n_rows, n_x_tiles, lp):
    tm = TM_TOK
    (sgu_ln_g, sgu_ln_b, sgu_w, sgu_b, sconv_w, conf_dw_w, conf_dw_b, conf_ln_g, conf_ln_b) = lp

    def sec(off):
        return pl.BlockSpec((pl.Element(tm), pl.Element(BW)), lambda i: (i * tm, off))

    hpt = tm // HALO

    def halo_prev(off):
        return pl.BlockSpec((pl.Element(HALO), pl.Element(BW)),
                            lambda i: (jnp.maximum(i * hpt - 1, 0) * HALO, off))

    def halo_next(off):
        return pl.BlockSpec((pl.Element(HALO), pl.Element(BW)),
                            lambda i: (jnp.minimum(i * hpt + hpt, n_rows // HALO - 1) * HALO, off))

    def const(shape):
        return pl.BlockSpec(shape, lambda i: (0,) * len(shape))

    sbias = jnp.repeat(sgu_b.T, BW // SGU_GROUPS, axis=1)
    row = lambda a: a.reshape(1, BW)
    conv_offs = (OFF_CC, OFF_CX, OFF_GA, OFF_GB)
    return pl.pallas_call(
        functools.partial(_mix_body, n_x_tiles=n_x_tiles),
        out_shape=jax.ShapeDtypeStruct((n_rows, 3 * BW), BF16),
        grid=(n_rows // tm,),
        in_specs=[sec(o) for o in (OFF_SU, OFF_SV, OFF_CB, OFF_CC, OFF_CX, OFF_GA, OFF_GB)]
        + [halo_prev(o) for o in conv_offs] + [halo_next(o) for o in conv_offs]
        + [const((1, BW)), const((1, BW)), const((SGU_GROUPS, SGU_CHUNK, SGU_CHUNK)), const((SGU_CHUNK, BW)),
           const((SCONV_K, BW)), const((CONF_K, BW)), const((1, BW)), const((1, BW)), const((1, BW))],
        out_specs=pl.BlockSpec((tm, 3 * BW), lambda i: (i, 0)),
        scratch_shapes=[pltpu.VMEM((tm + 2 * HALO, BW), F32), pltpu.VMEM((SUBLANES, tm + 2 * HALO, BW), F32)],
        compiler_params=_cparams(("arbitrary",), 32),
        name="mixers",
    )(*([p] * 15), row(sgu_ln_g), row(sgu_ln_b), sgu_w.astype(BF16), sbias, sconv_w, conf_dw_w,
      row(conf_dw_b), row(conf_ln_g), row(conf_ln_b))


def _norm2(xn, n2g, mod):
    return _rms(xn, n2g) * (1.0 + mod[4:5, :]) + mod[3:4, :]


def _post_body(g_ref, ya_ref, yac_ref, yr_ref, wb_ref, wo_ref, x_ref, cx_ref, mod_ref, n2g_ref, rw_ref,
               xo_ref, h2_ref, lg_ref):
    tm = TM_TOK
    merged = None
    for br in range(4):
        y = _stream_tile(ya_ref, yac_ref, tm) if br == 0 else yr_ref[:, (br - 1) * BW:br * BW]
        pr = jnp.dot(y, wb_ref[br], preferred_element_type=F32)
        term = g_ref[:, br * D:(br + 1) * D].astype(F32) * pr
        merged = term if merged is None else merged + term
    out = jnp.dot(merged.astype(BF16), wo_ref[...], preferred_element_type=F32)
    xn = _stream_tile(x_ref, cx_ref, tm) + mod_ref[2:3, :] * out
    xo_ref[...] = xn
    h2 = _norm2(xn, n2g_ref[...], mod_ref)
    h2_ref[...] = h2
    nt = (((1,), (1,)), ((), ()))
    lg_ref[...] = lax.dot_general(rw_ref[...], h2.astype(BF16), nt, preferred_element_type=F32)


def _post(p, ya, yac, yr, wb, wo, x, cx, c_blk0, mods, layer, n2g, rw, n_rows):
    tm = TM_TOK
    one = pl.Buffered(1)
    return pl.pallas_call(
        _post_body,
        out_shape=[jax.ShapeDtypeStruct((n_rows, D), F32), jax.ShapeDtypeStruct((n_rows, D), F32),
                   jax.ShapeDtypeStruct((NE, n_rows), F32)],
        grid=(n_rows // tm,),
        in_specs=[pl.BlockSpec((pl.Element(tm), pl.Element(4 * D)), lambda i: (i * tm, OFF_GATES))]
        + _stream_specs(tm, 0, BW) + [
            pl.BlockSpec((tm, 3 * BW), lambda i: (i, 0)),
            pl.BlockSpec((None, 4, BW, D), lambda i: (layer, 0, 0, 0), pipeline_mode=one),
            pl.BlockSpec((None, D, D), lambda i: (layer, 0, 0), pipeline_mode=one)]
        + _stream_specs(tm, c_blk0) + [
            pl.BlockSpec((None, None, 6, D), lambda i: (layer, _mod_row(i, tm), 0, 0)),
            pl.BlockSpec((1, D), lambda i: (0, 0)),
            pl.BlockSpec((NE, D), lambda i: (0, 0)),
        ],
        out_specs=[pl.BlockSpec((tm, D), lambda i: (i, 0)), pl.BlockSpec((tm, D), lambda i: (i, 0)),
                   pl.BlockSpec((NE, tm), lambda i: (0, i))],
        compiler_params=_cparams(("arbitrary",), 52),
        name="post",
    )(p, ya, yac, yr, wb, wo, x, cx, mods, n2g.reshape(1, D), rw)


def _route_body(lg_ref, rb_ref, tri_ref, o_ref, cnt_ref, carry_ref):
    tm = TM_TOK
    i = pl.program_id(0)

    @pl.when(i == 0)
    def _():
        carry_ref[...] = jnp.zeros_like(carry_ref)

    lg = lg_ref[...]
    e = jnp.exp(lg - jnp.max(lg, axis=0, keepdims=True))
    sc = e / jnp.sum(e, axis=0, keepdims=True)
    bi = sc + rb_ref[...]
    b = [bi[k:k + 1, :] for k in range(NE)]
    s = [sc[k:k + 1, :] for k in range(NE)]

    gs = []
    for g in range(NG):
        v = b[g * EPG:(g + 1) * EPG]
        best = None
        for a in range(EPG):
            for c in range(a + 1, EPG):
                ps = v[a] + v[c]
                best = ps if best is None else jnp.maximum(best, ps)
        gs.append(best)
    gsel = jnp.zeros((1, tm), I32)
    gbest = gs[0]
    for g in range(1, NG):
        take = gs[g] > gbest
        gsel = jnp.where(take, g, gsel)
        gbest = jnp.where(take, gs[g], gbest)

    vb, vs = [], []
    for j in range(EPG):
        xb, xs = b[j], s[j]
        for g in range(1, NG):
            xb = jnp.where(gsel == g, b[g * EPG + j], xb)
            xs = jnp.where(gsel == g, s[g * EPG + j], xs)
        vb.append(xb)
        vs.append(xs)
    order = []
    for j in range(EPG):
        c = jnp.zeros((1, tm), I32)
        for m in range(EPG):
            if m == j:
                continue
            ahead = (vb[m] >= vb[j]) if m < j else (vb[m] > vb[j])
            c = c + jnp.where(ahead, 1, 0)
        order.append(c)
    zero = jnp.zeros((1, tm), F32)
    w0 = zero
    w1 = zero
    j0 = jnp.zeros((1, tm), I32)
    j1 = jnp.zeros((1, tm), I32)
    for j in range(EPG):
        w0 = jnp.where(order[j] == 0, vs[j], w0)
        w1 = jnp.where(order[j] == 1, vs[j], w1)
        j0 = jnp.where(order[j] == 0, j, j0)
        j1 = jnp.where(order[j] == 1, j, j1)
    tot = w0 + w1
    e0 = gsel * EPG + j0
    e1 = gsel * EPG + j1

    eid = lax.broadcasted_iota(I32, (NE, tm), 0)
    oh0 = eid == e0
    oh1 = eid == e1
    oh = jnp.where(oh0 | oh1, 1.0, 0.0)
    rank = jnp.dot(oh.astype(BF16), tri_ref[...], preferred_element_type=F32) + carry_ref[:, 0:1]
    r0 = jnp.sum(jnp.where(oh0, rank, 0.0), axis=0, keepdims=True)
    r1 = jnp.sum(jnp.where(oh1, rank, 0.0), axis=0, keepdims=True)
    new_carry = carry_ref[...] + jnp.sum(oh, axis=1, keepdims=True)
    carry_ref[...] = new_carry
    cnt_ref[...] = new_carry

    o_ref[0:1, :] = e0.astype(F32)
    o_ref[1:2, :] = e1.astype(F32)
    o_ref[2:3, :] = r0
    o_ref[3:4, :] = r1
    o_ref[4:5, :] = w0 / tot
    o_ref[5:6, :] = w1 / tot
    o_ref[6:8, :] = jnp.zeros((2, tm), F32)


def _route(lg, router_b, n_rows):
    tm = TM_TOK
    tri = jnp.asarray(np.triu(np.ones((tm, tm), np.float32), 1), BF16)
    return pl.pallas_call(
        _route_body,
        out_shape=[jax.ShapeDtypeStruct((8, n_rows), F32), jax.ShapeDtypeStruct((NE, LANES), F32)],
        grid=(n_rows // tm,),
        in_specs=[
            pl.BlockSpec((NE, tm), lambda i: (0, i)),
            pl.BlockSpec((NE, 1), lambda i: (0, 0)),
            pl.BlockSpec((tm, tm), lambda i: (0, 0)),
        ],
        out_specs=[pl.BlockSpec((8, tm), lambda i: (0, i)), pl.BlockSpec((NE, LANES), lambda i: (0, 0))],
        scratch_shapes=[pltpu.VMEM((NE, LANES), F32)],
        compiler_params=_cparams(("arbitrary",), 32),
        name="route",
    )(lg, router_b.reshape(NE, 1), tri)


def _row_copy(src_ref, src_row, dst_ref, dst_row, sem):
    return pltpu.make_async_copy(src_ref.at[pl.ds(src_row, 1)], dst_ref.at[pl.ds(dst_row, 1)], sem)


def _issue_rows(n, start_row):
    def trip(t, c):
        base = pl.multiple_of(t * ROW_UNROLL, ROW_UNROLL)
        for u in range(ROW_UNROLL):
            start_row(base + u)
        return c

    lax.fori_loop(0, n // ROW_UNROLL, trip, 0)


def _dispatch_body(tail_ref, need_ref, pos0_ref, pos1_ref, h_ref, xs_ref, zero_ref, sem, zsem):
    tm = TM_TOK

    @pl.when(pl.program_id(0) == 0)
    def _():
        zero_ref[...] = jnp.zeros_like(zero_ref)

        def tail_copy(e):
            start = pl.multiple_of(tail_ref[e], TM_E)
            return pltpu.make_async_copy(zero_ref, xs_ref.at[pl.ds(start, TM_E)], zsem)

        for e in range(2 * NE):
            @pl.when(need_ref[e] > 0)
            def _():
                tail_copy(e).start()
        for e in range(2 * NE):
            @pl.when(need_ref[e] > 0)
            def _():
                tail_copy(e).wait()

    def start_row(r):
        _row_copy(h_ref, r, xs_ref, pos0_ref[0, r], sem).start()
        _row_copy(h_ref, r, xs_ref, pos1_ref[0, r], sem).start()

    _issue_rows(tm, start_row)
    for _ in range(2):
        pltpu.make_async_copy(h_ref, xs_ref.at[pl.ds(0, tm)], sem).wait()


def _pos_specs(tm, index):
    return [pl.BlockSpec((None, None, 1, tm), lambda i, *_, k=k: (k, index(i), 0, 0), memory_space=pltpu.SMEM)
            for k in range(2)]


def _dispatch(tail, need, pos, h2, n_rows):
    tm = TM_TOK
    return pl.pallas_call(
        _dispatch_body,
        out_shape=jax.ShapeDtypeStruct((_n_expert_tiles(n_rows) * TM_E, D), F32),
        grid_spec=pltpu.PrefetchScalarGridSpec(
            num_scalar_prefetch=2,
            grid=(n_rows // tm,),
            in_specs=_pos_specs(tm, lambda i: i) + [pl.BlockSpec((tm, D), lambda i, t, n: (i, 0))],
            out_specs=pl.BlockSpec(memory_space=pl.ANY),
            scratch_shapes=[pltpu.VMEM((TM_E, D), F32), pltpu.SemaphoreType.DMA(()), pltpu.SemaphoreType.DMA(())],
        ),
        compiler_params=_cparams(("arbitrary",), 32),
        name="dispatch",
    )(tail, need, pos, pos, h2)


def _moe_body(te_ref, nu_ref, first_ref, nxt_ref, rem_ref, xs_ref, wu_hbm, wd_hbm, ys_ref,
              wub, wdb, stage, sem, st, *, layer):
    r = pl.program_id(0)

    def chunk_copy(e, k):
        if k < MOE_UP_CHUNKS:
            src = wu_hbm.at[layer, e, pl.ds(k * MOE_W_CHUNK, MOE_W_CHUNK), :]
        else:
            src = wd_hbm.at[layer, e, pl.ds((k - MOE_UP_CHUNKS) * MOE_W_CHUNK, MOE_W_CHUNK), :]
        return pltpu.make_async_copy(src, stage.at[k % 2], sem.at[k % 2])

    def convert(e, k, slot):
        chunk_copy(e, k).wait()
        v = stage[k % 2].astype(BF16)
        if k < MOE_UP_CHUNKS:
            wub[slot, k * MOE_W_CHUNK:(k + 1) * MOE_W_CHUNK, :] = v
        else:
            kd = k - MOE_UP_CHUNKS
            wdb[slot, kd * MOE_W_CHUNK:(kd + 1) * MOE_W_CHUNK, :] = v
        if k + 2 < MOE_CHUNKS:
            chunk_copy(e, k + 2).start()

    def begin(e):
        chunk_copy(e, 0).start()
        chunk_copy(e, 1).start()
        st[1] = 0

    @pl.when(r < nu_ref[0])
    def _():
        e_cur = te_ref[r]
        e_nxt = nxt_ref[r]

        @pl.when(r == 0)
        def _():
            st[0] = 1
            begin(e_cur)

        @pl.when(first_ref[r] == 1)
        def _():
            slot = 1 - st[0]
            done = st[1]
            for k in range(MOE_CHUNKS):
                @pl.when(k >= done)
                def _():
                    convert(e_cur, k, slot)
            st[0] = slot
            st[1] = MOE_CHUNKS

            @pl.when(e_nxt >= 0)
            def _():
                begin(e_nxt)

        cur = st[0]
        x = xs_ref[...].astype(BF16)
        hc = jnp.dot(x, wub[cur], preferred_element_type=F32)
        a = hc[:, :DFF]
        b = hc[:, DFF:]
        act = (a * _sigmoid(a) * b).astype(BF16)
        ys_ref[...] = jnp.dot(act, wdb[cur], preferred_element_type=F32)

        @pl.when(e_nxt >= 0)
        def _():
            done = st[1]
            share = (MOE_CHUNKS - done + rem_ref[r] - 1) // rem_ref[r]
            for k in range(MOE_CHUNKS):
                @pl.when(jnp.logical_and(k >= done, k < done + share))
                def _():
                    convert(e_nxt, k, 1 - cur)
            st[1] = done + share

    @pl.when(r >= nu_ref[0])
    def _():
        ys_ref[...] = jnp.zeros_like(ys_ref)


def _moe(plan, xs, wu, wd, layer):
    def row(r, te, nu, *_):
        return (jnp.minimum(r, nu[0] - 1), 0)

    return pl.pallas_call(
        functools.partial(_moe_body, layer=layer),
        out_shape=jax.ShapeDtypeStruct(xs.shape, F32),
        grid_spec=pltpu.PrefetchScalarGridSpec(
            num_scalar_prefetch=5,
            grid=(xs.shape[0] // TM_E,),
            in_specs=[
                pl.BlockSpec((TM_E, D), row),
                pl.BlockSpec(memory_space=pl.ANY),
                pl.BlockSpec(memory_space=pl.ANY),
            ],
            out_specs=pl.BlockSpec((TM_E, D), lambda r, *_: (r, 0)),
            scratch_shapes=[
                pltpu.VMEM((2, D, 2 * DFF), BF16),
                pltpu.VMEM((2, DFF, D), BF16),
                pltpu.VMEM((2, MOE_W_CHUNK, D), F32),
                pltpu.SemaphoreType.DMA((2,)),
                pltpu.SMEM((2,), I32),
            ],
        ),
        compiler_params=_cparams(("arbitrary",), 56),
        name="moe",
    )(*plan, xs, wu, wd)


def _combine_body(pos0_ref, pos1_ref, posn0_ref, posn1_ref, ys_ref, x_ref, w_ref, mod_ref, g_ref, nmod_ref,
                  *rest, final):
    tm = TM_TOK
    outs, (ybuf, sem) = rest[:-2], rest[-2:]
    i = pl.program_id(0)
    slot = i % 2

    def fetch(p0_ref, p1_ref, s):
        def start_row(r):
            _row_copy(ys_ref, p0_ref[0, r], ybuf.at[s, 0], r, sem.at[s]).start()
            _row_copy(ys_ref, p1_ref[0, r], ybuf.at[s, 1], r, sem.at[s]).start()

        _issue_rows(tm, start_row)

    @pl.when(i == 0)
    def _():
        fetch(pos0_ref, pos1_ref, 0)

    @pl.when(i + 1 < pl.num_programs(0))
    def _():
        fetch(posn0_ref, posn1_ref, 1 - slot)

    for k in range(2):
        pltpu.make_async_copy(ys_ref.at[pl.ds(0, tm)], ybuf.at[slot, k], sem.at[slot]).wait()

    reps = D // LANES
    w0 = jnp.tile(w_ref[:, 0:LANES], (1, reps))
    w1 = jnp.tile(w_ref[:, LANES:2 * LANES], (1, reps))
    xn = x_ref[...] + mod_ref[5:6, :] * (w0 * ybuf[slot, 0] + w1 * ybuf[slot, 1])
    if final:
        outs[0][...] = _rms(xn, g_ref[...])
    else:
        outs[0][...] = xn
        outs[1][...] = (_rms(xn, g_ref[...]) * (1.0 + nmod_ref[1:2, :]) + nmod_ref[0:1, :]).astype(BF16)


def _combine(pos, ys, x, wlanes, mods, layer, n_rows, g_next, final):
    tm = TM_TOK
    nt = n_rows // tm
    next_layer = min(layer + 1, DEPTH - 1)
    tile = pl.BlockSpec((tm, D), lambda i: (i, 0))
    out_shape = [jax.ShapeDtypeStruct((n_rows, D), F32)]
    if not final:
        out_shape.append(jax.ShapeDtypeStruct((n_rows, D), BF16))
    return pl.pallas_call(
        functools.partial(_combine_body, final=final),
        out_shape=out_shape,
        grid=(nt,),
        in_specs=_pos_specs(tm, lambda i: i) + _pos_specs(tm, lambda i: jnp.minimum(i + 1, nt - 1)) + [
            pl.BlockSpec(memory_space=pl.ANY),
            tile,
            pl.BlockSpec((tm, 2 * LANES), lambda i: (i, 0)),
            pl.BlockSpec((None, None, 6, D), lambda i: (layer, _mod_row(i, tm), 0, 0)),
            pl.BlockSpec((1, D), lambda i: (0, 0)),
            pl.BlockSpec((None, None, 6, D), lambda i: (next_layer, _mod_row(i, tm), 0, 0)),
        ],
        out_specs=[tile] * len(out_shape),
        scratch_shapes=[pltpu.VMEM((2, 2, tm, D), F32), pltpu.SemaphoreType.DMA((2,))],
        compiler_params=_cparams(("arbitrary",), 40),
        name="combine",
    )(pos, pos, pos, pos, ys, x, wlanes, mods, g_next.reshape(1, D), mods)


def _route_plan(route, cnt, n_rows):
    counts = cnt[:, 0].astype(I32)
    padded = ((counts + TM_E - 1) // TM_E) * TM_E
    ends = jnp.cumsum(padded)
    offs = ends - padded
    e01 = route[0:2].astype(I32)
    eids = jnp.arange(NE, dtype=I32)[:, None, None]
    off01 = jnp.sum(jnp.where(e01[None] == eids, offs[:, None, None], 0), axis=0)
    pos = off01 + route[2:4].astype(I32)
    pos = pos.reshape(2, n_rows // TM_TOK, 1, TM_TOK)
    nt = _n_expert_tiles(n_rows)
    tile_start = jnp.arange(nt, dtype=I32) * TM_E
    tile_expert = jnp.minimum(jnp.sum((tile_start[:, None] >= ends[None, :]).astype(I32), axis=1), NE - 1)
    n_used = (ends[-1] // TM_E).reshape(1).astype(I32)
    spare = n_used[0] + jnp.arange(NE, dtype=I32)
    zstart = jnp.concatenate([offs + (counts // TM_E) * TM_E, jnp.minimum(spare, nt - 1) * TM_E])
    zneed = jnp.concatenate([counts % TM_E != 0, spare < nt]).astype(I32)
    wl = jnp.concatenate([jnp.broadcast_to(route[4][:, None], (n_rows, LANES)),
                          jnp.broadcast_to(route[5][:, None], (n_rows, LANES))], axis=1)
    ntile = padded // TM_E
    ecol = jnp.arange(NE, dtype=I32)
    onehot = tile_expert[:, None] == ecol[None, :]
    pick = lambda v: jnp.sum(jnp.where(onehot, v[None, :], 0), axis=1)
    j_in = jnp.arange(nt, dtype=I32) - pick(offs // TM_E)
    first = (j_in == 0).astype(I32)
    rem = jnp.maximum(pick(ntile) - j_in, 1)
    later = (ecol[None, :] > ecol[:, None]) & (ntile[None, :] > 0)
    nxt_e = jnp.min(jnp.where(later, ecol[None, :], NE), axis=1)
    nxt = pick(jnp.where(nxt_e < NE, nxt_e, -1))
    moe_plan = (tile_expert.astype(I32), n_used, first, nxt.astype(I32), rem.astype(I32))
    return pos, moe_plan, zstart.astype(I32), zneed, wl


def kernel(x, c, ctx, c_ctx, ada_w, ada_b, norm1_g, norm2_g, w_in, attn_sink, sgu_ln_g, sgu_ln_b, sgu_w, sgu_b,
           sconv_w, conf_dw_w, conf_dw_b, conf_ln_g, conf_ln_b, w_branch, w_out, router_w, router_b,
           exp_w_up, exp_w_down, final_g):
    cvec = jnp.concatenate([c, c_ctx[None, :], jnp.zeros((8 - NB - 1, D), F32)], axis=0)
    mods = _ada(cvec, ada_w, ada_b).reshape(DEPTH, 8, 6, D)
    xa, cxa = x.reshape(TX, D), ctx.reshape(TC, D)
    cos, sin = _rope_tables()
    consts = _rope_constants()
    rw = router_w.T.astype(BF16)
    wb_all = w_branch.astype(BF16)
    wo_all = w_out.astype(BF16)

    for l in range(DEPTH):
        last = l == DEPTH - 1
        n_rows = TX if last else T
        n_x_tiles = TX // TM_TOK
        lp = (sgu_ln_g[l], sgu_ln_b[l], sgu_w[l], sgu_b[l], sconv_w[l], conf_dw_w[l], conf_dw_b[l],
              conf_ln_g[l], conf_ln_b[l])
        c_rows0 = 0 if l == 0 else TX

        if l == 0:
            h = _norm_mod(xa, cxa, norm1_g[l], mods, l)
        if not last:
            p = _inproj(h, w_in, l, T, N_IN // TN_IN)
            qr, kr, vr = _prep(p, T, cos, sin, consts)
            ya = _attn(qr, kr, vr, kr, vr, TX // LC, attn_sink[l])
            yac = _ctx_attn(qr, kr, vr, attn_sink[l])
        else:
            p = _inproj(h, w_in, l, TX, N_IN // TN_IN)
            qr, kr, vr = _prep(p, TX, cos, sin, consts)
            kc, vc = _ctx_kv(h, w_in, l, consts[1])
            ya = _attn(qr, kr, vr, kc, vc, 0, attn_sink[l])
            yac = ya
        yr = _mixers(p, n_rows, n_x_tiles, lp)
        xn, h2, lg = _post(p, ya, yac, yr, wb_all, wo_all, xa, cxa, c_rows0 // TM_TOK, mods, l,
                           norm2_g[l], rw, n_rows)
        route, cnt = _route(lg, router_b, n_rows)
        pos, moe_plan, zstart, zneed, wl = _route_plan(route, cnt, n_rows)
        xs = _dispatch(zstart, zneed, pos, h2, n_rows)
        ys = _moe(moe_plan, xs, exp_w_up, exp_w_down, l)
        if last:
            (out,) = _combine(pos, ys, xn, wl, mods, l, n_rows, final_g, True)
        else:
            xa, h = _combine(pos, ys, xn, wl, mods, l, n_rows, norm1_g[l + 1], False)
            cxa = xa

    return out.reshape(NB, S, D)
```

```python
import functools

import numpy as np
import jax
import jax.numpy as jnp
from jax import lax
from jax.experimental import pallas as pl
from jax.experimental.pallas import tpu as pltpu

F32 = jnp.float32
BF16 = jnp.bfloat16
I32 = jnp.int32

D = 2048
NB = 4
S = 2048
LC = 256
DEPTH = 2
GRID_W = 64
BW = 512
HD = 64
NH = 8
NKV = 2
REP = NH // NKV
WINDOW = 128
QB = 128
ROPE_THETA = 10000.0
SGU_CHUNK = 128
SGU_GROUPS = 4
SCONV_K = 3
CONF_K = 31
NE = 16
NG = 4
EPG = NE // NG
DFF = D // 2
N_IN = BW + 2 * NKV * HD + 7 * BW + 4 * D
TX = NB * S
TC = NB * LC
T = TX + TC

OFF_KV = BW
OFF_SU, OFF_SV, OFF_CB, OFF_CC, OFF_CX, OFF_GA, OFF_GB, OFF_GATES = 768, 1280, 1792, 2304, 2816, 3328, 3840, 4352

LANES = 128
SUBLANES = 8
MIB = 1024 * 1024

TM_NORM = 512
TM_IN = 1024
INPROJ_ROW_CHUNK = 512
IN_W_CHUNK = 256
IN_CHUNKS = D // IN_W_CHUNK
TN_IN = 1792
TM_TOK = 256
TM_PREP = 512
TM_E = 256
HALO = 16
ROW_UNROLL = 8
MOE_W_CHUNK = 512
MOE_UP_CHUNKS = D // MOE_W_CHUNK
MOE_CHUNKS = MOE_UP_CHUNKS + DFF // MOE_W_CHUNK


def _n_expert_tiles(n_rows):
    return (2 * n_rows) // TM_E + NE


def _cparams(sem, vmem_mib):
    return pltpu.CompilerParams(dimension_semantics=sem, vmem_limit_bytes=vmem_mib * MIB)


def _sigmoid(x):
    return 0.5 * jnp.tanh(0.5 * x) + 0.5


def _gelu_tanh(x):
    c = np.float32(np.sqrt(2.0 / np.pi))
    return 0.5 * x * (1.0 + jnp.tanh(c * (x + np.float32(0.044715) * (x * x * x))))


def _layernorm(x, g, b, eps=1e-5):
    mu = jnp.mean(x, axis=-1, keepdims=True)
    xc = x - mu
    var = jnp.mean(xc * xc, axis=-1, keepdims=True)
    return xc * lax.rsqrt(var + eps) * g + b


def _mod_row(i, tm):
    return jnp.where(i < TX // tm, (i * tm) // S, NB)


def _ada_body(c_ref, w_ref, b_ref, o_ref):
    c = c_ref[...]
    s = (c * _sigmoid(c)).astype(BF16)
    o_ref[...] = jnp.dot(s, w_ref[...].astype(BF16), preferred_element_type=F32) + b_ref[...]


def _ada(cvec, ada_w, ada_b):
    tn = 1024
    return pl.pallas_call(
        _ada_body,
        out_shape=jax.ShapeDtypeStruct((DEPTH, 8, 6 * D), F32),
        grid=(DEPTH, 6 * D // tn),
        in_specs=[
            pl.BlockSpec((8, D), lambda l, j: (0, 0)),
            pl.BlockSpec((None, D, tn), lambda l, j: (l, 0, j)),
            pl.BlockSpec((None, 1, tn), lambda l, j: (l, 0, j)),
        ],
        out_specs=pl.BlockSpec((None, 8, tn), lambda l, j: (l, 0, j)),
        compiler_params=_cparams(("arbitrary", "arbitrary"), 40),
        name="ada",
    )(cvec, ada_w, ada_b.reshape(DEPTH, 1, 6 * D))


def _stream_specs(tm, c_blk0, width=D):
    nx = TX // tm
    return [pl.BlockSpec((tm, width), lambda i, *_: (jnp.minimum(i, nx - 1), 0)),
            pl.BlockSpec((tm, width), lambda i, *_: (c_blk0 + jnp.maximum(i - nx, 0), 0))]


def _stream_tile(x_ref, c_ref, tm):
    return jnp.where(pl.program_id(0) < TX // tm, x_ref[...], c_ref[...])


def _rms(x, g):
    return x * lax.rsqrt(jnp.mean(x * x, axis=-1, keepdims=True) + 1e-6) * g


def _norm_mod_body(x_ref, c_ref, g_ref, mod_ref, o_ref):
    y = _rms(_stream_tile(x_ref, c_ref, TM_NORM), g_ref[...])
    o_ref[...] = (y * (1.0 + mod_ref[1:2, :]) + mod_ref[0:1, :]).astype(o_ref.dtype)


def _norm_mod(x, cx, g, mods, layer):
    tm = TM_NORM
    return pl.pallas_call(
        _norm_mod_body,
        out_shape=jax.ShapeDtypeStruct((T, D), BF16),
        grid=(T // tm,),
        in_specs=_stream_specs(tm, 0) + [
            pl.BlockSpec((1, D), lambda i: (0, 0)),
            pl.BlockSpec((None, None, 6, D), lambda i: (layer, _mod_row(i, tm), 0, 0)),
        ],
        out_specs=pl.BlockSpec((tm, D), lambda i: (i, 0)),
        compiler_params=_cparams(("arbitrary",), 32),
        name="norm_mod",
    )(x, cx, g.reshape(1, D), mods)


def _inproj_body(h_ref, w_hbm, o_ref, wbf, stage, sem, st, *, layer):
    j = pl.program_id(0)
    i = pl.program_id(1)
    nj = pl.num_programs(0)
    ni = pl.num_programs(1)

    def chunk_copy(jj, k):
        cols = pl.ds(pl.multiple_of(jj * TN_IN, LANES), TN_IN)
        src = w_hbm.at[layer, pl.ds(k * IN_W_CHUNK, IN_W_CHUNK), cols]
        return pltpu.make_async_copy(src, stage.at[k % 2], sem.at[k % 2])

    def convert(jj, k, slot):
        chunk_copy(jj, k).wait()
        wbf[slot, k * IN_W_CHUNK:(k + 1) * IN_W_CHUNK, :] = stage[k % 2].astype(BF16)
        if k + 2 < IN_CHUNKS:
            chunk_copy(jj, k + 2).start()

    def begin(jj):
        chunk_copy(jj, 0).start()
        chunk_copy(jj, 1).start()
        st[1] = 0

    @pl.when(jnp.logical_and(i == 0, j == 0))
    def _():
        st[0] = 1
        begin(0)

    @pl.when(i == 0)
    def _():
        slot = 1 - st[0]
        done = st[1]
        for k in range(IN_CHUNKS):
            @pl.when(k >= done)
            def _():
                convert(j, k, slot)
        st[0] = slot
        st[1] = IN_CHUNKS

        @pl.when(j + 1 < nj)
        def _():
            begin(j + 1)

    cur = st[0]
    col0 = j * TN_IN
    rows = INPROJ_ROW_CHUNK

    def run(epilogue):
        for c in range(h_ref.shape[0] // rows):
            sl = slice(c * rows, (c + 1) * rows)
            acc = jnp.dot(h_ref[sl, :], wbf[cur], preferred_element_type=F32)
            o_ref[sl, :] = epilogue(acc).astype(BF16)

    @pl.when(col0 + TN_IN <= OFF_GB)
    def _():
        run(lambda acc: acc)

    @pl.when(col0 >= OFF_GB)
    def _():
        run(_sigmoid)

    @pl.when(jnp.logical_and(col0 < OFF_GB, col0 + TN_IN > OFF_GB))
    def _():
        col = col0 + lax.broadcasted_iota(I32, (rows, TN_IN), 1)
        run(lambda acc: jnp.where(col >= OFF_GB, _sigmoid(acc), acc))

    @pl.when(j + 1 < nj)
    def _():
        done = st[1]
        left = ni - i
        share = (IN_CHUNKS - done + left - 1) // left
        for k in range(IN_CHUNKS):
            @pl.when(jnp.logical_and(k >= done, k < done + share))
            def _():
                convert(j + 1, k, 1 - cur)
        st[1] = done + share


def _inproj(h, w_in, layer, n_rows, n_col_tiles):
    tm = TM_IN
    return pl.pallas_call(
        functools.partial(_inproj_body, layer=layer),
        out_shape=jax.ShapeDtypeStruct((n_rows, n_col_tiles * TN_IN), BF16),
        grid=(n_col_tiles, n_rows // tm),
        in_specs=[
            pl.BlockSpec((tm, D), lambda j, i: (i, 0)),
            pl.BlockSpec(memory_space=pl.ANY),
        ],
        out_specs=pl.BlockSpec((tm, TN_IN), lambda j, i: (i, j)),
        scratch_shapes=[
            pltpu.VMEM((2, D, TN_IN), BF16),
            pltpu.VMEM((2, IN_W_CHUNK, TN_IN), F32),
            pltpu.SemaphoreType.DMA((2,)),
            pltpu.SMEM((2,), I32),
        ],
        compiler_params=_cparams(("arbitrary", "arbitrary"), 56),
        name="inproj",
    )(h, w_in)


def _rope_constants():
    rh = np.zeros((HD, HD), np.float32)
    for base in (0, 32):
        for d in range(16):
            rh[base + 16 + d, base + d] = -1.0
            rh[base + d, base + 16 + d] = 1.0
    rq = np.kron(np.eye(NH, dtype=np.float32), rh)
    rk = np.kron(np.eye(NKV, dtype=np.float32), rh)
    rep = np.zeros((NKV * HD, NH * HD), np.float32)
    for g in range(NKV):
        for r in range(REP):
            for d in range(HD):
                rep[g * HD + d, g * REP * HD + r * HD + d] = 1.0
    return jnp.asarray(rq, BF16), jnp.asarray(rep, BF16), jnp.asarray(rk @ rep, BF16)


def _rope_tables():
    half = HD // 2
    inv = 1.0 / (ROPE_THETA ** (jnp.arange(0, half, 2, dtype=F32) / half))
    pos = jnp.arange(S)
    ar = (pos // GRID_W).astype(F32)[:, None] * inv
    ac = (pos % GRID_W).astype(F32)[:, None] * inv
    reps = LANES // HD
    cos = jnp.tile(jnp.concatenate([jnp.cos(ar), jnp.cos(ar), jnp.cos(ac), jnp.cos(ac)], axis=1), (1, reps))
    sin = jnp.tile(jnp.concatenate([jnp.sin(ar), jnp.sin(ar), jnp.sin(ac), jnp.sin(ac)], axis=1), (1, reps))
    cos = jnp.concatenate([cos, jnp.ones((TM_PREP, LANES), F32)], axis=0)
    sin = jnp.concatenate([sin, jnp.zeros((TM_PREP, LANES), F32)], axis=0)
    return cos, sin


def _prep_body(q_ref, kv_ref, cos_ref, sin_ref, rq_ref, rep_ref, rrep_ref, qo_ref, ko_ref, vo_ref):
    cos = jnp.tile(cos_ref[...], (1, NH * HD // LANES))
    sin = jnp.tile(sin_ref[...], (1, NH * HD // LANES))
    q = q_ref[...]
    qs = jnp.dot(q, rq_ref[...], preferred_element_type=F32)
    qo_ref[...] = ((q.astype(F32) * cos + qs * sin) * (HD ** -0.5)).astype(BF16)
    k = kv_ref[:, 0:NKV * HD]
    v = kv_ref[:, NKV * HD:2 * NKV * HD]
    kr = jnp.dot(k, rep_ref[...], preferred_element_type=F32)
    ks = jnp.dot(k, rrep_ref[...], preferred_element_type=F32)
    ko_ref[...] = (kr * cos + ks * sin).astype(BF16)
    vo_ref[...] = jnp.dot(v, rep_ref[...], preferred_element_type=F32).astype(BF16)


def _prep(p, n_rows, cos, sin, consts):
    tm = TM_PREP
    rq, rep, rrep = consts
    tps = S // tm

    def tab(i):
        return (jnp.where(i < TX // tm, i % tps, tps), 0)

    w = NH * HD
    return pl.pallas_call(
        _prep_body,
        out_shape=[jax.ShapeDtypeStruct((n_rows, w), BF16)] * 3,
        grid=(n_rows // tm,),
        in_specs=[
            pl.BlockSpec((tm, w), lambda i: (i, 0)),
            pl.BlockSpec((tm, 2 * NKV * HD), lambda i: (i, OFF_KV // (2 * NKV * HD))),
            pl.BlockSpec((tm, LANES), tab),
            pl.BlockSpec((tm, LANES), tab),
            pl.BlockSpec((w, w), lambda i: (0, 0)),
            pl.BlockSpec((NKV * HD, w), lambda i: (0, 0)),
            pl.BlockSpec((NKV * HD, w), lambda i: (0, 0)),
        ],
        out_specs=[pl.BlockSpec((tm, w), lambda i: (i, 0))] * 3,
        compiler_params=_cparams(("arbitrary",), 32),
        name="prep",
    )(p, p, cos, sin, rq, rep, rrep)


def _ctx_kv_body(h_ref, w_ref, rep_ref, ko_ref, vo_ref):
    kv = jnp.dot(h_ref[...], w_ref[...].astype(BF16), preferred_element_type=F32).astype(BF16)
    ko_ref[...] = jnp.dot(kv[:, 0:NKV * HD], rep_ref[...], preferred_element_type=F32).astype(BF16)
    vo_ref[...] = jnp.dot(kv[:, NKV * HD:2 * NKV * HD], rep_ref[...], preferred_element_type=F32).astype(BF16)


def _ctx_kv(h, w_in, layer, rep):
    tm = TM_TOK
    w = NH * HD
    kvw = 2 * NKV * HD
    return pl.pallas_call(
        _ctx_kv_body,
        out_shape=[jax.ShapeDtypeStruct((TC, w), BF16)] * 2,
        grid=(TC // tm,),
        in_specs=[
            pl.BlockSpec((tm, D), lambda i: (TX // tm + i, 0)),
            pl.BlockSpec((None, pl.Element(D), pl.Element(kvw)), lambda i: (layer, 0, OFF_KV)),
            pl.BlockSpec((NKV * HD, w), lambda i: (0, 0)),
        ],
        out_specs=[pl.BlockSpec((tm, w), lambda i: (i, 0))] * 2,
        compiler_params=_cparams(("arbitrary",), 32),
        name="ctx_kv",
    )(h, w_in, rep)


def _attn_groups(q, kbs, vbs, sink_ref, valid, nq):
    gw = REP * HD
    lane_head = lax.broadcasted_iota(I32, (nq, gw), 1) // HD
    row_head = lax.broadcasted_iota(I32, (REP * nq, 1), 0) // nq
    valid_rep = None if valid is None else jnp.concatenate([valid] * REP, axis=0)
    scores = []
    for g in range(NKV):
        qg = q[:, g * gw:(g + 1) * gw]
        qs = jnp.concatenate([jnp.where(lane_head == r, qg, jnp.zeros_like(qg)) for r in range(REP)], axis=0)
        s = lax.dot_general(qs, kbs[g], (((1,), (1,)), ((), ())), preferred_element_type=F32)
        scores.append(s if valid_rep is None else jnp.where(valid_rep, s, -jnp.inf))
    probs = []
    for g in range(NKV):
        s = scores[g]
        sink = jnp.zeros((REP * nq, 1), F32)
        for r in range(REP):
            sink = jnp.where(row_head == r, sink_ref[g * REP + r], sink)
        m = jnp.maximum(jnp.max(s, axis=-1, keepdims=True), sink)
        e = jnp.exp(s - m)
        den = jnp.sum(e, axis=-1, keepdims=True) + jnp.exp(sink - m)
        probs.append((e * (1.0 / den)).astype(BF16))
    outs = []
    for g in range(NKV):
        o = jnp.dot(probs[g], vbs[g], preferred_element_type=F32)
        og = jnp.zeros((nq, gw), F32)
        for r in range(REP):
            og = og + jnp.where(lane_head == r, o[r * nq:(r + 1) * nq, :], 0.0)
        outs.append(og)
    return jnp.concatenate(outs, axis=1).astype(BF16)


def _attn_body(sink_ref, q_ref, kp_ref, kc_ref, kn_ref, vp_ref, vc_ref, vn_ref, kx_ref, vx_ref, o_ref):
    n = pl.program_id(1)
    nblk = S // QB
    nk = 3 * QB + LC
    row = lax.broadcasted_iota(I32, (QB, nk), 0)
    col = lax.broadcasted_iota(I32, (QB, nk), 1)
    lo = jnp.where(n == 0, QB, 0)
    hi = jnp.where(n == nblk - 1, 2 * QB, 3 * QB)
    band = (col >= row) & (col <= row + 2 * WINDOW) & (col >= lo) & (col < hi)
    valid = band | (col >= 3 * QB)
    gw = REP * HD
    kbs, vbs = [], []
    for g in range(NKV):
        sl = slice(g * gw, (g + 1) * gw)
        kbs.append(jnp.concatenate([kp_ref[:, sl], kc_ref[:, sl], kn_ref[:, sl], kx_ref[:, sl]], axis=0))
        vbs.append(jnp.concatenate([vp_ref[:, sl], vc_ref[:, sl], vn_ref[:, sl], vx_ref[:, sl]], axis=0))
    o_ref[...] = _attn_groups(q_ref[...], kbs, vbs, sink_ref, valid, QB)


def _attn(qr, kr, vr, kc_arr, vc_arr, ctx_blk0, sink):
    nblk = S // QB
    w = NH * HD

    def cur(b, n):
        return (b * nblk + n, 0)

    def prev(b, n):
        return (b * nblk + jnp.maximum(n - 1, 0), 0)

    def nxt(b, n):
        return (b * nblk + jnp.minimum(n + 1, nblk - 1), 0)

    def cx(b, n):
        return (ctx_blk0 + b, 0)

    blk = lambda f: pl.BlockSpec((QB, w), f)
    return pl.pallas_call(
        _attn_body,
        out_shape=jax.ShapeDtypeStruct((TX, w), BF16),
        grid=(NB, nblk),
        in_specs=[
            pl.BlockSpec(memory_space=pltpu.SMEM),
            blk(cur), blk(prev), blk(cur), blk(nxt), blk(prev), blk(cur), blk(nxt),
            pl.BlockSpec((LC, w), cx), pl.BlockSpec((LC, w), cx),
        ],
        out_specs=blk(cur),
        compiler_params=_cparams(("arbitrary", "arbitrary"), 32),
        name="window_attn",
    )(sink, qr, kr, kr, kr, vr, vr, vr, kc_arr, vc_arr)


def _ctx_attn_body(sink_ref, q_ref, kx_ref, vx_ref, o_ref):
    gw = REP * HD
    kbs = [kx_ref[:, g * gw:(g + 1) * gw] for g in range(NKV)]
    vbs = [vx_ref[:, g * gw:(g + 1) * gw] for g in range(NKV)]
    o_ref[...] = _attn_groups(q_ref[...], kbs, vbs, sink_ref, None, LC)


def _ctx_attn(qr, kr, vr, sink):
    w = NH * HD
    blk0 = TX // LC
    spec = pl.BlockSpec((LC, w), lambda b: (blk0 + b, 0))
    return pl.pallas_call(
        _ctx_attn_body,
        out_shape=jax.ShapeDtypeStruct((TC, w), BF16),
        grid=(NB,),
        in_specs=[pl.BlockSpec(memory_space=pltpu.SMEM), spec, spec, spec],
        out_specs=pl.BlockSpec((LC, w), lambda b: (b, 0)),
        compiler_params=_cparams(("arbitrary",), 32),
        name="ctx_attn",
    )(sink, qr, kr, vr)


def _mix_body(su_ref, sv_ref, cb_ref, cc_ref, cx_ref, ga_ref, gb_ref,
              ccp_ref, cxp_ref, gap_ref, gbp_ref, ccn_ref, cxn_ref, gan_ref, gbn_ref,
              lng_ref, lnb_ref, sw_ref, sbias_ref, scw_ref, dww_ref, dwb_ref, clg_ref, clb_ref,
              o_ref, m_scr, z_scr, *, n_x_tiles):
    tm = TM_TOK
    i = pl.program_id(0)
    tps = S // tm
    is_x = i < n_x_tiles
    first = jnp.logical_or(jnp.logical_not(is_x), (i % tps) == 0)
    last = jnp.logical_or(jnp.logical_not(is_x), (i % tps) == tps - 1)
    keep_p = jnp.where(first, 0.0, 1.0)
    keep_n = jnp.where(last, 0.0, 1.0)

    u = _gelu_tanh(su_ref[...].astype(F32))
    v = _layernorm(_gelu_tanh(sv_ref[...].astype(F32)), lng_ref[...], lnb_ref[...]).astype(BF16)
    gwid = BW // SGU_GROUPS
    chunks = []
    for c in range(tm // SGU_CHUNK):
        parts = []
        for g in range(SGU_GROUPS):
            vc = v[c * SGU_CHUNK:(c + 1) * SGU_CHUNK, g * gwid:(g + 1) * gwid]
            parts.append(jnp.dot(sw_ref[g], vc, preferred_element_type=F32))
        chunks.append(jnp.concatenate(parts, axis=1) + sbias_ref[...])
    yb = u * jnp.concatenate(chunks, axis=0)

    m_scr[HALO:HALO + tm, :] = cc_ref[...].astype(F32) * cx_ref[...].astype(F32)
    m_scr[0:HALO, :] = ccp_ref[...].astype(F32) * cxp_ref[...].astype(F32) * keep_p
    m_scr[HALO + tm:, :] = ccn_ref[...].astype(F32) * cxn_ref[...].astype(F32) * keep_n
    z = jnp.zeros((tm, BW), F32)
    for k in range(SCONV_K):
        z = z + scw_ref[k:k + 1, :] * m_scr[pl.ds(HALO - SCONV_K // 2 + k, tm), :]
    yc = cb_ref[...].astype(F32) * z

    z_scr[0, HALO:HALO + tm, :] = ga_ref[...].astype(F32) * gb_ref[...].astype(F32)
    z_scr[0, 0:HALO, :] = gap_ref[...].astype(F32) * gbp_ref[...].astype(F32) * keep_p
    z_scr[0, HALO + tm:, :] = gan_ref[...].astype(F32) * gbn_ref[...].astype(F32) * keep_n
    n_sh = tm + 2 * HALO - SUBLANES
    for s in range(1, SUBLANES):
        z_scr[s, 0:n_sh, :] = z_scr[0, pl.ds(s, n_sh), :]
    acc = jnp.zeros((tm, BW), F32) + dwb_ref[...]
    for k in range(CONF_K):
        off = HALO - CONF_K // 2 + k
        base = off - off % SUBLANES
        acc = acc + dww_ref[k:k + 1, :] * z_scr[off % SUBLANES, base:base + tm, :]
    zn = _layernorm(acc, clg_ref[...], clb_ref[...])
    yd = zn * _sigmoid(zn)

    o_ref[...] = jnp.concatenate([yb, yc, yd], axis=1).astype(BF16)


def _mixers(p, n_rows, n_x_tiles, lp):
    tm = TM_TOK
    (sgu_ln_g, sgu_ln_b, sgu_w, sgu_b, sconv_w, conf_dw_w, conf_dw_b, conf_ln_g, conf_ln_b) = lp

    def sec(off):
        return pl.BlockSpec((pl.Element(tm), pl.Element(BW)), lambda i: (i * tm, off))

    hpt = tm // HALO

    def halo_prev(off):
        return pl.BlockSpec((pl.Element(HALO), pl.Element(BW)),
                            lambda i: (jnp.maximum(i * hpt - 1, 0) * HALO, off))

    def halo_next(off):
        return pl.BlockSpec((pl.Element(HALO), pl.Element(BW)),
                            lambda i: (jnp.minimum(i * hpt + hpt, n_rows // HALO - 1) * HALO, off))

    def const(shape):
        return pl.BlockSpec(shape, lambda i: (0,) * len(shape))

    sbias = jnp.repeat(sgu_b.T, BW // SGU_GROUPS, axis=1)
    row = lambda a: a.reshape(1, BW)
    conv_offs = (OFF_CC, OFF_CX, OFF_GA, OFF_GB)
    return pl.pallas_call(
        functools.partial(_mix_body, n_x_tiles=n_x_tiles),
        out_shape=jax.ShapeDtypeStruct((n_rows, 3 * BW), BF16),
        grid=(n_rows // tm,),
        in_specs=[sec(o) for o in (OFF_SU, OFF_SV, OFF_CB, OFF_CC, OFF_CX, OFF_GA, OFF_GB)]
        + [halo_prev(o) for o in conv_offs] + [halo_next(o) for o in conv_offs]
        + [const((1, BW)), const((1, BW)), const((SGU_GROUPS, SGU_CHUNK, SGU_CHUNK)), const((SGU_CHUNK, BW)),
           const((SCONV_K, BW)), const((CONF_K, BW)), const((1, BW)), const((1, BW)), const((1, BW))],
        out_specs=pl.BlockSpec((tm, 3 * BW), lambda i: (i, 0)),
        scratch_shapes=[pltpu.VMEM((tm + 2 * HALO, BW), F32), pltpu.VMEM((SUBLANES, tm + 2 * HALO, BW), F32)],
        compiler_params=_cparams(("arbitrary",), 32),
        name="mixers",
    )(*([p] * 15), row(sgu_ln_g), row(sgu_ln_b), sgu_w.astype(BF16), sbias, sconv_w, conf_dw_w,
      row(conf_dw_b), row(conf_ln_g), row(conf_ln_b))


def _norm2(xn, n2g, mod):
    return _rms(xn, n2g) * (1.0 + mod[4:5, :]) + mod[3:4, :]


def _post_body(g_ref, ya_ref, yac_ref, yr_ref, wb_ref, wo_ref, x_ref, cx_ref, mod_ref, n2g_ref, rw_ref,
               xo_ref, h2_ref, lg_ref):
    tm = TM_TOK
    merged = None
    for br in range(4):
        y = _stream_tile(ya_ref, yac_ref, tm) if br == 0 else yr_ref[:, (br - 1) * BW:br * BW]
        pr = jnp.dot(y, wb_ref[br], preferred_element_type=F32)
        term = g_ref[:, br * D:(br + 1) * D].astype(F32) * pr
        merged = term if merged is None else merged + term
    out = jnp.dot(merged.astype(BF16), wo_ref[...], preferred_element_type=F32)
    xn = _stream_tile(x_ref, cx_ref, tm) + mod_ref[2:3, :] * out
    xo_ref[...] = xn
    h2 = _norm2(xn, n2g_ref[...], mod_ref)
    h2_ref[...] = h2
    nt = (((1,), (1,)), ((), ()))
    lg_ref[...] = lax.dot_general(rw_ref[...], h2.astype(BF16), nt, preferred_element_type=F32)


def _post(p, ya, yac, yr, wb, wo, x, cx, c_blk0, mods, layer, n2g, rw, n_rows):
    tm = TM_TOK
    one = pl.Buffered(1)
    return pl.pallas_call(
        _post_body,
        out_shape=[jax.ShapeDtypeStruct((n_rows, D), F32), jax.ShapeDtypeStruct((n_rows, D), F32),
                   jax.ShapeDtypeStruct((NE, n_rows), F32)],
        grid=(n_rows // tm,),
        in_specs=[pl.BlockSpec((pl.Element(tm), pl.Element(4 * D)), lambda i: (i * tm, OFF_GATES))]
        + _stream_specs(tm, 0, BW) + [
            pl.BlockSpec((tm, 3 * BW), lambda i: (i, 0)),
            pl.BlockSpec((None, 4, BW, D), lambda i: (layer, 0, 0, 0), pipeline_mode=one),
            pl.BlockSpec((None, D, D), lambda i: (layer, 0, 0), pipeline_mode=one)]
        + _stream_specs(tm, c_blk0) + [
            pl.BlockSpec((None, None, 6, D), lambda i: (layer, _mod_row(i, tm), 0, 0)),
            pl.BlockSpec((1, D), lambda i: (0, 0)),
            pl.BlockSpec((NE, D), lambda i: (0, 0)),
        ],
        out_specs=[pl.BlockSpec((tm, D), lambda i: (i, 0)), pl.BlockSpec((tm, D), lambda i: (i, 0)),
                   pl.BlockSpec((NE, tm), lambda i: (0, i))],
        compiler_params=_cparams(("arbitrary",), 52),
        name="post",
    )(p, ya, yac, yr, wb, wo, x, cx, mods, n2g.reshape(1, D), rw)


def _route_body(lg_ref, rb_ref, tri_ref, o_ref, cnt_ref, carry_ref):
    tm = TM_TOK
    i = pl.program_id(0)

    @pl.when(i == 0)
    def _():
        carry_ref[...] = jnp.zeros_like(carry_ref)

    lg = lg_ref[...]
    e = jnp.exp(lg - jnp.max(lg, axis=0, keepdims=True))
    sc = e / jnp.sum(e, axis=0, keepdims=True)
    bi = sc + rb_ref[...]
    b = [bi[k:k + 1, :] for k in range(NE)]
    s = [sc[k:k + 1, :] for k in range(NE)]

    gs = []
    for g in range(NG):
        v = b[g * EPG:(g + 1) * EPG]
        best = None
        for a in range(EPG):
            for c in range(a + 1, EPG):
                ps = v[a] + v[c]
                best = ps if best is None else jnp.maximum(best, ps)
        gs.append(best)
    gsel = jnp.zeros((1, tm), I32)
    gbest = gs[0]
    for g in range(1, NG):
        take = gs[g] > gbest
        gsel = jnp.where(take, g, gsel)
        gbest = jnp.where(take, gs[g], gbest)

    vb, vs = [], []
    for j in range(EPG):
        xb, xs = b[j], s[j]
        for g in range(1, NG):
            xb = jnp.where(gsel == g, b[g * EPG + j], xb)
            xs = jnp.where(gsel == g, s[g * EPG + j], xs)
        vb.append(xb)
        vs.append(xs)
    order = []
    for j in range(EPG):
        c = jnp.zeros((1, tm), I32)
        for m in range(EPG):
            if m == j:
                continue
            ahead = (vb[m] >= vb[j]) if m < j else (vb[m] > vb[j])
            c = c + jnp.where(ahead, 1, 0)
        order.append(c)
    zero = jnp.zeros((1, tm), F32)
    w0 = zero
    w1 = zero
    j0 = jnp.zeros((1, tm), I32)
    j1 = jnp.zeros((1, tm), I32)
    for j in range(EPG):
        w0 = jnp.where(order[j] == 0, vs[j], w0)
        w1 = jnp.where(order[j] == 1, vs[j], w1)
        j0 = jnp.where(order[j] == 0, j, j0)
        j1 = jnp.where(order[j] == 1, j, j1)
    tot = w0 + w1
    e0 = gsel * EPG + j0
    e1 = gsel * EPG + j1

    eid = lax.broadcasted_iota(I32, (NE, tm), 0)
    oh0 = eid == e0
    oh1 = eid == e1
    oh = jnp.where(oh0 | oh1, 1.0, 0.0)
    rank = jnp.dot(oh.astype(BF16), tri_ref[...], preferred_element_type=F32) + carry_ref[:, 0:1]
    r0 = jnp.sum(jnp.where(oh0, rank, 0.0), axis=0, keepdims=True)
    r1 = jnp.sum(jnp.where(oh1, rank, 0.0), axis=0, keepdims=True)
    new_carry = carry_ref[...] + jnp.sum(oh, axis=1, keepdims=True)
    carry_ref[...] = new_carry
    cnt_ref[...] = new_carry

    o_ref[0:1, :] = e0.astype(F32)
    o_ref[1:2, :] = e1.astype(F32)
    o_ref[2:3, :] = r0
    o_ref[3:4, :] = r1
    o_ref[4:5, :] = w0 / tot
    o_ref[5:6, :] = w1 / tot
    o_ref[6:8, :] = jnp.zeros((2, tm), F32)


def _route(lg, router_b, n_rows):
    tm = TM_TOK
    tri = jnp.asarray(np.triu(np.ones((tm, tm), np.float32), 1), BF16)
    return pl.pallas_call(
        _route_body,
        out_shape=[jax.ShapeDtypeStruct((8, n_rows), F32), jax.ShapeDtypeStruct((NE, LANES), F32)],
        grid=(n_rows // tm,),
        in_specs=[
            pl.BlockSpec((NE, tm), lambda i: (0, i)),
            pl.BlockSpec((NE, 1), lambda i: (0, 0)),
            pl.BlockSpec((tm, tm), lambda i: (0, 0)),
        ],
        out_specs=[pl.BlockSpec((8, tm), lambda i: (0, i)), pl.BlockSpec((NE, LANES), lambda i: (0, 0))],
        scratch_shapes=[pltpu.VMEM((NE, LANES), F32)],
        compiler_params=_cparams(("arbitrary",), 32),
        name="route",
    )(lg, router_b.reshape(NE, 1), tri)


def _row_copy(src_ref, src_row, dst_ref, dst_row, sem):
    return pltpu.make_async_copy(src_ref.at[pl.ds(src_row, 1)], dst_ref.at[pl.ds(dst_row, 1)], sem)


def _issue_rows(n, start_row):
    def trip(t, c):
        base = pl.multiple_of(t * ROW_UNROLL, ROW_UNROLL)
        for u in range(ROW_UNROLL):
            start_row(base + u)
        return c

    lax.fori_loop(0, n // ROW_UNROLL, trip, 0)


def _dispatch_body(tail_ref, need_ref, pos0_ref, pos1_ref, h_ref, xs_ref, hbuf, zero_ref, sem, zsem):
    tm = TM_TOK
    i = pl.program_id(0)
    slot = i % 2
    hbuf[slot] = h_ref[...]

    @pl.when(pl.program_id(0) == 0)
    def _():
        zero_ref[...] = jnp.zeros_like(zero_ref)

        def tail_copy(e):
            start = pl.multiple_of(tail_ref[e], TM_E)
            return pltpu.make_async_copy(zero_ref, xs_ref.at[pl.ds(start, TM_E)], zsem)

        for e in range(2 * NE):
            @pl.when(need_ref[e] > 0)
            def _():
                tail_copy(e).start()
        for e in range(2 * NE):
            @pl.when(need_ref[e] > 0)
            def _():
                tail_copy(e).wait()

    def start_row(r):
        _row_copy(hbuf.at[slot], r, xs_ref, pos0_ref[0, r], sem.at[slot]).start()
        _row_copy(hbuf.at[slot], r, xs_ref, pos1_ref[0, r], sem.at[slot]).start()

    _issue_rows(tm, start_row)

    def drain(s):
        for _ in range(2):
            pltpu.make_async_copy(hbuf.at[s], xs_ref.at[pl.ds(0, tm)], sem.at[s]).wait()

    @pl.when(i > 0)
    def _():
        drain(1 - slot)

    @pl.when(i == pl.num_programs(0) - 1)
    def _():
        drain(slot)


def _pos_specs(tm, index):
    return [pl.BlockSpec((None, None, 1, tm), lambda i, *_, k=k: (k, index(i), 0, 0), memory_space=pltpu.SMEM)
            for k in range(2)]


def _dispatch(tail, need, pos, h2, n_rows):
    tm = TM_TOK
    return pl.pallas_call(
        _dispatch_body,
        out_shape=jax.ShapeDtypeStruct((_n_expert_tiles(n_rows) * TM_E, D), F32),
        grid_spec=pltpu.PrefetchScalarGridSpec(
            num_scalar_prefetch=2,
            grid=(n_rows // tm,),
            in_specs=_pos_specs(tm, lambda i: i) + [pl.BlockSpec((tm, D), lambda i, t, n: (i, 0))],
            out_specs=pl.BlockSpec(memory_space=pl.ANY),
            scratch_shapes=[pltpu.VMEM((2, tm, D), F32), pltpu.VMEM((TM_E, D), F32),
                            pltpu.SemaphoreType.DMA((2,)), pltpu.SemaphoreType.DMA(())],
        ),
        compiler_params=_cparams(("arbitrary",), 32),
        name="dispatch",
    )(tail, need, pos, pos, h2)


def _moe_body(te_ref, nu_ref, first_ref, nxt_ref, rem_ref, xs_ref, wu_hbm, wd_hbm, ys_ref,
              wub, wdb, stage, sem, st, *, layer):
    r = pl.program_id(0)

    def chunk_copy(e, k):
        if k < MOE_UP_CHUNKS:
            src = wu_hbm.at[layer, e, pl.ds(k * MOE_W_CHUNK, MOE_W_CHUNK), :]
        else:
            src = wd_hbm.at[layer, e, pl.ds((k - MOE_UP_CHUNKS) * MOE_W_CHUNK, MOE_W_CHUNK), :]
        return pltpu.make_async_copy(src, stage.at[k % 2], sem.at[k % 2])

    def convert(e, k, slot):
        chunk_copy(e, k).wait()
        v = stage[k % 2].astype(BF16)
        if k < MOE_UP_CHUNKS:
            wub[slot, k * MOE_W_CHUNK:(k + 1) * MOE_W_CHUNK, :] = v
        else:
            kd = k - MOE_UP_CHUNKS
            wdb[slot, kd * MOE_W_CHUNK:(kd + 1) * MOE_W_CHUNK, :] = v
        if k + 2 < MOE_CHUNKS:
            chunk_copy(e, k + 2).start()

    def begin(e):
        chunk_copy(e, 0).start()
        chunk_copy(e, 1).start()
        st[1] = 0

    @pl.when(r < nu_ref[0])
    def _():
        e_cur = te_ref[r]
        e_nxt = nxt_ref[r]

        @pl.when(r == 0)
        def _():
            st[0] = 1
            begin(e_cur)

        @pl.when(first_ref[r] == 1)
        def _():
            slot = 1 - st[0]
            done = st[1]
            for k in range(MOE_CHUNKS):
                @pl.when(k >= done)
                def _():
                    convert(e_cur, k, slot)
            st[0] = slot
            st[1] = MOE_CHUNKS

            @pl.when(e_nxt >= 0)
            def _():
                begin(e_nxt)

        cur = st[0]
        x = xs_ref[...].astype(BF16)
        hc = jnp.dot(x, wub[cur], preferred_element_type=F32)
        a = hc[:, :DFF]
        b = hc[:, DFF:]
        act = (a * _sigmoid(a) * b).astype(BF16)
        ys_ref[...] = jnp.dot(act, wdb[cur], preferred_element_type=F32)

        @pl.when(e_nxt >= 0)
        def _():
            done = st[1]
            share = (MOE_CHUNKS - done + rem_ref[r] - 1) // rem_ref[r]
            for k in range(MOE_CHUNKS):
                @pl.when(jnp.logical_and(k >= done, k < done + share))
                def _():
                    convert(e_nxt, k, 1 - cur)
            st[1] = done + share

    @pl.when(r >= nu_ref[0])
    def _():
        ys_ref[...] = jnp.zeros_like(ys_ref)


def _moe(plan, xs, wu, wd, layer):
    def row(r, te, nu, *_):
        return (jnp.minimum(r, nu[0] - 1), 0)

    return pl.pallas_call(
        functools.partial(_moe_body, layer=layer),
        out_shape=jax.ShapeDtypeStruct(xs.shape, F32),
        grid_spec=pltpu.PrefetchScalarGridSpec(
            num_scalar_prefetch=5,
            grid=(xs.shape[0] // TM_E,),
            in_specs=[
                pl.BlockSpec((TM_E, D), row),
                pl.BlockSpec(memory_space=pl.ANY),
                pl.BlockSpec(memory_space=pl.ANY),
            ],
            out_specs=pl.BlockSpec((TM_E, D), lambda r, *_: (r, 0)),
            scratch_shapes=[
                pltpu.VMEM((2, D, 2 * DFF), BF16),
                pltpu.VMEM((2, DFF, D), BF16),
                pltpu.VMEM((2, MOE_W_CHUNK, D), F32),
                pltpu.SemaphoreType.DMA((2,)),
                pltpu.SMEM((2,), I32),
            ],
        ),
        compiler_params=_cparams(("arbitrary",), 56),
        name="moe",
    )(*plan, xs, wu, wd)


def _combine_body(pos0_ref, pos1_ref, posn0_ref, posn1_ref, ys_ref, x_ref, w_ref, mod_ref, g_ref, nmod_ref,
                  *rest, final):
    tm = TM_TOK
    outs, (ybuf, sem) = rest[:-2], rest[-2:]
    i = pl.program_id(0)
    slot = i % 2

    def fetch(p0_ref, p1_ref, s):
        def start_row(r):
            _row_copy(ys_ref, p0_ref[0, r], ybuf.at[s, 0], r, sem.at[s]).start()
            _row_copy(ys_ref, p1_ref[0, r], ybuf.at[s, 1], r, sem.at[s]).start()

        _issue_rows(tm, start_row)

    @pl.when(i == 0)
    def _():
        fetch(pos0_ref, pos1_ref, 0)

    @pl.when(i + 1 < pl.num_programs(0))
    def _():
        fetch(posn0_ref, posn1_ref, 1 - slot)

    for k in range(2):
        pltpu.make_async_copy(ys_ref.at[pl.ds(0, tm)], ybuf.at[slot, k], sem.at[slot]).wait()

    reps = D // LANES
    w0 = jnp.tile(w_ref[:, 0:LANES], (1, reps))
    w1 = jnp.tile(w_ref[:, LANES:2 * LANES], (1, reps))
    xn = x_ref[...] + mod_ref[5:6, :] * (w0 * ybuf[slot, 0] + w1 * ybuf[slot, 1])
    if final:
        outs[0][...] = _rms(xn, g_ref[...])
    else:
        outs[0][...] = xn
        outs[1][...] = (_rms(xn, g_ref[...]) * (1.0 + nmod_ref[1:2, :]) + nmod_ref[0:1, :]).astype(BF16)


def _combine(pos, ys, x, wlanes, mods, layer, n_rows, g_next, final):
    tm = TM_TOK
    nt = n_rows // tm
    next_layer = min(layer + 1, DEPTH - 1)
    tile = pl.BlockSpec((tm, D), lambda i: (i, 0))
    out_shape = [jax.ShapeDtypeStruct((n_rows, D), F32)]
    if not final:
        out_shape.append(jax.ShapeDtypeStruct((n_rows, D), BF16))
    return pl.pallas_call(
        functools.partial(_combine_body, final=final),
        out_shape=out_shape,
        grid=(nt,),
        in_specs=_pos_specs(tm, lambda i: i) + _pos_specs(tm, lambda i: jnp.minimum(i + 1, nt - 1)) + [
            pl.BlockSpec(memory_space=pl.ANY),
            tile,
            pl.BlockSpec((tm, 2 * LANES), lambda i: (i, 0)),
            pl.BlockSpec((None, None, 6, D), lambda i: (layer, _mod_row(i, tm), 0, 0)),
            pl.BlockSpec((1, D), lambda i: (0, 0)),
            pl.BlockSpec((None, None, 6, D), lambda i: (next_layer, _mod_row(i, tm), 0, 0)),
        ],
        out_specs=[tile] * len(out_shape),
        scratch_shapes=[pltpu.VMEM((2, 2, tm, D), F32), pltpu.SemaphoreType.DMA((2,))],
        compiler_params=_cparams(("arbitrary",), 40),
        name="combine",
    )(pos, pos, pos, pos, ys, x, wlanes, mods, g_next.reshape(1, D), mods)


def _route_plan(route, cnt, n_rows):
    counts = cnt[:, 0].astype(I32)
    padded = ((counts + TM_E - 1) // TM_E) * TM_E
    ends = jnp.cumsum(padded)
    offs = ends - padded
    e01 = route[0:2].astype(I32)
    eids = jnp.arange(NE, dtype=I32)[:, None, None]
    off01 = jnp.sum(jnp.where(e01[None] == eids, offs[:, None, None], 0), axis=0)
    pos = off01 + route[2:4].astype(I32)
    pos = pos.reshape(2, n_rows // TM_TOK, 1, TM_TOK)
    nt = _n_expert_tiles(n_rows)
    tile_start = jnp.arange(nt, dtype=I32) * TM_E
    tile_expert = jnp.minimum(jnp.sum((tile_start[:, None] >= ends[None, :]).astype(I32), axis=1), NE - 1)
    n_used = (ends[-1] // TM_E).reshape(1).astype(I32)
    spare = n_used[0] + jnp.arange(NE, dtype=I32)
    zstart = jnp.concatenate([offs + (counts // TM_E) * TM_E, jnp.minimum(spare, nt - 1) * TM_E])
    zneed = jnp.concatenate([counts % TM_E != 0, spare < nt]).astype(I32)
    wl = jnp.concatenate([jnp.broadcast_to(route[4][:, None], (n_rows, LANES)),
                          jnp.broadcast_to(route[5][:, None], (n_rows, LANES))], axis=1)
    ntile = padded // TM_E
    ecol = jnp.arange(NE, dtype=I32)
    onehot = tile_expert[:, None] == ecol[None, :]
    pick = lambda v: jnp.sum(jnp.where(onehot, v[None, :], 0), axis=1)
    j_in = jnp.arange(nt, dtype=I32) - pick(offs // TM_E)
    first = (j_in == 0).astype(I32)
    rem = jnp.maximum(pick(ntile) - j_in, 1)
    later = (ecol[None, :] > ecol[:, None]) & (ntile[None, :] > 0)
    nxt_e = jnp.min(jnp.where(later, ecol[None, :], NE), axis=1)
    nxt = pick(jnp.where(nxt_e < NE, nxt_e, -1))
    moe_plan = (tile_expert.astype(I32), n_used, first, nxt.astype(I32), rem.astype(I32))
    return pos, moe_plan, zstart.astype(I32), zneed, wl


def kernel(x, c, ctx, c_ctx, ada_w, ada_b, norm1_g, norm2_g, w_in, attn_sink, sgu_ln_g, sgu_ln_b, sgu_w, sgu_b,
           sconv_w, conf_dw_w, conf_dw_b, conf_ln_g, conf_ln_b, w_branch, w_out, router_w, router_b,
           exp_w_up, exp_w_down, final_g):
    cvec = jnp.concatenate([c, c_ctx[None, :], jnp.zeros((8 - NB - 1, D), F32)], axis=0)
    mods = _ada(cvec, ada_w, ada_b).reshape(DEPTH, 8, 6, D)
    xa, cxa = x.reshape(TX, D), ctx.reshape(TC, D)
    cos, sin = _rope_tables()
    consts = _rope_constants()
    rw = router_w.T.astype(BF16)
    wb_all = w_branch.astype(BF16)
    wo_all = w_out.astype(BF16)

    for l in range(DEPTH):
        last = l == DEPTH - 1
        n_rows = TX if last else T
        n_x_tiles = TX // TM_TOK
        lp = (sgu_ln_g[l], sgu_ln_b[l], sgu_w[l], sgu_b[l], sconv_w[l], conf_dw_w[l], conf_dw_b[l],
              conf_ln_g[l], conf_ln_b[l])
        c_rows0 = 0 if l == 0 else TX

        if l == 0:
            h = _norm_mod(xa, cxa, norm1_g[l], mods, l)
        if not last:
            p = _inproj(h, w_in, l, T, N_IN // TN_IN)
            qr, kr, vr = _prep(p, T, cos, sin, consts)
            ya = _attn(qr, kr, vr, kr, vr, TX // LC, attn_sink[l])
            yac = _ctx_attn(qr, kr, vr, attn_sink[l])
        else:
            p = _inproj(h, w_in, l, TX, N_IN // TN_IN)
            qr, kr, vr = _prep(p, TX, cos, sin, consts)
            kc, vc = _ctx_kv(h, w_in, l, consts[1])
            ya = _attn(qr, kr, vr, kc, vc, 0, attn_sink[l])
            yac = ya
        yr = _mixers(p, n_rows, n_x_tiles, lp)
        xn, h2, lg = _post(p, ya, yac, yr, wb_all, wo_all, xa, cxa, c_rows0 // TM_TOK, mods, l,
                           norm2_g[l], rw, n_rows)
        route, cnt = _route(lg, router_b, n_rows)
        pos, moe_plan, zstart, zneed, wl = _route_plan(route, cnt, n_rows)
        xs = _dispatch(zstart, zneed, pos, h2, n_rows)
        ys = _moe(moe_plan, xs, exp_w_up, exp_w_down, l)
        if last:
            (out,) = _combine(pos, ys, xn, wl, mods, l, n_rows, final_g, True)
        else:
            xa, h = _combine(pos, ys, xn, wl, mods, l, n_rows, norm1_g[l + 1], False)
            cxa = xa

    return out.reshape(NB, S, D)
```

```python
import functools

import numpy as np
import jax
import jax.numpy as jnp
from jax import lax
from jax.experimental import pallas as pl
from jax.experimental.pallas import tpu as pltpu

F32 = jnp.float32
BF16 = jnp.bfloat16
I32 = jnp.int32

D = 2048
NB = 4
S = 2048
LC = 256
DEPTH = 2
GRID_W = 64
BW = 512
HD = 64
NH = 8
NKV = 2
REP = NH // NKV
WINDOW = 128
QB = 128
ROPE_THETA = 10000.0
SGU_CHUNK = 128
SGU_GROUPS = 4
SCONV_K = 3
CONF_K = 31
NE = 16
NG = 4
EPG = NE // NG
DFF = D // 2
N_IN = BW + 2 * NKV * HD + 7 * BW + 4 * D
TX = NB * S
TC = NB * LC
T = TX + TC

OFF_KV = BW
OFF_SU, OFF_SV, OFF_CB, OFF_CC, OFF_CX, OFF_GA, OFF_GB, OFF_GATES = 768, 1280, 1792, 2304, 2816, 3328, 3840, 4352

LANES = 128
SUBLANES = 8
MIB = 1024 * 1024

TM_NORM = 512
TM_IN = 1024
INPROJ_ROW_CHUNK = 512
IN_W_CHUNK = 256
IN_CHUNKS = D // IN_W_CHUNK
TN_IN = 1792
TM_TOK = 256
TM_PREP = 512
TM_PERM = 512
TM_E = 256
HALO = 16
ROW_UNROLL = 8
MOE_W_CHUNK = 512
MOE_UP_CHUNKS = D // MOE_W_CHUNK
MOE_CHUNKS = MOE_UP_CHUNKS + DFF // MOE_W_CHUNK


def _n_expert_tiles(n_rows):
    return (2 * n_rows) // TM_E + NE


def _cparams(sem, vmem_mib):
    return pltpu.CompilerParams(dimension_semantics=sem, vmem_limit_bytes=vmem_mib * MIB)


def _sigmoid(x):
    return 0.5 * jnp.tanh(0.5 * x) + 0.5


def _gelu_tanh(x):
    c = np.float32(np.sqrt(2.0 / np.pi))
    return 0.5 * x * (1.0 + jnp.tanh(c * (x + np.float32(0.044715) * (x * x * x))))


def _layernorm(x, g, b, eps=1e-5):
    mu = jnp.mean(x, axis=-1, keepdims=True)
    xc = x - mu
    var = jnp.mean(xc * xc, axis=-1, keepdims=True)
    return xc * lax.rsqrt(var + eps) * g + b


def _mod_row(i, tm):
    return jnp.where(i < TX // tm, (i * tm) // S, NB)


def _ada_body(c_ref, w_ref, b_ref, o_ref):
    c = c_ref[...]
    s = (c * _sigmoid(c)).astype(BF16)
    o_ref[...] = jnp.dot(s, w_ref[...].astype(BF16), preferred_element_type=F32) + b_ref[...]


def _ada(cvec, ada_w, ada_b):
    tn = 1024
    return pl.pallas_call(
        _ada_body,
        out_shape=jax.ShapeDtypeStruct((DEPTH, 8, 6 * D), F32),
        grid=(DEPTH, 6 * D // tn),
        in_specs=[
            pl.BlockSpec((8, D), lambda l, j: (0, 0)),
            pl.BlockSpec((None, D, tn), lambda l, j: (l, 0, j)),
            pl.BlockSpec((None, 1, tn), lambda l, j: (l, 0, j)),
        ],
        out_specs=pl.BlockSpec((None, 8, tn), lambda l, j: (l, 0, j)),
        compiler_params=_cparams(("arbitrary", "arbitrary"), 40),
        name="ada",
    )(cvec, ada_w, ada_b.reshape(DEPTH, 1, 6 * D))


def _stream_specs(tm, c_blk0, width=D):
    nx = TX // tm
    return [pl.BlockSpec((tm, width), lambda i, *_: (jnp.minimum(i, nx - 1), 0)),
            pl.BlockSpec((tm, width), lambda i, *_: (c_blk0 + jnp.maximum(i - nx, 0), 0))]


def _stream_tile(x_ref, c_ref, tm):
    return jnp.where(pl.program_id(0) < TX // tm, x_ref[...], c_ref[...])


def _rms(x, g):
    return x * lax.rsqrt(jnp.mean(x * x, axis=-1, keepdims=True) + 1e-6) * g


def _norm_mod_body(x_ref, c_ref, g_ref, mod_ref, o_ref):
    y = _rms(_stream_tile(x_ref, c_ref, TM_NORM), g_ref[...])
    o_ref[...] = (y * (1.0 + mod_ref[1:2, :]) + mod_ref[0:1, :]).astype(o_ref.dtype)


def _norm_mod(x, cx, g, mods, layer):
    tm = TM_NORM
    return pl.pallas_call(
        _norm_mod_body,
        out_shape=jax.ShapeDtypeStruct((T, D), BF16),
        grid=(T // tm,),
        in_specs=_stream_specs(tm, 0) + [
            pl.BlockSpec((1, D), lambda i: (0, 0)),
            pl.BlockSpec((None, None, 6, D), lambda i: (layer, _mod_row(i, tm), 0, 0)),
        ],
        out_specs=pl.BlockSpec((tm, D), lambda i: (i, 0)),
        compiler_params=_cparams(("arbitrary",), 32),
        name="norm_mod",
    )(x, cx, g.reshape(1, D), mods)


def _inproj_body(h_ref, w_hbm, o_ref, wbf, stage, sem, st, *, layer):
    j = pl.program_id(0)
    i = pl.program_id(1)
    nj = pl.num_programs(0)
    ni = pl.num_programs(1)

    def chunk_copy(jj, k):
        cols = pl.ds(pl.multiple_of(jj * TN_IN, LANES), TN_IN)
        src = w_hbm.at[layer, pl.ds(k * IN_W_CHUNK, IN_W_CHUNK), cols]
        return pltpu.make_async_copy(src, stage.at[k % 2], sem.at[k % 2])

    def convert(jj, k, slot):
        chunk_copy(jj, k).wait()
        wbf[slot, k * IN_W_CHUNK:(k + 1) * IN_W_CHUNK, :] = stage[k % 2].astype(BF16)
        if k + 2 < IN_CHUNKS:
            chunk_copy(jj, k + 2).start()

    def begin(jj):
        chunk_copy(jj, 0).start()
        chunk_copy(jj, 1).start()
        st[1] = 0

    @pl.when(jnp.logical_and(i == 0, j == 0))
    def _():
        st[0] = 1
        begin(0)

    @pl.when(i == 0)
    def _():
        slot = 1 - st[0]
        done = st[1]
        for k in range(IN_CHUNKS):
            @pl.when(k >= done)
            def _():
                convert(j, k, slot)
        st[0] = slot
        st[1] = IN_CHUNKS

        @pl.when(j + 1 < nj)
        def _():
            begin(j + 1)

    cur = st[0]
    col0 = j * TN_IN
    rows = INPROJ_ROW_CHUNK

    def run(epilogue):
        for c in range(h_ref.shape[0] // rows):
            sl = slice(c * rows, (c + 1) * rows)
            acc = jnp.dot(h_ref[sl, :], wbf[cur], preferred_element_type=F32)
            o_ref[sl, :] = epilogue(acc).astype(BF16)

    @pl.when(col0 + TN_IN <= OFF_GB)
    def _():
        run(lambda acc: acc)

    @pl.when(col0 >= OFF_GB)
    def _():
        run(_sigmoid)

    @pl.when(jnp.logical_and(col0 < OFF_GB, col0 + TN_IN > OFF_GB))
    def _():
        col = col0 + lax.broadcasted_iota(I32, (rows, TN_IN), 1)
        run(lambda acc: jnp.where(col >= OFF_GB, _sigmoid(acc), acc))

    @pl.when(j + 1 < nj)
    def _():
        done = st[1]
        left = ni - i
        share = (IN_CHUNKS - done + left - 1) // left
        for k in range(IN_CHUNKS):
            @pl.when(jnp.logical_and(k >= done, k < done + share))
            def _():
                convert(j + 1, k, 1 - cur)
        st[1] = done + share


def _inproj(h, w_in, layer, n_rows, n_col_tiles):
    tm = TM_IN
    return pl.pallas_call(
        functools.partial(_inproj_body, layer=layer),
        out_shape=jax.ShapeDtypeStruct((n_rows, n_col_tiles * TN_IN), BF16),
        grid=(n_col_tiles, n_rows // tm),
        in_specs=[
            pl.BlockSpec((tm, D), lambda j, i: (i, 0)),
            pl.BlockSpec(memory_space=pl.ANY),
        ],
        out_specs=pl.BlockSpec((tm, TN_IN), lambda j, i: (i, j)),
        scratch_shapes=[
            pltpu.VMEM((2, D, TN_IN), BF16),
            pltpu.VMEM((2, IN_W_CHUNK, TN_IN), F32),
            pltpu.SemaphoreType.DMA((2,)),
            pltpu.SMEM((2,), I32),
        ],
        compiler_params=_cparams(("arbitrary", "arbitrary"), 56),
        name="inproj",
    )(h, w_in)


def _rope_constants():
    rh = np.zeros((HD, HD), np.float32)
    for base in (0, 32):
        for d in range(16):
            rh[base + 16 + d, base + d] = -1.0
            rh[base + d, base + 16 + d] = 1.0
    rq = np.kron(np.eye(NH, dtype=np.float32), rh)
    rk = np.kron(np.eye(NKV, dtype=np.float32), rh)
    rep = np.zeros((NKV * HD, NH * HD), np.float32)
    for g in range(NKV):
        for r in range(REP):
            for d in range(HD):
                rep[g * HD + d, g * REP * HD + r * HD + d] = 1.0
    return jnp.asarray(rq, BF16), jnp.asarray(rep, BF16), jnp.asarray(rk @ rep, BF16)


def _rope_tables():
    half = HD // 2
    inv = 1.0 / (ROPE_THETA ** (jnp.arange(0, half, 2, dtype=F32) / half))
    pos = jnp.arange(S)
    ar = (pos // GRID_W).astype(F32)[:, None] * inv
    ac = (pos % GRID_W).astype(F32)[:, None] * inv
    reps = LANES // HD
    cos = jnp.tile(jnp.concatenate([jnp.cos(ar), jnp.cos(ar), jnp.cos(ac), jnp.cos(ac)], axis=1), (1, reps))
    sin = jnp.tile(jnp.concatenate([jnp.sin(ar), jnp.sin(ar), jnp.sin(ac), jnp.sin(ac)], axis=1), (1, reps))
    cos = jnp.concatenate([cos, jnp.ones((TM_PREP, LANES), F32)], axis=0)
    sin = jnp.concatenate([sin, jnp.zeros((TM_PREP, LANES), F32)], axis=0)
    return cos, sin


def _prep_body(q_ref, kv_ref, cos_ref, sin_ref, rq_ref, rep_ref, rrep_ref, qo_ref, ko_ref, vo_ref):
    cos = jnp.tile(cos_ref[...], (1, NH * HD // LANES))
    sin = jnp.tile(sin_ref[...], (1, NH * HD // LANES))
    q = q_ref[...]
    qs = jnp.dot(q, rq_ref[...], preferred_element_type=F32)
    qo_ref[...] = ((q.astype(F32) * cos + qs * sin) * (HD ** -0.5)).astype(BF16)
    k = kv_ref[:, 0:NKV * HD]
    v = kv_ref[:, NKV * HD:2 * NKV * HD]
    kr = jnp.dot(k, rep_ref[...], preferred_element_type=F32)
    ks = jnp.dot(k, rrep_ref[...], preferred_element_type=F32)
    ko_ref[...] = (kr * cos + ks * sin).astype(BF16)
    vo_ref[...] = jnp.dot(v, rep_ref[...], preferred_element_type=F32).astype(BF16)


def _prep(p, n_rows, cos, sin, consts):
    tm = TM_PREP
    rq, rep, rrep = consts
    tps = S // tm

    def tab(i):
        return (jnp.where(i < TX // tm, i % tps, tps), 0)

    w = NH * HD
    return pl.pallas_call(
        _prep_body,
        out_shape=[jax.ShapeDtypeStruct((n_rows, w), BF16)] * 3,
        grid=(n_rows // tm,),
        in_specs=[
            pl.BlockSpec((tm, w), lambda i: (i, 0)),
            pl.BlockSpec((tm, 2 * NKV * HD), lambda i: (i, OFF_KV // (2 * NKV * HD))),
            pl.BlockSpec((tm, LANES), tab),
            pl.BlockSpec((tm, LANES), tab),
            pl.BlockSpec((w, w), lambda i: (0, 0)),
            pl.BlockSpec((NKV * HD, w), lambda i: (0, 0)),
            pl.BlockSpec((NKV * HD, w), lambda i: (0, 0)),
        ],
        out_specs=[pl.BlockSpec((tm, w), lambda i: (i, 0))] * 3,
        compiler_params=_cparams(("arbitrary",), 32),
        name="prep",
    )(p, p, cos, sin, rq, rep, rrep)


def _ctx_kv_body(h_ref, w_ref, rep_ref, ko_ref, vo_ref):
    kv = jnp.dot(h_ref[...], w_ref[...].astype(BF16), preferred_element_type=F32).astype(BF16)
    ko_ref[...] = jnp.dot(kv[:, 0:NKV * HD], rep_ref[...], preferred_element_type=F32).astype(BF16)
    vo_ref[...] = jnp.dot(kv[:, NKV * HD:2 * NKV * HD], rep_ref[...], preferred_element_type=F32).astype(BF16)


def _ctx_kv(h, w_in, layer, rep):
    tm = TM_TOK
    w = NH * HD
    kvw = 2 * NKV * HD
    return pl.pallas_call(
        _ctx_kv_body,
        out_shape=[jax.ShapeDtypeStruct((TC, w), BF16)] * 2,
        grid=(TC // tm,),
        in_specs=[
            pl.BlockSpec((tm, D), lambda i: (TX // tm + i, 0)),
            pl.BlockSpec((None, pl.Element(D), pl.Element(kvw)), lambda i: (layer, 0, OFF_KV)),
            pl.BlockSpec((NKV * HD, w), lambda i: (0, 0)),
        ],
        out_specs=[pl.BlockSpec((tm, w), lambda i: (i, 0))] * 2,
        compiler_params=_cparams(("arbitrary",), 32),
        name="ctx_kv",
    )(h, w_in, rep)


def _attn_groups(q, kbs, vbs, sink_ref, valid, nq):
    gw = REP * HD
    lane_head = lax.broadcasted_iota(I32, (nq, gw), 1) // HD
    row_head = lax.broadcasted_iota(I32, (REP * nq, 1), 0) // nq
    valid_rep = None if valid is None else jnp.concatenate([valid] * REP, axis=0)
    scores = []
    for g in range(NKV):
        qg = q[:, g * gw:(g + 1) * gw]
        qs = jnp.concatenate([jnp.where(lane_head == r, qg, jnp.zeros_like(qg)) for r in range(REP)], axis=0)
        s = lax.dot_general(qs, kbs[g], (((1,), (1,)), ((), ())), preferred_element_type=F32)
        scores.append(s if valid_rep is None else jnp.where(valid_rep, s, -jnp.inf))
    probs = []
    for g in range(NKV):
        s = scores[g]
        sink = jnp.zeros((REP * nq, 1), F32)
        for r in range(REP):
            sink = jnp.where(row_head == r, sink_ref[g * REP + r], sink)
        m = jnp.maximum(jnp.max(s, axis=-1, keepdims=True), sink)
        e = jnp.exp(s - m)
        den = jnp.sum(e, axis=-1, keepdims=True) + jnp.exp(sink - m)
        probs.append((e * (1.0 / den)).astype(BF16))
    outs = []
    for g in range(NKV):
        o = jnp.dot(probs[g], vbs[g], preferred_element_type=F32)
        og = jnp.zeros((nq, gw), F32)
        for r in range(REP):
            og = og + jnp.where(lane_head == r, o[r * nq:(r + 1) * nq, :], 0.0)
        outs.append(og)
    return jnp.concatenate(outs, axis=1).astype(BF16)


def _attn_body(sink_ref, q_ref, kp_ref, kc_ref, kn_ref, vp_ref, vc_ref, vn_ref, kx_ref, vx_ref, o_ref):
    n = pl.program_id(1)
    nblk = S // QB
    nk = 3 * QB + LC
    row = lax.broadcasted_iota(I32, (QB, nk), 0)
    col = lax.broadcasted_iota(I32, (QB, nk), 1)
    lo = jnp.where(n == 0, QB, 0)
    hi = jnp.where(n == nblk - 1, 2 * QB, 3 * QB)
    band = (col >= row) & (col <= row + 2 * WINDOW) & (col >= lo) & (col < hi)
    valid = band | (col >= 3 * QB)
    gw = REP * HD
    kbs, vbs = [], []
    for g in range(NKV):
        sl = slice(g * gw, (g + 1) * gw)
        kbs.append(jnp.concatenate([kp_ref[:, sl], kc_ref[:, sl], kn_ref[:, sl], kx_ref[:, sl]], axis=0))
        vbs.append(jnp.concatenate([vp_ref[:, sl], vc_ref[:, sl], vn_ref[:, sl], vx_ref[:, sl]], axis=0))
    o_ref[...] = _attn_groups(q_ref[...], kbs, vbs, sink_ref, valid, QB)


def _attn(qr, kr, vr, kc_arr, vc_arr, ctx_blk0, sink):
    nblk = S // QB
    w = NH * HD

    def cur(b, n):
        return (b * nblk + n, 0)

    def prev(b, n):
        return (b * nblk + jnp.maximum(n - 1, 0), 0)

    def nxt(b, n):
        return (b * nblk + jnp.minimum(n + 1, nblk - 1), 0)

    def cx(b, n):
        return (ctx_blk0 + b, 0)

    blk = lambda f: pl.BlockSpec((QB, w), f)
    return pl.pallas_call(
        _attn_body,
        out_shape=jax.ShapeDtypeStruct((TX, w), BF16),
        grid=(NB, nblk),
        in_specs=[
            pl.BlockSpec(memory_space=pltpu.SMEM),
            blk(cur), blk(prev), blk(cur), blk(nxt), blk(prev), blk(cur), blk(nxt),
            pl.BlockSpec((LC, w), cx), pl.BlockSpec((LC, w), cx),
        ],
        out_specs=blk(cur),
        compiler_params=_cparams(("arbitrary", "arbitrary"), 32),
        name="window_attn",
    )(sink, qr, kr, kr, kr, vr, vr, vr, kc_arr, vc_arr)


def _ctx_attn_body(sink_ref, q_ref, kx_ref, vx_ref, o_ref):
    gw = REP * HD
    kbs = [kx_ref[:, g * gw:(g + 1) * gw] for g in range(NKV)]
    vbs = [vx_ref[:, g * gw:(g + 1) * gw] for g in range(NKV)]
    o_ref[...] = _attn_groups(q_ref[...], kbs, vbs, sink_ref, None, LC)


def _ctx_attn(qr, kr, vr, sink):
    w = NH * HD
    blk0 = TX // LC
    spec = pl.BlockSpec((LC, w), lambda b: (blk0 + b, 0))
    return pl.pallas_call(
        _ctx_attn_body,
        out_shape=jax.ShapeDtypeStruct((TC, w), BF16),
        grid=(NB,),
        in_specs=[pl.BlockSpec(memory_space=pltpu.SMEM), spec, spec, spec],
        out_specs=pl.BlockSpec((LC, w), lambda b: (b, 0)),
        compiler_params=_cparams(("arbitrary",), 32),
        name="ctx_attn",
    )(sink, qr, kr, vr)


def _mix_body(su_ref, sv_ref, cb_ref, cc_ref, cx_ref, ga_ref, gb_ref,
              ccp_ref, cxp_ref, gap_ref, gbp_ref, ccn_ref, cxn_ref, gan_ref, gbn_ref,
              lng_ref, lnb_ref, sw_ref, sbias_ref, scw_ref, dww_ref, dwb_ref, clg_ref, clb_ref,
              o_ref, m_scr, z_scr, *, n_x_tiles):
    tm = TM_TOK
    i = pl.program_id(0)
    tps = S // tm
    is_x = i < n_x_tiles
    first = jnp.logical_or(jnp.logical_not(is_x), (i % tps) == 0)
    last = jnp.logical_or(jnp.logical_not(is_x), (i % tps) == tps - 1)
    keep_p = jnp.where(first, 0.0, 1.0)
    keep_n = jnp.where(last, 0.0, 1.0)

    u = _gelu_tanh(su_ref[...].astype(F32))
    v = _layernorm(_gelu_tanh(sv_ref[...].astype(F32)), lng_ref[...], lnb_ref[...]).astype(BF16)
    gwid = BW // SGU_GROUPS
    chunks = []
    for c in range(tm // SGU_CHUNK):
        parts = []
        for g in range(SGU_GROUPS):
            vc = v[c * SGU_CHUNK:(c + 1) * SGU_CHUNK, g * gwid:(g + 1) * gwid]
            parts.append(jnp.dot(sw_ref[g], vc, preferred_element_type=F32))
        chunks.append(jnp.concatenate(parts, axis=1) + sbias_ref[...])
    yb = u * jnp.concatenate(chunks, axis=0)

    m_scr[HALO:HALO + tm, :] = cc_ref[...].astype(F32) * cx_ref[...].astype(F32)
    m_scr[0:HALO, :] = ccp_ref[...].astype(F32) * cxp_ref[...].astype(F32) * keep_p
    m_scr[HALO + tm:, :] = ccn_ref[...].astype(F32) * cxn_ref[...].astype(F32) * keep_n
    z = jnp.zeros((tm, BW), F32)
    for k in range(SCONV_K):
        z = z + scw_ref[k:k + 1, :] * m_scr[pl.ds(HALO - SCONV_K // 2 + k, tm), :]
    yc = cb_ref[...].astype(F32) * z

    z_scr[0, HALO:HALO + tm, :] = ga_ref[...].astype(F32) * gb_ref[...].astype(F32)
    z_scr[0, 0:HALO, :] = gap_ref[...].astype(F32) * gbp_ref[...].astype(F32) * keep_p
    z_scr[0, HALO + tm:, :] = gan_ref[...].astype(F32) * gbn_ref[...].astype(F32) * keep_n
    n_sh = tm + 2 * HALO - SUBLANES
    for s in range(1, SUBLANES):
        z_scr[s, 0:n_sh, :] = z_scr[0, pl.ds(s, n_sh), :]
    acc = jnp.zeros((tm, BW), F32) + dwb_ref[...]
    for k in range(CONF_K):
        off = HALO - CONF_K // 2 + k
        base = off - off % SUBLANES
        acc = acc + dww_ref[k:k + 1, :] * z_scr[off % SUBLANES, base:base + tm, :]
    zn = _layernorm(acc, clg_ref[...], clb_ref[...])
    yd = zn * _sigmoid(zn)

    o_ref[...] = jnp.concatenate([yb, yc, yd], axis=1).astype(BF16)


def _mixers(p, n_rows, n_x_tiles, lp):
    tm = TM_TOK
    (sgu_ln_g, sgu_ln_b, sgu_w, sgu_b, sconv_w, conf_dw_w, conf_dw_b, conf_ln_g, conf_ln_b) = lp

    def sec(off):
        return pl.BlockSpec((pl.Element(tm), pl.Element(BW)), lambda i: (i * tm, off))

    hpt = tm // HALO

    def halo_prev(off):
        return pl.BlockSpec((pl.Element(HALO), pl.Element(BW)),
                            lambda i: (jnp.maximum(i * hpt - 1, 0) * HALO, off))

    def halo_next(off):
        return pl.BlockSpec((pl.Element(HALO), pl.Element(BW)),
                            lambda i: (jnp.minimum(i * hpt + hpt, n_rows // HALO - 1) * HALO, off))

    def const(shape):
        return pl.BlockSpec(shape, lambda i: (0,) * len(shape))

    sbias = jnp.repeat(sgu_b.T, BW // SGU_GROUPS, axis=1)
    row = lambda a: a.reshape(1, BW)
    conv_offs = (OFF_CC, OFF_CX, OFF_GA, OFF_GB)
    return pl.pallas_call(
        functools.partial(_mix_body, n_x_tiles=n_x_tiles),
        out_shape=jax.ShapeDtypeStruct((n_rows, 3 * BW), BF16),
        grid=(n_rows // tm,),
        in_specs=[sec(o) for o in (OFF_SU, OFF_SV, OFF_CB, OFF_CC, OFF_CX, OFF_GA, OFF_GB)]
        + [halo_prev(o) for o in conv_offs] + [halo_next(o) for o in conv_offs]
        + [const((1, BW)), const((1, BW)), const((SGU_GROUPS, SGU_CHUNK, SGU_CHUNK)), const((SGU_CHUNK, BW)),
           const((SCONV_K, BW)), const((CONF_K, BW)), const((1, BW)), const((1, BW)), const((1, BW))],
        out_specs=pl.BlockSpec((tm, 3 * BW), lambda i: (i, 0)),
        scratch_shapes=[pltpu.VMEM((tm + 2 * HALO, BW), F32), pltpu.VMEM((SUBLANES, tm + 2 * HALO, BW), F32)],
        compiler_params=_cparams(("arbitrary",), 32),
        name="mixers",
    )(*([p] * 15), row(sgu_ln_g), row(sgu_ln_b), sgu_w.astype(BF16), sbias, sconv_w, conf_dw_w,
      row(conf_dw_b), row(conf_ln_g), row(conf_ln_b))


def _norm2(xn, n2g, mod):
    return _rms(xn, n2g) * (1.0 + mod[4:5, :]) + mod[3:4, :]


def _post_body(g_ref, ya_ref, yac_ref, yr_ref, wb_ref, wo_ref, x_ref, cx_ref, mod_ref, n2g_ref, rw_ref,
               xo_ref, h2_ref, lg_ref):
    tm = TM_TOK
    merged = None
    for br in range(4):
        y = _stream_tile(ya_ref, yac_ref, tm) if br == 0 else yr_ref[:, (br - 1) * BW:br * BW]
        pr = jnp.dot(y, wb_ref[br], preferred_element_type=F32)
        term = g_ref[:, br * D:(br + 1) * D].astype(F32) * pr
        merged = term if merged is None else merged + term
    out = jnp.dot(merged.astype(BF16), wo_ref[...], preferred_element_type=F32)
    xn = _stream_tile(x_ref, cx_ref, tm) + mod_ref[2:3, :] * out
    xo_ref[...] = xn
    h2 = _norm2(xn, n2g_ref[...], mod_ref)
    h2_ref[...] = h2
    nt = (((1,), (1,)), ((), ()))
    lg_ref[...] = lax.dot_general(rw_ref[...], h2.astype(BF16), nt, preferred_element_type=F32)


def _post(p, ya, yac, yr, wb, wo, x, cx, c_blk0, mods, layer, n2g, rw, n_rows):
    tm = TM_TOK
    one = pl.Buffered(1)
    return pl.pallas_call(
        _post_body,
        out_shape=[jax.ShapeDtypeStruct((n_rows, D), F32), jax.ShapeDtypeStruct((n_rows, D), F32),
                   jax.ShapeDtypeStruct((NE, n_rows), F32)],
        grid=(n_rows // tm,),
        in_specs=[pl.BlockSpec((pl.Element(tm), pl.Element(4 * D)), lambda i: (i * tm, OFF_GATES))]
        + _stream_specs(tm, 0, BW) + [
            pl.BlockSpec((tm, 3 * BW), lambda i: (i, 0)),
            pl.BlockSpec((None, 4, BW, D), lambda i: (layer, 0, 0, 0), pipeline_mode=one),
            pl.BlockSpec((None, D, D), lambda i: (layer, 0, 0), pipeline_mode=one)]
        + _stream_specs(tm, c_blk0) + [
            pl.BlockSpec((None, None, 6, D), lambda i: (layer, _mod_row(i, tm), 0, 0)),
            pl.BlockSpec((1, D), lambda i: (0, 0)),
            pl.BlockSpec((NE, D), lambda i: (0, 0)),
        ],
        out_specs=[pl.BlockSpec((tm, D), lambda i: (i, 0)), pl.BlockSpec((tm, D), lambda i: (i, 0)),
                   pl.BlockSpec((NE, tm), lambda i: (0, i))],
        compiler_params=_cparams(("arbitrary",), 52),
        name="post",
    )(p, ya, yac, yr, wb, wo, x, cx, mods, n2g.reshape(1, D), rw)


def _route_body(lg_ref, rb_ref, tri_ref, o_ref, cnt_ref, carry_ref):
    tm = TM_TOK
    i = pl.program_id(0)

    @pl.when(i == 0)
    def _():
        carry_ref[...] = jnp.zeros_like(carry_ref)

    lg = lg_ref[...]
    e = jnp.exp(lg - jnp.max(lg, axis=0, keepdims=True))
    sc = e / jnp.sum(e, axis=0, keepdims=True)
    bi = sc + rb_ref[...]
    b = [bi[k:k + 1, :] for k in range(NE)]
    s = [sc[k:k + 1, :] for k in range(NE)]

    gs = []
    for g in range(NG):
        v = b[g * EPG:(g + 1) * EPG]
        best = None
        for a in range(EPG):
            for c in range(a + 1, EPG):
                ps = v[a] + v[c]
                best = ps if best is None else jnp.maximum(best, ps)
        gs.append(best)
    gsel = jnp.zeros((1, tm), I32)
    gbest = gs[0]
    for g in range(1, NG):
        take = gs[g] > gbest
        gsel = jnp.where(take, g, gsel)
        gbest = jnp.where(take, gs[g], gbest)

    vb, vs = [], []
    for j in range(EPG):
        xb, xs = b[j], s[j]
        for g in range(1, NG):
            xb = jnp.where(gsel == g, b[g * EPG + j], xb)
            xs = jnp.where(gsel == g, s[g * EPG + j], xs)
        vb.append(xb)
        vs.append(xs)
    order = []
    for j in range(EPG):
        c = jnp.zeros((1, tm), I32)
        for m in range(EPG):
            if m == j:
                continue
            ahead = (vb[m] >= vb[j]) if m < j else (vb[m] > vb[j])
            c = c + jnp.where(ahead, 1, 0)
        order.append(c)
    zero = jnp.zeros((1, tm), F32)
    w0 = zero
    w1 = zero
    j0 = jnp.zeros((1, tm), I32)
    j1 = jnp.zeros((1, tm), I32)
    for j in range(EPG):
        w0 = jnp.where(order[j] == 0, vs[j], w0)
        w1 = jnp.where(order[j] == 1, vs[j], w1)
        j0 = jnp.where(order[j] == 0, j, j0)
        j1 = jnp.where(order[j] == 1, j, j1)
    tot = w0 + w1
    e0 = gsel * EPG + j0
    e1 = gsel * EPG + j1

    eid = lax.broadcasted_iota(I32, (NE, tm), 0)
    oh0 = eid == e0
    oh1 = eid == e1
    oh = jnp.where(oh0 | oh1, 1.0, 0.0)
    rank = jnp.dot(oh.astype(BF16), tri_ref[...], preferred_element_type=F32) + carry_ref[:, 0:1]
    r0 = jnp.sum(jnp.where(oh0, rank, 0.0), axis=0, keepdims=True)
    r1 = jnp.sum(jnp.where(oh1, rank, 0.0), axis=0, keepdims=True)
    new_carry = carry_ref[...] + jnp.sum(oh, axis=1, keepdims=True)
    carry_ref[...] = new_carry
    cnt_ref[...] = new_carry

    o_ref[0:1, :] = e0.astype(F32)
    o_ref[1:2, :] = e1.astype(F32)
    o_ref[2:3, :] = r0
    o_ref[3:4, :] = r1
    o_ref[4:5, :] = w0 / tot
    o_ref[5:6, :] = w1 / tot
    o_ref[6:8, :] = jnp.zeros((2, tm), F32)


def _route(lg, router_b, n_rows):
    tm = TM_TOK
    tri = jnp.asarray(np.triu(np.ones((tm, tm), np.float32), 1), BF16)
    return pl.pallas_call(
        _route_body,
        out_shape=[jax.ShapeDtypeStruct((8, n_rows), F32), jax.ShapeDtypeStruct((NE, LANES), F32)],
        grid=(n_rows // tm,),
        in_specs=[
            pl.BlockSpec((NE, tm), lambda i: (0, i)),
            pl.BlockSpec((NE, 1), lambda i: (0, 0)),
            pl.BlockSpec((tm, tm), lambda i: (0, 0)),
        ],
        out_specs=[pl.BlockSpec((8, tm), lambda i: (0, i)), pl.BlockSpec((NE, LANES), lambda i: (0, 0))],
        scratch_shapes=[pltpu.VMEM((NE, LANES), F32)],
        compiler_params=_cparams(("arbitrary",), 32),
        name="route",
    )(lg, router_b.reshape(NE, 1), tri)


def _row_copy(src_ref, src_row, dst_ref, dst_row, sem):
    return pltpu.make_async_copy(src_ref.at[pl.ds(src_row, 1)], dst_ref.at[pl.ds(dst_row, 1)], sem)


def _issue_rows(n, start_row):
    def trip(t, c):
        base = pl.multiple_of(t * ROW_UNROLL, ROW_UNROLL)
        for u in range(ROW_UNROLL):
            start_row(base + u)
        return c

    lax.fori_loop(0, n // ROW_UNROLL, trip, 0)


def _dispatch_body(tail_ref, need_ref, pos0_ref, pos1_ref, h_ref, xs_ref, hbuf, zero_ref, sem, zsem):
    tm = TM_PERM
    i = pl.program_id(0)
    slot = i % 2
    hbuf[slot] = h_ref[...]

    @pl.when(pl.program_id(0) == 0)
    def _():
        zero_ref[...] = jnp.zeros_like(zero_ref)

        def tail_copy(e):
            start = pl.multiple_of(tail_ref[e], TM_E)
            return pltpu.make_async_copy(zero_ref, xs_ref.at[pl.ds(start, TM_E)], zsem)

        for e in range(2 * NE):
            @pl.when(need_ref[e] > 0)
            def _():
                tail_copy(e).start()
        for e in range(2 * NE):
            @pl.when(need_ref[e] > 0)
            def _():
                tail_copy(e).wait()

    def start_row(r):
        _row_copy(hbuf.at[slot], r, xs_ref, pos0_ref[0, r], sem.at[slot]).start()
        _row_copy(hbuf.at[slot], r, xs_ref, pos1_ref[0, r], sem.at[slot]).start()

    _issue_rows(tm, start_row)

    def drain(s):
        for _ in range(2):
            pltpu.make_async_copy(hbuf.at[s], xs_ref.at[pl.ds(0, tm)], sem.at[s]).wait()

    @pl.when(i > 0)
    def _():
        drain(1 - slot)

    @pl.when(i == pl.num_programs(0) - 1)
    def _():
        drain(slot)


def _pos_specs(tm, index):
    return [pl.BlockSpec((None, None, 1, tm), lambda i, *_, k=k: (k, index(i), 0, 0), memory_space=pltpu.SMEM)
            for k in range(2)]


def _dispatch(tail, need, pos, h2, n_rows):
    tm = TM_PERM
    return pl.pallas_call(
        _dispatch_body,
        out_shape=jax.ShapeDtypeStruct((_n_expert_tiles(n_rows) * TM_E, D), F32),
        grid_spec=pltpu.PrefetchScalarGridSpec(
            num_scalar_prefetch=2,
            grid=(n_rows // tm,),
            in_specs=_pos_specs(tm, lambda i: i) + [pl.BlockSpec((tm, D), lambda i, t, n: (i, 0))],
            out_specs=pl.BlockSpec(memory_space=pl.ANY),
            scratch_shapes=[pltpu.VMEM((2, tm, D), F32), pltpu.VMEM((TM_E, D), F32),
                            pltpu.SemaphoreType.DMA((2,)), pltpu.SemaphoreType.DMA(())],
        ),
        compiler_params=_cparams(("arbitrary",), 32),
        name="dispatch",
    )(tail, need, pos, pos, h2)


def _moe_body(te_ref, nu_ref, first_ref, nxt_ref, rem_ref, xs_ref, wu_hbm, wd_hbm, ys_ref,
              wub, wdb, stage, sem, st, *, layer):
    r = pl.program_id(0)

    def chunk_copy(e, k):
        if k < MOE_UP_CHUNKS:
            src = wu_hbm.at[layer, e, pl.ds(k * MOE_W_CHUNK, MOE_W_CHUNK), :]
        else:
            src = wd_hbm.at[layer, e, pl.ds((k - MOE_UP_CHUNKS) * MOE_W_CHUNK, MOE_W_CHUNK), :]
        return pltpu.make_async_copy(src, stage.at[k % 2], sem.at[k % 2])

    def convert(e, k, slot):
        chunk_copy(e, k).wait()
        v = stage[k % 2].astype(BF16)
        if k < MOE_UP_CHUNKS:
            wub[slot, k * MOE_W_CHUNK:(k + 1) * MOE_W_CHUNK, :] = v
        else:
            kd = k - MOE_UP_CHUNKS
            wdb[slot, kd * MOE_W_CHUNK:(kd + 1) * MOE_W_CHUNK, :] = v
        if k + 2 < MOE_CHUNKS:
            chunk_copy(e, k + 2).start()

    def begin(e):
        chunk_copy(e, 0).start()
        chunk_copy(e, 1).start()
        st[1] = 0

    @pl.when(r < nu_ref[0])
    def _():
        e_cur = te_ref[r]
        e_nxt = nxt_ref[r]

        @pl.when(r == 0)
        def _():
            st[0] = 1
            begin(e_cur)

        @pl.when(first_ref[r] == 1)
        def _():
            slot = 1 - st[0]
            done = st[1]
            for k in range(MOE_CHUNKS):
                @pl.when(k >= done)
                def _():
                    convert(e_cur, k, slot)
            st[0] = slot
            st[1] = MOE_CHUNKS

            @pl.when(e_nxt >= 0)
            def _():
                begin(e_nxt)

        cur = st[0]
        x = xs_ref[...].astype(BF16)
        hc = jnp.dot(x, wub[cur], preferred_element_type=F32)
        a = hc[:, :DFF]
        b = hc[:, DFF:]
        act = (a * _sigmoid(a) * b).astype(BF16)
        ys_ref[...] = jnp.dot(act, wdb[cur], preferred_element_type=F32)

        @pl.when(e_nxt >= 0)
        def _():
            done = st[1]
            share = (MOE_CHUNKS - done + rem_ref[r] - 1) // rem_ref[r]
            for k in range(MOE_CHUNKS):
                @pl.when(jnp.logical_and(k >= done, k < done + share))
                def _():
                    convert(e_nxt, k, 1 - cur)
            st[1] = done + share

    @pl.when(r >= nu_ref[0])
    def _():
        ys_ref[...] = jnp.zeros_like(ys_ref)


def _moe(plan, xs, wu, wd, layer):
    def row(r, te, nu, *_):
        return (jnp.minimum(r, nu[0] - 1), 0)

    return pl.pallas_call(
        functools.partial(_moe_body, layer=layer),
        out_shape=jax.ShapeDtypeStruct(xs.shape, F32),
        grid_spec=pltpu.PrefetchScalarGridSpec(
            num_scalar_prefetch=5,
            grid=(xs.shape[0] // TM_E,),
            in_specs=[
                pl.BlockSpec((TM_E, D), row),
                pl.BlockSpec(memory_space=pl.ANY),
                pl.BlockSpec(memory_space=pl.ANY),
            ],
            out_specs=pl.BlockSpec((TM_E, D), lambda r, *_: (r, 0)),
            scratch_shapes=[
                pltpu.VMEM((2, D, 2 * DFF), BF16),
                pltpu.VMEM((2, DFF, D), BF16),
                pltpu.VMEM((2, MOE_W_CHUNK, D), F32),
                pltpu.SemaphoreType.DMA((2,)),
                pltpu.SMEM((2,), I32),
            ],
        ),
        compiler_params=_cparams(("arbitrary",), 56),
        name="moe",
    )(*plan, xs, wu, wd)


def _combine_body(pos0_ref, pos1_ref, posn0_ref, posn1_ref, ys_ref, x_ref, w_ref, mod_ref, g_ref, nmod_ref,
                  *rest, final):
    tm = TM_PERM
    outs, (ybuf, sem) = rest[:-2], rest[-2:]
    i = pl.program_id(0)
    slot = i % 2

    def fetch(p0_ref, p1_ref, s):
        def start_row(r):
            _row_copy(ys_ref, p0_ref[0, r], ybuf.at[s, 0], r, sem.at[s]).start()
            _row_copy(ys_ref, p1_ref[0, r], ybuf.at[s, 1], r, sem.at[s]).start()

        _issue_rows(tm, start_row)

    @pl.when(i == 0)
    def _():
        fetch(pos0_ref, pos1_ref, 0)

    @pl.when(i + 1 < pl.num_programs(0))
    def _():
        fetch(posn0_ref, posn1_ref, 1 - slot)

    for k in range(2):
        pltpu.make_async_copy(ys_ref.at[pl.ds(0, tm)], ybuf.at[slot, k], sem.at[slot]).wait()

    reps = D // LANES
    w0 = jnp.tile(w_ref[:, 0:LANES], (1, reps))
    w1 = jnp.tile(w_ref[:, LANES:2 * LANES], (1, reps))
    xn = x_ref[...] + mod_ref[5:6, :] * (w0 * ybuf[slot, 0] + w1 * ybuf[slot, 1])
    if final:
        outs[0][...] = _rms(xn, g_ref[...])
    else:
        outs[0][...] = xn
        outs[1][...] = (_rms(xn, g_ref[...]) * (1.0 + nmod_ref[1:2, :]) + nmod_ref[0:1, :]).astype(BF16)


def _combine(pos, ys, x, wlanes, mods, layer, n_rows, g_next, final):
    tm = TM_PERM
    nt = n_rows // tm
    next_layer = min(layer + 1, DEPTH - 1)
    tile = pl.BlockSpec((tm, D), lambda i: (i, 0))
    out_shape = [jax.ShapeDtypeStruct((n_rows, D), F32)]
    if not final:
        out_shape.append(jax.ShapeDtypeStruct((n_rows, D), BF16))
    return pl.pallas_call(
        functools.partial(_combine_body, final=final),
        out_shape=out_shape,
        grid=(nt,),
        in_specs=_pos_specs(tm, lambda i: i) + _pos_specs(tm, lambda i: jnp.minimum(i + 1, nt - 1)) + [
            pl.BlockSpec(memory_space=pl.ANY),
            tile,
            pl.BlockSpec((tm, 2 * LANES), lambda i: (i, 0)),
            pl.BlockSpec((None, None, 6, D), lambda i: (layer, _mod_row(i, tm), 0, 0)),
            pl.BlockSpec((1, D), lambda i: (0, 0)),
            pl.BlockSpec((None, None, 6, D), lambda i: (next_layer, _mod_row(i, tm), 0, 0)),
        ],
        out_specs=[tile] * len(out_shape),
        scratch_shapes=[pltpu.VMEM((2, 2, tm, D), F32), pltpu.SemaphoreType.DMA((2,))],
        compiler_params=_cparams(("arbitrary",), 52),
        name="combine",
    )(pos, pos, pos, pos, ys, x, wlanes, mods, g_next.reshape(1, D), mods)


def _route_plan(route, cnt, n_rows):
    counts = cnt[:, 0].astype(I32)
    padded = ((counts + TM_E - 1) // TM_E) * TM_E
    ends = jnp.cumsum(padded)
    offs = ends - padded
    e01 = route[0:2].astype(I32)
    eids = jnp.arange(NE, dtype=I32)[:, None, None]
    off01 = jnp.sum(jnp.where(e01[None] == eids, offs[:, None, None], 0), axis=0)
    pos = off01 + route[2:4].astype(I32)
    pos = pos.reshape(2, n_rows // TM_PERM, 1, TM_PERM)
    nt = _n_expert_tiles(n_rows)
    tile_start = jnp.arange(nt, dtype=I32) * TM_E
    tile_expert = jnp.minimum(jnp.sum((tile_start[:, None] >= ends[None, :]).astype(I32), axis=1), NE - 1)
    n_used = (ends[-1] // TM_E).reshape(1).astype(I32)
    spare = n_used[0] + jnp.arange(NE, dtype=I32)
    zstart = jnp.concatenate([offs + (counts // TM_E) * TM_E, jnp.minimum(spare, nt - 1) * TM_E])
    zneed = jnp.concatenate([counts % TM_E != 0, spare < nt]).astype(I32)
    wl = jnp.concatenate([jnp.broadcast_to(route[4][:, None], (n_rows, LANES)),
                          jnp.broadcast_to(route[5][:, None], (n_rows, LANES))], axis=1)
    ntile = padded // TM_E
    ecol = jnp.arange(NE, dtype=I32)
    onehot = tile_expert[:, None] == ecol[None, :]
    pick = lambda v: jnp.sum(jnp.where(onehot, v[None, :], 0), axis=1)
    j_in = jnp.arange(nt, dtype=I32) - pick(offs // TM_E)
    first = (j_in == 0).astype(I32)
    rem = jnp.maximum(pick(ntile) - j_in, 1)
    later = (ecol[None, :] > ecol[:, None]) & (ntile[None, :] > 0)
    nxt_e = jnp.min(jnp.where(later, ecol[None, :], NE), axis=1)
    nxt = pick(jnp.where(nxt_e < NE, nxt_e, -1))
    moe_plan = (tile_expert.astype(I32), n_used, first, nxt.astype(I32), rem.astype(I32))
    return pos, moe_plan, zstart.astype(I32), zneed, wl


def kernel(x, c, ctx, c_ctx, ada_w, ada_b, norm1_g, norm2_g, w_in, attn_sink, sgu_ln_g, sgu_ln_b, sgu_w, sgu_b,
           sconv_w, conf_dw_w, conf_dw_b, conf_ln_g, conf_ln_b, w_branch, w_out, router_w, router_b,
           exp_w_up, exp_w_down, final_g):
    cvec = jnp.concatenate([c, c_ctx[None, :], jnp.zeros((8 - NB - 1, D), F32)], axis=0)
    mods = _ada(cvec, ada_w, ada_b).reshape(DEPTH, 8, 6, D)
    xa, cxa = x.reshape(TX, D), ctx.reshape(TC, D)
    cos, sin = _rope_tables()
    consts = _rope_constants()
    rw = router_w.T.astype(BF16)
    wb_all = w_branch.astype(BF16)
    wo_all = w_out.astype(BF16)

    for l in range(DEPTH):
        last = l == DEPTH - 1
        n_rows = TX if last else T
        n_x_tiles = TX // TM_TOK
        lp = (sgu_ln_g[l], sgu_ln_b[l], sgu_w[l], sgu_b[l], sconv_w[l], conf_dw_w[l], conf_dw_b[l],
              conf_ln_g[l], conf_ln_b[l])
        c_rows0 = 0 if l == 0 else TX

        if l == 0:
            h = _norm_mod(xa, cxa, norm1_g[l], mods, l)
        if not last:
            p = _inproj(h, w_in, l, T, N_IN // TN_IN)
            qr, kr, vr = _prep(p, T, cos, sin, consts)
            ya = _attn(qr, kr, vr, kr, vr, TX // LC, attn_sink[l])
            yac = _ctx_attn(qr, kr, vr, attn_sink[l])
        else:
            p = _inproj(h, w_in, l, TX, N_IN // TN_IN)
            qr, kr, vr = _prep(p, TX, cos, sin, consts)
            kc, vc = _ctx_kv(h, w_in, l, consts[1])
            ya = _attn(qr, kr, vr, kc, vc, 0, attn_sink[l])
            yac = ya
        yr = _mixers(p, n_rows, n_x_tiles, lp)
        xn, h2, lg = _post(p, ya, yac, yr, wb_all, wo_all, xa, cxa, c_rows0 // TM_TOK, mods, l,
                           norm2_g[l], rw, n_rows)
        route, cnt = _route(lg, router_b, n_rows)
        pos, moe_plan, zstart, zneed, wl = _route_plan(route, cnt, n_rows)
        xs = _dispatch(zstart, zneed, pos, h2, n_rows)
        ys = _moe(moe_plan, xs, exp_w_up, exp_w_down, l)
        if last:
            (out,) = _combine(pos, ys, xn, wl, mods, l, n_rows, final_g, True)
        else:
            xa, h = _combine(pos, ys, xn, wl, mods, l, n_rows, norm1_g[l + 1], False)
            cxa = xa

    return out.reshape(NB, S, D)
```
